```python
import math
import jax, jax.numpy as jnp
from jax import lax
import numpy as np

D_MODEL = 1024
BATCH = 8
SEQ = 2048
DEPTH = 1
DEC_BATCH = 16
DEC_SEQ = 64
PAST_LEN = 1024

CHUNK = 64
Q_BLOCK = 128
ATTN_WIDTH = D_MODEL // 2
SSM_WIDTH = D_MODEL - ATTN_WIDTH
N_HEADS = 8
HEAD_DIM = ATTN_WIDTH // N_HEADS
HALF_DIM = HEAD_DIM // 2
ROT_DIM = HALF_DIM // 4
ROPE_THETA = 500000.0
SSM_GROUP = 16
N_SSM_GROUPS = SSM_WIDTH // SSM_GROUP
SSM_STATE = 64
D_FF = 4 * D_MODEL
IN_WIDTH = 3 * ATTN_WIDTH + SSM_WIDTH
LN_EPS = 1e-5
SUBLN_EPS = 1e-5
DEEPNORM_ALPHA = (2 * DEPTH) ** 0.25
DEEPNORM_BETA = (8 * DEPTH) ** -0.25

kernel_name = "hymba_diffattn_s5_streaming_step"


def lambda_init_fn(layer_idx):
    return 0.8 - 0.6 * math.exp(-0.3 * layer_idx)


def layer_norm(x, g, b):
    xf = x.astype(jnp.float32)
    mu = jnp.mean(xf, axis=-1, keepdims=True)
    var = jnp.mean(jnp.square(xf - mu), axis=-1, keepdims=True)
    return ((xf - mu) * lax.rsqrt(var + LN_EPS) * g + b).astype(x.dtype)


def partial_rope(x, pos):
    inv = ROPE_THETA ** (-jnp.arange(0, ROT_DIM, 2, dtype=jnp.float32) / ROT_DIM)
    ang = pos.astype(jnp.float32)[:, None] * inv[None, :]
    cos = jnp.cos(ang)[None, :, None, :]
    sin = jnp.sin(ang)[None, :, None, :]
    xr = x[..., :ROT_DIM].astype(jnp.float32)
    x1, x2 = xr[..., :ROT_DIM // 2], xr[..., ROT_DIM // 2:]
    rot = jnp.concatenate([x1 * cos - x2 * sin, x2 * cos + x1 * sin], axis=-1).astype(x.dtype)
    return jnp.concatenate([rot, x[..., ROT_DIM:]], axis=-1)


def diff_attn_core(q, k, v, q_pos, k_pos, lam, subln_g, lam_init):
    s = jnp.einsum('bqhd,bkhd->bhqk', q, k).astype(jnp.float32) * (HALF_DIM ** -0.5)
    mask = (q_pos // CHUNK)[:, None] >= (k_pos // CHUNK)[None, :]
    s = jnp.where(mask[None, None], s, -jnp.inf)
    a = jax.nn.softmax(s, axis=-1)
    bsz, _, lq, sk = a.shape
    a = a.reshape(bsz, N_HEADS, 2, lq, sk)
    w = (a[:, :, 0] - lam * a[:, :, 1]).astype(v.dtype)
    o = jnp.einsum('bhqk,bkhd->bqhd', w, v).astype(jnp.float32)
    o = o * lax.rsqrt(jnp.mean(o * o, axis=-1, keepdims=True) + SUBLN_EPS) * subln_g
    return (o * (1.0 - lam_init)).astype(v.dtype)


def diff_attn_blocked(q, k, v, pos, lam, subln_g, lam_init):
    bsz, seq = q.shape[0], q.shape[1]
    nb = seq // Q_BLOCK
    qb = q.reshape(bsz, nb, Q_BLOCK, 2 * N_HEADS, HALF_DIM).transpose(1, 0, 2, 3, 4)
    pb = pos.reshape(nb, Q_BLOCK)
    ob = lax.map(lambda qp: diff_attn_core(qp[0], k, v, qp[1], pos, lam, subln_g, lam_init), (qb, pb))
    return ob.transpose(1, 0, 2, 3, 4).reshape(bsz, seq, N_HEADS, HEAD_DIM)


def s5_discretise(a_re, a_im, log_dt, b_re, b_im):
    dt = jnp.exp(log_dt.astype(jnp.float32))[:, None]
    ar = a_re.astype(jnp.float32)
    ai = a_im.astype(jnp.float32)
    mag = jnp.exp(ar * dt)
    lb_re = mag * jnp.cos(ai * dt)
    lb_im = mag * jnp.sin(ai * dt)
    nr = lb_re - 1.0
    ni = lb_im
    den = ar * ar + ai * ai
    f_re = ((nr * ar + ni * ai) / den)[..., None]
    f_im = ((ni * ar - nr * ai) / den)[..., None]
    br = b_re.astype(jnp.float32)
    bi = b_im.astype(jnp.float32)
    bb_re = f_re * br - f_im * bi
    bb_im = f_re * bi + f_im * br
    return lb_re, lb_im, bb_re, bb_im


def _complex_affine_combine(e1, e2):
    a1r, a1i, b1r, b1i = e1
    a2r, a2i, b2r, b2i = e2
    return (a2r * a1r - a2i * a1i,
            a2r * a1i + a2i * a1r,
            a2r * b1r - a2i * b1i + b2r,
            a2r * b1i + a2i * b1r + b2i)


def s5_ssm(u, h0_re, h0_im, a_re, a_im, log_dt, b_re, b_im, c_re, c_im, d):
    lb_re, lb_im, bb_re, bb_im = s5_discretise(a_re, a_im, log_dt, b_re, b_im)
    uf = u.astype(jnp.float32)
    bu_re = jnp.einsum('blgc,gpc->blgp', uf, bb_re)
    bu_im = jnp.einsum('blgc,gpc->blgp', uf, bb_im)
    shape = bu_re.shape
    acum_re, acum_im, h_re, h_im = lax.associative_scan(
        _complex_affine_combine,
        (jnp.broadcast_to(lb_re, shape), jnp.broadcast_to(lb_im, shape), bu_re, bu_im),
        axis=1)
    if h0_re is not None:
        r0 = h0_re.astype(jnp.float32)[:, None]
        i0 = h0_im.astype(jnp.float32)[:, None]
        h_re, h_im = (h_re + acum_re * r0 - acum_im * i0,
                      h_im + acum_re * i0 + acum_im * r0)
    y = (jnp.einsum('blgp,gcp->blgc', h_re, c_re.astype(jnp.float32))
         - jnp.einsum('blgp,gcp->blgc', h_im, c_im.astype(jnp.float32))
         + d.astype(jnp.float32) * uf)
    return y.astype(u.dtype), h_re[:, -1], h_im[:, -1]


def hybrid_layer(x, pos, cache_k, cache_v, h0_re, h0_im, p, lam_init):
    bsz, seq, _ = x.shape
    proj = x @ p['w_in']
    q, k, v, u = jnp.split(proj, [ATTN_WIDTH, 2 * ATTN_WIDTH, 3 * ATTN_WIDTH], axis=-1)
    q = partial_rope(q.reshape(bsz, seq, 2 * N_HEADS, HALF_DIM), pos)
    k = partial_rope(k.reshape(bsz, seq, 2 * N_HEADS, HALF_DIM), pos)
    v = v.reshape(bsz, seq, N_HEADS, HEAD_DIM)
    new_k = k.reshape(bsz, seq, N_HEADS, HEAD_DIM)
    lam = (jnp.exp(jnp.sum(p['lambda_q1'].astype(jnp.float32) * p['lambda_k1'].astype(jnp.float32)))
           - jnp.exp(jnp.sum(p['lambda_q2'].astype(jnp.float32) * p['lambda_k2'].astype(jnp.float32)))
           + lam_init)
    if cache_k is None:
        attn = diff_attn_blocked(q, k, v, pos, lam, p['subln_g'], lam_init)
    else:
        past = cache_k.shape[1]
        k_all = jnp.concatenate([cache_k.reshape(bsz, past, 2 * N_HEADS, HALF_DIM), k], axis=1)
        v_all = jnp.concatenate([cache_v, v], axis=1)
        k_pos = jnp.arange(past + seq, dtype=jnp.int32)
        attn = diff_attn_core(q, k_all, v_all, pos, k_pos, lam, p['subln_g'], lam_init)
    y_ssm, h_re, h_im = s5_ssm(u.reshape(bsz, seq, N_SSM_GROUPS, SSM_GROUP), h0_re, h0_im,
                               p['ssm_a_re'], p['ssm_a_im'], p['ssm_log_dt'], p['ssm_b_re'],
                               p['ssm_b_im'], p['ssm_c_re'], p['ssm_c_im'], p['ssm_d'])
    z = jax.nn.gelu(y_ssm.reshape(bsz, seq, SSM_WIDTH)) @ p['w_glu'] + p['b_glu']
    ssm_out = z[..., :SSM_WIDTH] * jax.nn.sigmoid(z[..., SSM_WIDTH:])
    mix = jnp.concatenate([attn.reshape(bsz, seq, ATTN_WIDTH), ssm_out], axis=-1) @ p['w_out']
    x1 = layer_norm(DEEPNORM_ALPHA * x + mix, p['ln1_g'], p['ln1_b'])
    ff = jnp.square(jax.nn.relu(x1 @ p['w_up'])) @ p['w_down']
    x2 = layer_norm(DEEPNORM_ALPHA * x1 + ff, p['ln2_g'], p['ln2_b'])
    return x2, new_k, v, h_re, h_im


def setup_inputs(seed: int = 0) -> dict:
    key = jax.random.key(seed)
    ks = jax.random.split(key, 32)
    f32 = jnp.float32
    nrm = lambda k, s, sc: jax.random.normal(k, s, f32) * sc
    G, P, C = N_SSM_GROUPS, SSM_STATE, SSM_GROUP
    n_idx = jnp.arange(P, dtype=f32)
    return {
        'x_prompt': nrm(ks[0], (BATCH, SEQ, D_MODEL), 1.0),
        'x_sample': nrm(ks[1], (DEC_BATCH, DEC_SEQ, D_MODEL), 1.0),
        'cache_k': nrm(ks[2], (DEPTH, DEC_BATCH, PAST_LEN, N_HEADS, HEAD_DIM), 1.0),
        'cache_v': nrm(ks[3], (DEPTH, DEC_BATCH, PAST_LEN, N_HEADS, HEAD_DIM), 1.0),
        'state_ssm_re': nrm(ks[4], (DEPTH, DEC_BATCH, G, P), 0.1),
        'state_ssm_im': nrm(ks[5], (DEPTH, DEC_BATCH, G, P), 0.1),
        'w_in': nrm(ks[6], (DEPTH, D_MODEL, IN_WIDTH), D_MODEL ** -0.5),
        'lambda_q1': nrm(ks[7], (DEPTH, HALF_DIM), 0.1),
        'lambda_k1': nrm(ks[8], (DEPTH, HALF_DIM), 0.1),
        'lambda_q2': nrm(ks[9], (DEPTH, HALF_DIM), 0.1),
        'lambda_k2': nrm(ks[10], (DEPTH, HALF_DIM), 0.1),
        'subln_g': 1.0 + nrm(ks[11], (DEPTH, HEAD_DIM), 0.01),
        'ssm_a_re': -0.5 + nrm(ks[12], (DEPTH, G, P), 0.01),
        'ssm_a_im': math.pi * n_idx + nrm(ks[13], (DEPTH, G, P), 0.01),
        'ssm_log_dt': jax.random.uniform(ks[14], (DEPTH, G), f32, math.log(0.001), math.log(0.1)),
        'ssm_b_re': nrm(ks[15], (DEPTH, G, P, C), (2 * C) ** -0.5),
        'ssm_b_im': nrm(ks[16], (DEPTH, G, P, C), (2 * C) ** -0.5),
        'ssm_c_re': nrm(ks[17], (DEPTH, G, C, P), (2 * P) ** -0.5),
        'ssm_c_im': nrm(ks[18], (DEPTH, G, C, P), (2 * P) ** -0.5),
        'ssm_d': nrm(ks[19], (DEPTH, G, C), 1.0),
        'w_glu': nrm(ks[20], (DEPTH, SSM_WIDTH, 2 * SSM_WIDTH), SSM_WIDTH ** -0.5),
        'b_glu': nrm(ks[21], (DEPTH, 2 * SSM_WIDTH), 0.01),
        'w_out': nrm(ks[22], (DEPTH, D_MODEL, D_MODEL), DEEPNORM_BETA * D_MODEL ** -0.5),
        'ln1_g': 1.0 + nrm(ks[23], (DEPTH, D_MODEL), 0.01),
        'ln1_b': nrm(ks[24], (DEPTH, D_MODEL), 0.01),
        'w_up': nrm(ks[25], (DEPTH, D_MODEL, D_FF), D_MODEL ** -0.5),
        'w_down': nrm(ks[26], (DEPTH, D_FF, D_MODEL), DEEPNORM_BETA * D_FF ** -0.5),
        'ln2_g': 1.0 + nrm(ks[27], (DEPTH, D_MODEL), 0.01),
        'ln2_b': nrm(ks[28], (DEPTH, D_MODEL), 0.01),
    }


def reference(x_prompt, x_sample, cache_k, cache_v, state_ssm_re, state_ssm_im,
              w_in, lambda_q1, lambda_k1, lambda_q2, lambda_k2, subln_g,
              ssm_a_re, ssm_a_im, ssm_log_dt, ssm_b_re, ssm_b_im, ssm_c_re, ssm_c_im, ssm_d,
              w_glu, b_glu, w_out, ln1_g, ln1_b, w_up, w_down, ln2_g, ln2_b):
    seq = x_prompt.shape[1]
    past = cache_k.shape[2]
    dec_seq = x_sample.shape[1]
    pos_prompt = jnp.arange(seq, dtype=jnp.int32)
    pos_sample = past + jnp.arange(dec_seq, dtype=jnp.int32)
    hp, hs = x_prompt, x_sample
    kp_l, vp_l, rp_l, ip_l, ks_l, vs_l, rs_l, is_l = [], [], [], [], [], [], [], []
    for l in range(DEPTH):
        p = dict(w_in=w_in[l], lambda_q1=lambda_q1[l], lambda_k1=lambda_k1[l],
                 lambda_q2=lambda_q2[l], lambda_k2=lambda_k2[l], subln_g=subln_g[l],
                 ssm_a_re=ssm_a_re[l], ssm_a_im=ssm_a_im[l], ssm_log_dt=ssm_log_dt[l],
                 ssm_b_re=ssm_b_re[l], ssm_b_im=ssm_b_im[l], ssm_c_re=ssm_c_re[l],
                 ssm_c_im=ssm_c_im[l], ssm_d=ssm_d[l], w_glu=w_glu[l], b_glu=b_glu[l],
                 w_out=w_out[l], ln1_g=ln1_g[l], ln1_b=ln1_b[l], w_up=w_up[l],
                 w_down=w_down[l], ln2_g=ln2_g[l], ln2_b=ln2_b[l])
        lam_init = lambda_init_fn(l)
        hp, kp, vp, rp, ip = hybrid_layer(hp, pos_prompt, None, None, None, None, p, lam_init)
        hs, ksn, vsn, rsn, isn = hybrid_layer(hs, pos_sample, cache_k[l], cache_v[l],
                                              state_ssm_re[l], state_ssm_im[l], p, lam_init)
        kp_l.append(kp); vp_l.append(vp); rp_l.append(rp); ip_l.append(ip)
        ks_l.append(ksn); vs_l.append(vsn); rs_l.append(rsn); is_l.append(isn)
    return (hp, hs,
            jnp.stack(kp_l), jnp.stack(vp_l), jnp.stack(rp_l), jnp.stack(ip_l),
            jnp.stack(ks_l), jnp.stack(vs_l), jnp.stack(rs_l), jnp.stack(is_l))
```

```python
import functools
import math

import numpy as np
import jax
import jax.numpy as jnp
from jax import lax
from jax.experimental import pallas as pl
from jax.experimental.pallas import tpu as pltpu

D_MODEL = 1024
DEPTH = 1
CHUNK = 64
ATTN_WIDTH = 512
SSM_WIDTH = 512
N_HEADS = 8
HEAD_DIM = 64
HALF_DIM = 32
ROT_DIM = 8
ROPE_THETA = 500000.0
SSM_GROUP = 16
N_SSM_GROUPS = 32
SSM_STATE = 64
D_FF = 4 * D_MODEL
LN_EPS = 1e-5
SUBLN_EPS = 1e-5
DEEPNORM_ALPHA = (2 * DEPTH) ** 0.25
QK_SCALE = HALF_DIM ** -0.5

SUBLANES = 8
LANES = 128
SSM_LANES = N_SSM_GROUPS * SSM_STATE
GROUPS_PER_SLAB = LANES // SSM_GROUP
N_SLABS = SSM_WIDTH // LANES
SLAB_STATES = GROUPS_PER_SLAB * SSM_STATE
HEADS_PER_SLAB = LANES // HEAD_DIM
MAPS_PER_SLAB = LANES // HALF_DIM
VMEM_LIMIT_BYTES = 56 * 1024 * 1024

F32 = jnp.float32
BF16 = jnp.bfloat16


def _lambda_init(layer_idx):
    return 0.8 - 0.6 * math.exp(-0.3 * layer_idx)


def _params(semantics):
    return pltpu.CompilerParams(dimension_semantics=semantics, vmem_limit_bytes=VMEM_LIMIT_BYTES)


def _prep_kernel(are_ref, aim_ref, logdt_ref, bre_ref, bim_ref, lq1_ref, lk1_ref, lq2_ref, lk2_ref,
                 lbre_ref, lbim_ref, bbre_ref, bbim_ref, lam_ref, *, lam_init):
    dt = jnp.exp(logdt_ref[...])
    ar = are_ref[...]
    ai = aim_ref[...]
    mag = jnp.exp(ar * dt)
    lb_re = mag * jnp.cos(ai * dt)
    lb_im = mag * jnp.sin(ai * dt)
    nr = lb_re - 1.0
    ni = lb_im
    den = ar * ar + ai * ai
    f_re = (nr * ar + ni * ai) / den
    f_im = (ni * ar - nr * ai) / den
    lbre_ref[...] = lb_re
    lbim_ref[...] = lb_im
    br = bre_ref[...]
    bi = bim_ref[...]
    bbre_ref[...] = f_re[:, None, :] * br - f_im[:, None, :] * bi
    bbim_ref[...] = f_re[:, None, :] * bi + f_im[:, None, :] * br
    s1 = jnp.sum(lq1_ref[...] * lk1_ref[...], axis=-1, keepdims=True)
    s2 = jnp.sum(lq2_ref[...] * lk2_ref[...], axis=-1, keepdims=True)
    lam_ref[...] = jnp.exp(s1) - jnp.exp(s2) + lam_init


def _prep(a_re, a_im, log_dt, b_re, b_im, lq1, lk1, lq2, lk2, lam_init):
    g, p, c = N_SSM_GROUPS, SSM_STATE, SSM_GROUP
    out_shape = (jax.ShapeDtypeStruct((g, p), F32), jax.ShapeDtypeStruct((g, p), F32),
                 jax.ShapeDtypeStruct((g, c, p), F32), jax.ShapeDtypeStruct((g, c, p), F32),
                 jax.ShapeDtypeStruct((1, 1), F32))
    return pl.pallas_call(
        functools.partial(_prep_kernel, lam_init=lam_init),
        out_shape=out_shape, name="prep",
    )(a_re, a_im, log_dt.reshape(g, 1), jnp.swapaxes(b_re, 1, 2), jnp.swapaxes(b_im, 1, 2),
      lq1.reshape(1, HALF_DIM), lk1.reshape(1, HALF_DIM), lq2.reshape(1, HALF_DIM), lk2.reshape(1, HALF_DIM))


def _block_diag_slabs(blocks):
    _, r, c = blocks.shape
    b4 = blocks.reshape(N_SLABS, GROUPS_PER_SLAB, r, c)
    eye = jnp.eye(GROUPS_PER_SLAB, dtype=blocks.dtype)
    out = jnp.einsum('sgrc,gh->sgrhc', b4, eye)
    return out.reshape(N_SLABS, GROUPS_PER_SLAB * r, GROUPS_PER_SLAB * c)


def _rope_tables(positions):
    inv = ROPE_THETA ** (-np.arange(0, ROT_DIM, 2, dtype=np.float64) / ROT_DIM)
    ang = np.asarray(positions, np.float64)[:, None] * inv[None, :]
    r = np.arange(LANES) % HALF_DIM
    half = ROT_DIM // 2
    idx = r % half
    cos = np.where(r[None, :] < ROT_DIM, np.cos(ang)[:, idx], 1.0)
    sin = np.sin(ang)[:, idx]
    s_up = np.where(r[None, :] < half, -sin, 0.0)
    s_dn = np.where((r[None, :] >= half) & (r[None, :] < ROT_DIM), sin, 0.0)
    return (jnp.asarray(cos, F32), jnp.asarray(s_up, F32), jnp.asarray(s_dn, F32))


def _inproj_kernel(x_ref, w_ref, cos_ref, sup_ref, sdn_ref, q_ref, k_ref, v_ref, u_ref, *, nb, tl):
    half = ROT_DIM // 2
    x = x_ref[...].reshape(nb * tl, D_MODEL).astype(BF16)
    cos = jnp.concatenate([cos_ref[...]] * nb, axis=0)
    sup = jnp.concatenate([sup_ref[...]] * nb, axis=0)
    sdn = jnp.concatenate([sdn_ref[...]] * nb, axis=0)

    def rope(t):
        slabs = []
        for j in range(ATTN_WIDTH // LANES):
            s = t[:, j * LANES:(j + 1) * LANES]
            slabs.append(s * cos + pltpu.roll(s, LANES - half, 1) * sup + pltpu.roll(s, half, 1) * sdn)
        return jnp.concatenate(slabs, axis=1)

    q = jnp.dot(x, w_ref[:, 0:ATTN_WIDTH], preferred_element_type=F32)
    q_ref[...] = (rope(q) * QK_SCALE).astype(BF16).reshape(nb, tl, ATTN_WIDTH)
    k = jnp.dot(x, w_ref[:, ATTN_WIDTH:2 * ATTN_WIDTH], preferred_element_type=F32)
    k_ref[...] = rope(k).reshape(nb, tl, ATTN_WIDTH)
    v = jnp.dot(x, w_ref[:, 2 * ATTN_WIDTH:3 * ATTN_WIDTH], preferred_element_type=F32)
    v_ref[...] = v.reshape(nb, tl, ATTN_WIDTH)
    u = jnp.dot(x, w_ref[:, 3 * ATTN_WIDTH:], preferred_element_type=F32)
    for b in range(nb):
        u_ref[:, b * SSM_WIDTH:(b + 1) * SSM_WIDTH] = u[b * tl:(b + 1) * tl, :]


def _inproj(x, w_bf, tables, nb, tl):
    bsz, seq, _ = x.shape
    nblk = bsz // SUBLANES
    per_blk = SUBLANES // nb
    grid = (bsz // nb, seq // tl)
    tab_spec = pl.BlockSpec((tl, LANES), lambda b, i: (i, 0))
    act_spec = lambda width: pl.BlockSpec((nb, tl, width), lambda b, i: (b, i, 0))
    out_shape = (jax.ShapeDtypeStruct((bsz, seq, ATTN_WIDTH), BF16),
                 jax.ShapeDtypeStruct((bsz, seq, ATTN_WIDTH), F32),
                 jax.ShapeDtypeStruct((bsz, seq, ATTN_WIDTH), F32),
                 jax.ShapeDtypeStruct((nblk, seq, SUBLANES * SSM_WIDTH), F32))
    return pl.pallas_call(
        functools.partial(_inproj_kernel, nb=nb, tl=tl),
        grid=grid,
        in_specs=[act_spec(D_MODEL),
                  pl.BlockSpec((D_MODEL, 4 * ATTN_WIDTH), lambda b, i: (0, 0)),
                  tab_spec, tab_spec, tab_spec],
        out_specs=(act_spec(ATTN_WIDTH), act_spec(ATTN_WIDTH), act_spec(ATTN_WIDTH),
                   pl.BlockSpec((None, tl, nb * SSM_WIDTH), lambda b, i: (b // per_blk, i, b % per_blk))),
        out_shape=out_shape,
        compiler_params=_params(("parallel", "parallel")),
        name="inproj",
    )(x, w_bf, *tables)


def _expand_maps(q):
    lane = lax.broadcasted_iota(jnp.int32, q.shape, 1)
    zero = jnp.zeros_like(q)
    return jnp.concatenate(
        [jnp.where((lane >= m * HALF_DIM) & (lane < (m + 1) * HALF_DIM), q, zero) for m in range(MAPS_PER_SLAB)],
        axis=0)


def _diff_combine(o, lam, g, n, lam_init):
    lane = lax.broadcasted_iota(jnp.int32, (n, LANES), 1)
    first = lane < HEAD_DIM
    d = jnp.where(first, o[0:n] - lam * o[n:2 * n], o[2 * n:3 * n] - lam * o[3 * n:4 * n])
    sq = d * d
    ss_a = jnp.sum(jnp.where(first, sq, 0.0), axis=-1, keepdims=True)
    ss_b = jnp.sum(jnp.where(first, 0.0, sq), axis=-1, keepdims=True)
    ms = jnp.where(first, ss_a, ss_b) * (1.0 / HEAD_DIM)
    return d * lax.rsqrt(ms + SUBLN_EPS) * g * (1.0 - lam_init)


def _attn_prompt_kernel(lam_ref, q_ref, k_ref, v_ref, g_ref, o_ref, kb, vb, m_ref, l_ref, acc_ref,
                        *, blk, lam_init):
    i = pl.program_id(2)
    rows = MAPS_PER_SLAB * blk

    @pl.when(i == 0)
    def _():
        kb[...] = k_ref[...].astype(BF16)
        vb[...] = v_ref[...].astype(BF16)

    qx = _expand_maps(q_ref[...])
    m_ref[...] = jnp.full((rows, LANES), -jnp.inf, F32)
    l_ref[...] = jnp.zeros((rows, LANES), F32)
    acc_ref[...] = jnp.zeros((rows, LANES), F32)

    def step(j, masked):
        start = pl.multiple_of(j * blk, blk)
        kblk = kb[pl.ds(start, blk), :]
        vblk = vb[pl.ds(start, blk), :]
        s = lax.dot_general(qx, kblk, (((1,), (1,)), ((), ())), preferred_element_type=F32)
        if masked:
            r = lax.broadcasted_iota(jnp.int32, (rows, blk), 0)
            c = lax.broadcasted_iota(jnp.int32, (rows, blk), 1)
            s = jnp.where((r % blk) // CHUNK >= c // CHUNK, s, -jnp.inf)
        m_prev = m_ref[...]
        m_new = jnp.maximum(m_prev, jnp.max(s, axis=-1, keepdims=True))
        alpha = jnp.exp(m_prev - m_new)
        p = jnp.exp(s - jnp.concatenate([m_new] * (blk // LANES), axis=1))
        l_ref[...] = alpha * l_ref[...] + jnp.sum(p, axis=-1, keepdims=True)
        acc_ref[...] = alpha * acc_ref[...] + jnp.dot(p.astype(BF16), vblk, preferred_element_type=F32)
        m_ref[...] = m_new

    def body(j, carry):
        step(j, False)
        return carry

    lax.fori_loop(0, i, body, 0)
    step(i, True)
    o = acc_ref[...] / l_ref[...]
    o_ref[...] = _diff_combine(o, lam_ref[0, 0], g_ref[...], blk, lam_init).astype(BF16)


def _attn_prompt(lam, q, k, v, g128, blk, lam_init):
    bsz, seq, _ = q.shape
    rows = MAPS_PER_SLAB * blk
    grid = (bsz, ATTN_WIDTH // LANES, seq // blk)
    kv_spec = pl.BlockSpec((None, seq, LANES), lambda b, j, i: (b, 0, j))
    return pl.pallas_call(
        functools.partial(_attn_prompt_kernel, blk=blk, lam_init=lam_init),
        grid=grid,
        in_specs=[pl.BlockSpec(memory_space=pltpu.SMEM),
                  pl.BlockSpec((None, blk, LANES), lambda b, j, i: (b, i, j)),
                  kv_spec, kv_spec,
                  pl.BlockSpec((1, LANES), lambda b, j, i: (0, 0))],
        out_specs=pl.BlockSpec((None, blk, LANES), lambda b, j, i: (b, i, j)),
        out_shape=jax.ShapeDtypeStruct((bsz, seq, ATTN_WIDTH), BF16),
        scratch_shapes=[pltpu.VMEM((seq, LANES), BF16), pltpu.VMEM((seq, LANES), BF16),
                        pltpu.VMEM((rows, LANES), F32), pltpu.VMEM((rows, LANES), F32),
                        pltpu.VMEM((rows, LANES), F32)],
        compiler_params=_params(("parallel", "parallel", "arbitrary")),
        name="attn_prompt",
    )(lam, q, k, v, g128)


def _attn_sample_kernel(lam_ref, q_ref, kc_ref, vc_ref, kn_ref, vn_ref, g_ref, o_ref, *, n, lam_init):
    qx = _expand_maps(q_ref[...])
    dims = (((1,), (1,)), ((), ()))
    s_c = lax.dot_general(qx, kc_ref[...].astype(BF16), dims, preferred_element_type=F32)
    s_n = lax.dot_general(qx, kn_ref[...].astype(BF16), dims, preferred_element_type=F32)
    m = jnp.maximum(jnp.max(s_c, axis=-1, keepdims=True), jnp.max(s_n, axis=-1, keepdims=True))
    p_c = jnp.exp(s_c - m)
    p_n = jnp.exp(s_n - m)
    l = jnp.sum(p_c, axis=-1, keepdims=True) + jnp.sum(p_n, axis=-1, keepdims=True)
    acc = (jnp.dot(p_c.astype(BF16), vc_ref[...].astype(BF16), preferred_element_type=F32)
           + jnp.dot(p_n.astype(BF16), vn_ref[...].astype(BF16), preferred_element_type=F32))
    o_ref[...] = _diff_combine(acc / l, lam_ref[0, 0], g_ref[...], n, lam_init).astype(BF16)


def _attn_sample(lam, q, cache_k, cache_v, k_new, v_new, g128, lam_init):
    bsz, n, _ = q.shape
    past = cache_k.shape[1]
    assert past % CHUNK == 0 and n <= CHUNK
    grid = (bsz, ATTN_WIDTH // LANES)
    cache_spec = pl.BlockSpec((None, past, LANES), lambda b, j: (b, 0, j))
    new_spec = pl.BlockSpec((None, n, LANES), lambda b, j: (b, 0, j))
    return pl.pallas_call(
        functools.partial(_attn_sample_kernel, n=n, lam_init=lam_init),
        grid=grid,
        in_specs=[pl.BlockSpec(memory_space=pltpu.SMEM), new_spec, cache_spec, cache_spec, new_spec, new_spec,
                  pl.BlockSpec((1, LANES), lambda b, j: (0, 0))],
        out_specs=new_spec,
        out_shape=jax.ShapeDtypeStruct((bsz, n, ATTN_WIDTH), BF16),
        compiler_params=_params(("parallel", "parallel")),
        name="attn_sample",
    )(lam, q, cache_k, cache_v, k_new, v_new, g128)


def _ssm_kernel(u_ref, h0re_ref, h0im_ref, are_ref, aim_ref, bre_ref, bim_ref, cre_ref, cim_ref, d_ref,
                wglu_ref, bglu_ref, out_ref, hre_out, him_out, bure, buim, hsre, hsim, hre, him,
                *, tt, chunk_lanes):
    i = pl.program_id(1)

    @pl.when(i == 0)
    def _():
        hre[...] = h0re_ref[...]
        him[...] = h0im_ref[...]

    u = u_ref[...]
    ub = u.astype(BF16)
    for s in range(N_SLABS):
        us = ub[:, s * LANES:(s + 1) * LANES]
        cols = slice(s * SLAB_STATES, (s + 1) * SLAB_STATES)
        bure[:, cols] = jnp.dot(us, bre_ref[s], preferred_element_type=F32)
        buim[:, cols] = jnp.dot(us, bim_ref[s], preferred_element_type=F32)

    for c in range(SSM_LANES // chunk_lanes):
        cols = slice(c * chunk_lanes, (c + 1) * chunk_lanes)
        ar = jnp.broadcast_to(are_ref[:, cols], (SUBLANES, chunk_lanes))
        ai = jnp.broadcast_to(aim_ref[:, cols], (SUBLANES, chunk_lanes))

        def body(t, carry):
            hr, hi = carry
            r = pl.ds(pl.multiple_of(t * SUBLANES, SUBLANES), SUBLANES)
            nr = ar * hr - ai * hi + bure[r, cols]
            ni = ar * hi + ai * hr + buim[r, cols]
            hsre[r, cols] = nr
            hsim[r, cols] = ni
            return nr, ni

        hr, hi = lax.fori_loop(0, tt, body, (hre[:, cols], him[:, cols]), unroll=4)
        hre[:, cols] = hr
        him[:, cols] = hi

    ys = []
    for s in range(N_SLABS):
        cols = slice(s * SLAB_STATES, (s + 1) * SLAB_STATES)
        ys.append(jnp.dot(hsre[:, cols].astype(BF16), cre_ref[s], preferred_element_type=F32)
                  - jnp.dot(hsim[:, cols].astype(BF16), cim_ref[s], preferred_element_type=F32))
    y = jnp.concatenate(ys, axis=1) + d_ref[...] * u
    z = jnp.dot(jax.nn.gelu(y).astype(BF16), wglu_ref[...], preferred_element_type=F32) + bglu_ref[...]
    out_ref[...] = (z[:, :SSM_WIDTH] * jax.nn.sigmoid(z[:, SSM_WIDTH:])).astype(BF16)

    @pl.when(i == pl.num_programs(1) - 1)
    def _():
        hre_out[...] = hre[...]
        him_out[...] = him[...]


def _ssm(u_t, h0_re, h0_im, lb_re, lb_im, bre, bim, cre, cim, d, w_glu, b_glu, tt):
    nblk, rows_total, _ = u_t.shape
    seq = rows_total // SUBLANES
    rows = SUBLANES * tt
    grid = (nblk, seq // tt)
    const = lambda shape: pl.BlockSpec(shape, lambda b, i: (0,) * len(shape))
    state_spec = pl.BlockSpec((SUBLANES, SSM_LANES), lambda b, i: (b, 0))
    return pl.pallas_call(
        functools.partial(_ssm_kernel, tt=tt, chunk_lanes=1024),
        grid=grid,
        in_specs=[pl.BlockSpec((None, rows, SSM_WIDTH), lambda b, i: (b, i, 0)),
                  state_spec, state_spec,
                  const((1, SSM_LANES)), const((1, SSM_LANES)),
                  const((N_SLABS, LANES, SLAB_STATES)), const((N_SLABS, LANES, SLAB_STATES)),
                  const((N_SLABS, SLAB_STATES, LANES)), const((N_SLABS, SLAB_STATES, LANES)),
                  const((1, SSM_WIDTH)), const((SSM_WIDTH, 2 * SSM_WIDTH)), const((1, 2 * SSM_WIDTH))],
        out_specs=(pl.BlockSpec((None, rows, SSM_WIDTH), lambda b, i: (b, i, 0)), state_spec, state_spec),
        out_shape=(jax.ShapeDtypeStruct((nblk, rows_total, SSM_WIDTH), BF16),
                   jax.ShapeDtypeStruct((nblk * SUBLANES, SSM_LANES), F32),
                   jax.ShapeDtypeStruct((nblk * SUBLANES, SSM_LANES), F32)),
        scratch_shapes=[pltpu.VMEM((rows, SSM_LANES), F32) for _ in range(4)]
                       + [pltpu.VMEM((SUBLANES, SSM_LANES), F32) for _ in range(2)],
        compiler_params=_params(("parallel", "arbitrary")),
        name="ssm",
    )(u_t, h0_re, h0_im, lb_re, lb_im, bre, bim, cre, cim, d, w_glu, b_glu)


def _layer_norm(x, g, b):
    mu = jnp.mean(x, axis=-1, keepdims=True)
    xc = x - mu
    var = jnp.mean(xc * xc, axis=-1, keepdims=True)
    return xc * lax.rsqrt(var + LN_EPS) * g + b


def _post_kernel(x_ref, a_ref, s_ref, wout_ref, g1_ref, b1_ref, wup_ref, wdown_ref, g2_ref, b2_ref, o_ref,
                 *, nb, tl, ff_chunk):
    n = nb * tl
    x = x_ref[...].reshape(n, D_MODEL)
    attn = a_ref[...].reshape(n, ATTN_WIDTH)
    ssm = jnp.concatenate([s_ref[:, b * SSM_WIDTH:(b + 1) * SSM_WIDTH] for b in range(nb)], axis=0)
    mix = jnp.dot(jnp.concatenate([attn, ssm], axis=1), wout_ref[...], preferred_element_type=F32)
    x1 = _layer_norm(DEEPNORM_ALPHA * x + mix, g1_ref[...], b1_ref[...])
    x1b = x1.astype(BF16)
    ff = jnp.zeros((n, D_MODEL), F32)
    for c in range(D_FF // ff_chunk):
        h = jnp.dot(x1b, wup_ref[:, c * ff_chunk:(c + 1) * ff_chunk], preferred_element_type=F32)
        h = jnp.square(jnp.maximum(h, 0.0)).astype(BF16)
        ff = ff + jnp.dot(h, wdown_ref[c * ff_chunk:(c + 1) * ff_chunk, :], preferred_element_type=F32)
    x2 = _layer_norm(DEEPNORM_ALPHA * x1 + ff, g2_ref[...], b2_ref[...])
    o_ref[...] = x2.reshape(nb, tl, D_MODEL)


def _post(x, attn, ssm_t, w_out, g1, b1, w_up, w_down, g2, b2, nb, tl):
    bsz, seq, _ = x.shape
    per_blk = SUBLANES // nb
    grid = (bsz // nb, seq // tl)
    act_spec = lambda width: pl.BlockSpec((nb, tl, width), lambda b, i: (b, i, 0))
    const = lambda shape: pl.BlockSpec(shape, lambda b, i: (0,) * len(shape), pipeline_mode=pl.Buffered(1))
    return pl.pallas_call(
        functools.partial(_post_kernel, nb=nb, tl=tl, ff_chunk=1024),
        grid=grid,
        in_specs=[act_spec(D_MODEL), act_spec(ATTN_WIDTH),
                  pl.BlockSpec((None, tl, nb * SSM_WIDTH), lambda b, i: (b // per_blk, i, b % per_blk)),
                  const((D_MODEL, D_MODEL)), const((1, D_MODEL)), const((1, D_MODEL)),
                  const((D_MODEL, D_FF)), const((D_FF, D_MODEL)), const((1, D_MODEL)), const((1, D_MODEL))],
        out_specs=act_spec(D_MODEL),
        out_shape=jax.ShapeDtypeStruct((bsz, seq, D_MODEL), F32),
        compiler_params=_params(("parallel", "parallel")),
        name="post",
    )(x, attn, ssm_t, w_out, g1, b1, w_up, w_down, g2, b2)


def kernel(x_prompt, x_sample, cache_k, cache_v, state_ssm_re, state_ssm_im, w_in, lambda_q1, lambda_k1,
           lambda_q2, lambda_k2, subln_g, ssm_a_re, ssm_a_im, ssm_log_dt, ssm_b_re, ssm_b_im, ssm_c_re,
           ssm_c_im, ssm_d, w_glu, b_glu, w_out, ln1_g, ln1_b, w_up, w_down, ln2_g, ln2_b):
    assert w_in.shape[0] == DEPTH
    l = 0
    lam_init = _lambda_init(l)
    bp, seq, _ = x_prompt.shape
    bs, dec_seq, _ = x_sample.shape
    past = cache_k.shape[2]

    lb_re, lb_im, bbt_re, bbt_im, lam = _prep(ssm_a_re[l], ssm_a_im[l], ssm_log_dt[l], ssm_b_re[l],
                                             ssm_b_im[l], lambda_q1[l], lambda_k1[l], lambda_q2[l],
                                             lambda_k2[l], lam_init)
    lb_re = lb_re.reshape(1, SSM_LANES)
    lb_im = lb_im.reshape(1, SSM_LANES)
    bre = _block_diag_slabs(bbt_re).astype(BF16)
    bim = _block_diag_slabs(bbt_im).astype(BF16)
    cre = _block_diag_slabs(jnp.swapaxes(ssm_c_re[l], 1, 2)).astype(BF16)
    cim = _block_diag_slabs(jnp.swapaxes(ssm_c_im[l], 1, 2)).astype(BF16)
    d = ssm_d[l].reshape(1, SSM_WIDTH)
    g128 = jnp.concatenate([subln_g[l]] * HEADS_PER_SLAB).reshape(1, LANES)
    w_in_bf = w_in[l].astype(BF16)
    w_glu_bf = w_glu[l].astype(BF16)
    w_out_bf = w_out[l].astype(BF16)
    w_up_bf = w_up[l].astype(BF16)
    w_down_bf = w_down[l].astype(BF16)
    b_glu2 = b_glu[l].reshape(1, 2 * SSM_WIDTH)
    ln = [a[l].reshape(1, D_MODEL) for a in (ln1_g, ln1_b, ln2_g, ln2_b)]

    def layer(x, positions, cache, h0_re, h0_im, nb, tl, tt):
        bsz, n, _ = x.shape
        q, k, v, u_t = _inproj(x, w_in_bf, _rope_tables(positions), nb, tl)
        if cache is None:
            attn = _attn_prompt(lam, q, k, v, g128, 256, lam_init)
        else:
            attn = _attn_sample(lam, q, cache[0], cache[1], k, v, g128, lam_init)
        u_rows = u_t.reshape(bsz // SUBLANES, n * SUBLANES, SSM_WIDTH)
        ssm_rows, h_re, h_im = _ssm(u_rows, h0_re, h0_im, lb_re, lb_im, bre, bim, cre, cim, d,
                                    w_glu_bf, b_glu2, tt)
        ssm_t = ssm_rows.reshape(bsz // SUBLANES, n, SUBLANES * SSM_WIDTH)
        y = _post(x, attn, ssm_t, w_out_bf, ln[0], ln[1], w_up_bf, w_down_bf, ln[2], ln[3], nb, tl)
        shape_kv = (1, bsz, n, N_HEADS, HEAD_DIM)
        shape_h = (1, bsz, N_SSM_GROUPS, SSM_STATE)
        return y, k.reshape(shape_kv), v.reshape(shape_kv), h_re.reshape(shape_h), h_im.reshape(shape_h)

    zeros = jnp.zeros((bp, SSM_LANES), F32)
    yp, kp, vp, rp, ip = layer(x_prompt, np.arange(seq), None, zeros, zeros, nb=1, tl=512, tt=64)
    cache = (cache_k[l].reshape(bs, past, ATTN_WIDTH), cache_v[l].reshape(bs, past, ATTN_WIDTH))
    ys, ks, vs, rs, is_ = layer(x_sample, past + np.arange(dec_seq), cache,
                                state_ssm_re[l].reshape(bs, SSM_LANES), state_ssm_im[l].reshape(bs, SSM_LANES),
                                nb=SUBLANES, tl=dec_seq, tt=dec_seq)
    return (yp, ys, kp, vp, rp, ip, ks, vs, rs, is_)
```

```python
import functools
import math

import numpy as np
import jax
import jax.numpy as jnp
from jax import lax
from jax.experimental import pallas as pl
from jax.experimental.pallas import tpu as pltpu

D_MODEL = 1024
DEPTH = 1
CHUNK = 64
ATTN_WIDTH = 512
SSM_WIDTH = 512
N_HEADS = 8
HEAD_DIM = 64
HALF_DIM = 32
ROT_DIM = 8
ROPE_THETA = 500000.0
SSM_GROUP = 16
N_SSM_GROUPS = 32
SSM_STATE = 64
D_FF = 4 * D_MODEL
LN_EPS = 1e-5
SUBLN_EPS = 1e-5
DEEPNORM_ALPHA = (2 * DEPTH) ** 0.25
QK_SCALE = HALF_DIM ** -0.5
LOG2_E = math.log2(math.e)

SUBLANES = 8
LANES = 128
SSM_LANES = N_SSM_GROUPS * SSM_STATE
GROUPS_PER_SLAB = LANES // SSM_GROUP
N_SLABS = SSM_WIDTH // LANES
SLAB_STATES = GROUPS_PER_SLAB * SSM_STATE
HEADS_PER_SLAB = LANES // HEAD_DIM
MAPS_PER_SLAB = LANES // HALF_DIM
VMEM_LIMIT_BYTES = 56 * 1024 * 1024

F32 = jnp.float32
BF16 = jnp.bfloat16


def _lambda_init(layer_idx):
    return 0.8 - 0.6 * math.exp(-0.3 * layer_idx)


def _params(semantics):
    return pltpu.CompilerParams(dimension_semantics=semantics, vmem_limit_bytes=VMEM_LIMIT_BYTES)


def _prep_kernel(are_ref, aim_ref, logdt_ref, bre_ref, bim_ref, lq1_ref, lk1_ref, lq2_ref, lk2_ref,
                 lbre_ref, lbim_ref, bbre_ref, bbim_ref, lam_ref, *, lam_init):
    dt = jnp.exp(logdt_ref[...])
    ar = are_ref[...]
    ai = aim_ref[...]
    mag = jnp.exp(ar * dt)
    lb_re = mag * jnp.cos(ai * dt)
    lb_im = mag * jnp.sin(ai * dt)
    nr = lb_re - 1.0
    ni = lb_im
    den = ar * ar + ai * ai
    f_re = (nr * ar + ni * ai) / den
    f_im = (ni * ar - nr * ai) / den
    lbre_ref[...] = lb_re
    lbim_ref[...] = lb_im
    br = bre_ref[...]
    bi = bim_ref[...]
    bbre_ref[...] = f_re[:, None, :] * br - f_im[:, None, :] * bi
    bbim_ref[...] = f_re[:, None, :] * bi + f_im[:, None, :] * br
    s1 = jnp.sum(lq1_ref[...] * lk1_ref[...], axis=-1, keepdims=True)
    s2 = jnp.sum(lq2_ref[...] * lk2_ref[...], axis=-1, keepdims=True)
    lam_ref[...] = jnp.exp(s1) - jnp.exp(s2) + lam_init


def _prep(a_re, a_im, log_dt, b_re, b_im, lq1, lk1, lq2, lk2, lam_init):
    g, p, c = N_SSM_GROUPS, SSM_STATE, SSM_GROUP
    out_shape = (jax.ShapeDtypeStruct((g, p), F32), jax.ShapeDtypeStruct((g, p), F32),
                 jax.ShapeDtypeStruct((g, c, p), F32), jax.ShapeDtypeStruct((g, c, p), F32),
                 jax.ShapeDtypeStruct((1, 1), F32))
    return pl.pallas_call(
        functools.partial(_prep_kernel, lam_init=lam_init),
        out_shape=out_shape, name="prep",
    )(a_re, a_im, log_dt.reshape(g, 1), jnp.swapaxes(b_re, 1, 2), jnp.swapaxes(b_im, 1, 2),
      lq1.reshape(1, HALF_DIM), lk1.reshape(1, HALF_DIM), lq2.reshape(1, HALF_DIM), lk2.reshape(1, HALF_DIM))


def _block_diag_slabs(blocks):
    _, r, c = blocks.shape
    b4 = blocks.reshape(N_SLABS, GROUPS_PER_SLAB, r, c)
    eye = jnp.eye(GROUPS_PER_SLAB, dtype=blocks.dtype)
    out = jnp.einsum('sgrc,gh->sgrhc', b4, eye)
    return out.reshape(N_SLABS, GROUPS_PER_SLAB * r, GROUPS_PER_SLAB * c)


def _rope_tables(positions):
    inv = ROPE_THETA ** (-np.arange(0, ROT_DIM, 2, dtype=np.float64) / ROT_DIM)
    ang = np.asarray(positions, np.float64)[:, None] * inv[None, :]
    r = np.arange(LANES) % HALF_DIM
    half = ROT_DIM // 2
    idx = r % half
    cos = np.where(r[None, :] < ROT_DIM, np.cos(ang)[:, idx], 1.0)
    sin = np.sin(ang)[:, idx]
    s_up = np.where(r[None, :] < half, -sin, 0.0)
    s_dn = np.where((r[None, :] >= half) & (r[None, :] < ROT_DIM), sin, 0.0)
    return (jnp.asarray(cos, F32), jnp.asarray(s_up, F32), jnp.asarray(s_dn, F32))


def _inproj_kernel(x_ref, w_ref, cos_ref, sup_ref, sdn_ref, q_ref, k_ref, v_ref, u_ref, *, nb, tl):
    half = ROT_DIM // 2
    x = x_ref[...].reshape(nb * tl, D_MODEL).astype(BF16)
    cos = jnp.concatenate([cos_ref[...]] * nb, axis=0)
    sup = jnp.concatenate([sup_ref[...]] * nb, axis=0)
    sdn = jnp.concatenate([sdn_ref[...]] * nb, axis=0)

    def rope(t):
        slabs = []
        for j in range(ATTN_WIDTH // LANES):
            s = t[:, j * LANES:(j + 1) * LANES]
            slabs.append(s * cos + pltpu.roll(s, LANES - half, 1) * sup + pltpu.roll(s, half, 1) * sdn)
        return jnp.concatenate(slabs, axis=1)

    q = jnp.dot(x, w_ref[:, 0:ATTN_WIDTH], preferred_element_type=F32)
    q_ref[...] = (rope(q) * (QK_SCALE * LOG2_E)).astype(BF16).reshape(nb, tl, ATTN_WIDTH)
    k = jnp.dot(x, w_ref[:, ATTN_WIDTH:2 * ATTN_WIDTH], preferred_element_type=F32)
    k_ref[...] = rope(k).reshape(nb, tl, ATTN_WIDTH)
    v = jnp.dot(x, w_ref[:, 2 * ATTN_WIDTH:3 * ATTN_WIDTH], preferred_element_type=F32)
    v_ref[...] = v.reshape(nb, tl, ATTN_WIDTH)
    u = jnp.dot(x, w_ref[:, 3 * ATTN_WIDTH:], preferred_element_type=F32)
    for b in range(nb):
        u_ref[:, b * SSM_WIDTH:(b + 1) * SSM_WIDTH] = u[b * tl:(b + 1) * tl, :]


def _inproj(x, w_bf, tables, nb, tl):
    bsz, seq, _ = x.shape
    nblk = bsz // SUBLANES
    per_blk = SUBLANES // nb
    grid = (bsz // nb, seq // tl)
    tab_spec = pl.BlockSpec((tl, LANES), lambda b, i: (i, 0))
    act_spec = lambda width: pl.BlockSpec((nb, tl, width), lambda b, i: (b, i, 0))
    out_shape = (jax.ShapeDtypeStruct((bsz, seq, ATTN_WIDTH), BF16),
                 jax.ShapeDtypeStruct((bsz, seq, ATTN_WIDTH), F32),
                 jax.ShapeDtypeStruct((bsz, seq, ATTN_WIDTH), F32),
                 jax.ShapeDtypeStruct((nblk, seq, SUBLANES * SSM_WIDTH), F32))
    return pl.pallas_call(
        functools.partial(_inproj_kernel, nb=nb, tl=tl),
        grid=grid,
        in_specs=[act_spec(D_MODEL),
                  pl.BlockSpec((D_MODEL, 4 * ATTN_WIDTH), lambda b, i: (0, 0)),
                  tab_spec, tab_spec, tab_spec],
        out_specs=(act_spec(ATTN_WIDTH), act_spec(ATTN_WIDTH), act_spec(ATTN_WIDTH),
                   pl.BlockSpec((None, tl, nb * SSM_WIDTH), lambda b, i: (b // per_blk, i, b % per_blk))),
        out_shape=out_shape,
        compiler_params=_params(("parallel", "parallel")),
        name="inproj",
    )(x, w_bf, *tables)


def _expand_maps(q):
    lane = lax.broadcasted_iota(jnp.int32, q.shape, 1)
    zero = jnp.zeros_like(q)
    return jnp.concatenate(
        [jnp.where((lane >= m * HALF_DIM) & (lane < (m + 1) * HALF_DIM), q, zero) for m in range(MAPS_PER_SLAB)],
        axis=0)


def _diff_combine(o, lam, g, n, lam_init):
    lane = lax.broadcasted_iota(jnp.int32, (n, LANES), 1)
    first = lane < HEAD_DIM
    d = jnp.where(first, o[0:n] - lam * o[n:2 * n], o[2 * n:3 * n] - lam * o[3 * n:4 * n])
    sq = d * d
    ss_a = jnp.sum(jnp.where(first, sq, 0.0), axis=-1, keepdims=True)
    ss_b = jnp.sum(jnp.where(first, 0.0, sq), axis=-1, keepdims=True)
    ms = jnp.where(first, ss_a, ss_b) * (1.0 / HEAD_DIM)
    return d * lax.rsqrt(ms + SUBLN_EPS) * g * (1.0 - lam_init)


def _attn_prompt_kernel(lam_ref, q_ref, k_ref, v_ref, g_ref, o_ref, kb, vb, qx_ref, s_a, s_b, m_ref, acc_ref,
                        *, blk, lam_init):
    i = pl.program_id(2)
    hb = blk // 2
    half = MAPS_PER_SLAB * hb
    rows = 2 * half
    dims = (((1,), (1,)), ((), ()))

    @pl.when(i == 0)
    def _():
        kb[...] = k_ref[...].astype(BF16)
        vb[:, 0:LANES] = v_ref[...].astype(BF16)
        vb[:, LANES:2 * LANES] = jnp.ones(v_ref.shape, BF16)

    qx_ref[0:half, :] = _expand_maps(q_ref[0:hb, :])
    qx_ref[half:rows, :] = _expand_maps(q_ref[hb:blk, :])
    m_ref[...] = jnp.full((rows, LANES), -jnp.inf, F32)
    acc_ref[...] = jnp.zeros((rows, 2 * LANES), F32)

    def scores(j, dst):
        start = pl.multiple_of(j * blk, blk)
        dst[...] = lax.dot_general(qx_ref[...], kb[pl.ds(start, blk), :], dims, preferred_element_type=F32)

    def update(j, src, h, width, row_offset):
        rs = slice(h * half, (h + 1) * half)
        s = src[rs, 0:width]
        if row_offset is not None:
            r = lax.broadcasted_iota(jnp.int32, (half, width), 0)
            c = lax.broadcasted_iota(jnp.int32, (half, width), 1)
            s = jnp.where((row_offset + r % hb) // CHUNK >= c // CHUNK, s, -jnp.inf)
        m_prev = m_ref[rs, :]
        m_new = jnp.maximum(m_prev, jnp.max(s, axis=-1, keepdims=True))
        alpha = jnp.exp2(m_prev - m_new)
        p = jnp.exp2(s - jnp.concatenate([m_new] * (width // LANES), axis=1)).astype(BF16)
        start = pl.multiple_of(j * blk, blk)
        pv = jnp.dot(p, vb[pl.ds(start, width), :], preferred_element_type=F32)
        acc_ref[rs, :] = jnp.concatenate([alpha, alpha], axis=1) * acc_ref[rs, :] + pv
        m_ref[rs, :] = m_new

    def update_full(j, src):
        update(j, src, 0, blk, None)
        update(j, src, 1, blk, None)

    def update_diag(j, src):
        update(j, src, 0, hb, 0)
        update(j, src, 1, blk, hb)

    scores(0, s_a)

    def pair(jj, carry):
        j = 2 * jj
        scores(j + 1, s_b)
        update_full(j, s_a)
        scores(j + 2, s_a)
        update_full(j + 1, s_b)
        return carry

    lax.fori_loop(0, i // 2, pair, 0)

    @pl.when(i % 2 == 0)
    def _():
        update_diag(i, s_a)

    @pl.when(i % 2 == 1)
    def _():
        scores(i, s_b)
        update_full(i - 1, s_a)
        update_diag(i, s_b)

    for h in range(2):
        acc = acc_ref[h * half:(h + 1) * half, :]
        o = acc[:, 0:LANES] / acc[:, LANES:2 * LANES]
        o_ref[h * hb:(h + 1) * hb, :] = _diff_combine(o, lam_ref[0, 0], g_ref[...], hb, lam_init).astype(BF16)


def _attn_prompt(lam, q, k, v, g128, blk, lam_init):
    bsz, seq, _ = q.shape
    rows = MAPS_PER_SLAB * blk
    grid = (bsz, ATTN_WIDTH // LANES, seq // blk)
    kv_spec = pl.BlockSpec((None, seq, LANES), lambda b, j, i: (b, 0, j))
    return pl.pallas_call(
        functools.partial(_attn_prompt_kernel, blk=blk, lam_init=lam_init),
        grid=grid,
        in_specs=[pl.BlockSpec(memory_space=pltpu.SMEM),
                  pl.BlockSpec((None, blk, LANES), lambda b, j, i: (b, i, j)),
                  kv_spec, kv_spec,
                  pl.BlockSpec((1, LANES), lambda b, j, i: (0, 0))],
        out_specs=pl.BlockSpec((None, blk, LANES), lambda b, j, i: (b, i, j)),
        out_shape=jax.ShapeDtypeStruct((bsz, seq, ATTN_WIDTH), BF16),
        scratch_shapes=[pltpu.VMEM((seq, LANES), BF16), pltpu.VMEM((seq, 2 * LANES), BF16),
                        pltpu.VMEM((rows, LANES), BF16),
                        pltpu.VMEM((rows, blk), F32), pltpu.VMEM((rows, blk), F32),
                        pltpu.VMEM((rows, LANES), F32), pltpu.VMEM((rows, 2 * LANES), F32)],
        compiler_params=_params(("parallel", "parallel", "arbitrary")),
        name="attn_prompt",
    )(lam, q, k, v, g128)


def _attn_sample_kernel(lam_ref, q_ref, kc_ref, vc_ref, kn_ref, vn_ref, g_ref, o_ref, *, n, lam_init):
    qx = _expand_maps(q_ref[...])
    dims = (((1,), (1,)), ((), ()))
    s_c = lax.dot_general(qx, kc_ref[...].astype(BF16), dims, preferred_element_type=F32)
    s_n = lax.dot_general(qx, kn_ref[...].astype(BF16), dims, preferred_element_type=F32)
    m = jnp.maximum(jnp.max(s_c, axis=-1, keepdims=True), jnp.max(s_n, axis=-1, keepdims=True))
    p_c = jnp.exp2(s_c - m)
    p_n = jnp.exp2(s_n - m)
    l = jnp.sum(p_c, axis=-1, keepdims=True) + jnp.sum(p_n, axis=-1, keepdims=True)
    acc = (jnp.dot(p_c.astype(BF16), vc_ref[...].astype(BF16), preferred_element_type=F32)
           + jnp.dot(p_n.astype(BF16), vn_ref[...].astype(BF16), preferred_element_type=F32))
    o_ref[...] = _diff_combine(acc / l, lam_ref[0, 0], g_ref[...], n, lam_init).astype(BF16)


def _attn_sample(lam, q, cache_k, cache_v, k_new, v_new, g128, lam_init):
    bsz, n, _ = q.shape
    past = cache_k.shape[1]
    assert past % CHUNK == 0 and n <= CHUNK
    grid = (bsz, ATTN_WIDTH // LANES)
    cache_spec = pl.BlockSpec((None, past, LANES), lambda b, j: (b, 0, j))
    new_spec = pl.BlockSpec((None, n, LANES), lambda b, j: (b, 0, j))
    return pl.pallas_call(
        functools.partial(_attn_sample_kernel, n=n, lam_init=lam_init),
        grid=grid,
        in_specs=[pl.BlockSpec(memory_space=pltpu.SMEM), new_spec, cache_spec, cache_spec, new_spec, new_spec,
                  pl.BlockSpec((1, LANES), lambda b, j: (0, 0))],
        out_specs=new_spec,
        out_shape=jax.ShapeDtypeStruct((bsz, n, ATTN_WIDTH), BF16),
        compiler_params=_params(("parallel", "parallel")),
        name="attn_sample",
    )(lam, q, cache_k, cache_v, k_new, v_new, g128)


def _ssm_kernel(u_ref, h0re_ref, h0im_ref, are_ref, aim_ref, bre_ref, bim_ref, cre_ref, cim_ref, d_ref,
                wglu_ref, bglu_ref, out_ref, hre_out, him_out, bure, buim, hsre, hsim, hre, him,
                *, tt, chunk_lanes):
    i = pl.program_id(1)

    @pl.when(i == 0)
    def _():
        hre[...] = h0re_ref[...]
        him[...] = h0im_ref[...]

    u = u_ref[...]
    ub = u.astype(BF16)
    for s in range(N_SLABS):
        us = ub[:, s * LANES:(s + 1) * LANES]
        cols = slice(s * SLAB_STATES, (s + 1) * SLAB_STATES)
        bure[:, cols] = jnp.dot(us, bre_ref[s], preferred_element_type=F32)
        buim[:, cols] = jnp.dot(us, bim_ref[s], preferred_element_type=F32)

    for c in range(SSM_LANES // chunk_lanes):
        cols = slice(c * chunk_lanes, (c + 1) * chunk_lanes)
        ar = jnp.broadcast_to(are_ref[:, cols], (SUBLANES, chunk_lanes))
        ai = jnp.broadcast_to(aim_ref[:, cols], (SUBLANES, chunk_lanes))

        def body(t, carry):
            hr, hi = carry
            r = pl.ds(pl.multiple_of(t * SUBLANES, SUBLANES), SUBLANES)
            nr = ar * hr - ai * hi + bure[r, cols]
            ni = ar * hi + ai * hr + buim[r, cols]
            hsre[r, cols] = nr
            hsim[r, cols] = ni
            return nr, ni

        hr, hi = lax.fori_loop(0, tt, body, (hre[:, cols], him[:, cols]), unroll=4)
        hre[:, cols] = hr
        him[:, cols] = hi

    ys = []
    for s in range(N_SLABS):
        cols = slice(s * SLAB_STATES, (s + 1) * SLAB_STATES)
        ys.append(jnp.dot(hsre[:, cols].astype(BF16), cre_ref[s], preferred_element_type=F32)
                  - jnp.dot(hsim[:, cols].astype(BF16), cim_ref[s], preferred_element_type=F32))
    y = jnp.concatenate(ys, axis=1) + d_ref[...] * u
    z = jnp.dot(jax.nn.gelu(y).astype(BF16), wglu_ref[...], preferred_element_type=F32) + bglu_ref[...]
    out_ref[...] = (z[:, :SSM_WIDTH] * jax.nn.sigmoid(z[:, SSM_WIDTH:])).astype(BF16)

    @pl.when(i == pl.num_programs(1) - 1)
    def _():
        hre_out[...] = hre[...]
        him_out[...] = him[...]


def _ssm(u_t, h0_re, h0_im, lb_re, lb_im, bre, bim, cre, cim, d, w_glu, b_glu, tt):
    nblk, rows_total, _ = u_t.shape
    seq = rows_total // SUBLANES
    rows = SUBLANES * tt
    grid = (nblk, seq // tt)
    const = lambda shape: pl.BlockSpec(shape, lambda b, i: (0,) * len(shape))
    state_spec = pl.BlockSpec((SUBLANES, SSM_LANES), lambda b, i: (b, 0))
    return pl.pallas_call(
        functools.partial(_ssm_kernel, tt=tt, chunk_lanes=1024),
        grid=grid,
        in_specs=[pl.BlockSpec((None, rows, SSM_WIDTH), lambda b, i: (b, i, 0)),
                  state_spec, state_spec,
                  const((1, SSM_LANES)), const((1, SSM_LANES)),
                  const((N_SLABS, LANES, SLAB_STATES)), const((N_SLABS, LANES, SLAB_STATES)),
                  const((N_SLABS, SLAB_STATES, LANES)), const((N_SLABS, SLAB_STATES, LANES)),
                  const((1, SSM_WIDTH)), const((SSM_WIDTH, 2 * SSM_WIDTH)), const((1, 2 * SSM_WIDTH))],
        out_specs=(pl.BlockSpec((None, rows, SSM_WIDTH), lambda b, i: (b, i, 0)), state_spec, state_spec),
        out_shape=(jax.ShapeDtypeStruct((nblk, rows_total, SSM_WIDTH), BF16),
                   jax.ShapeDtypeStruct((nblk * SUBLANES, SSM_LANES), F32),
                   jax.ShapeDtypeStruct((nblk * SUBLANES, SSM_LANES), F32)),
        scratch_shapes=[pltpu.VMEM((rows, SSM_LANES), F32) for _ in range(4)]
                       + [pltpu.VMEM((SUBLANES, SSM_LANES), F32) for _ in range(2)],
        compiler_params=_params(("parallel", "arbitrary")),
        name="ssm",
    )(u_t, h0_re, h0_im, lb_re, lb_im, bre, bim, cre, cim, d, w_glu, b_glu)


def _layer_norm(x, g, b):
    mu = jnp.mean(x, axis=-1, keepdims=True)
    xc = x - mu
    var = jnp.mean(xc * xc, axis=-1, keepdims=True)
    return xc * lax.rsqrt(var + LN_EPS) * g + b


def _post_kernel(x_ref, a_ref, s_ref, wout_ref, g1_ref, b1_ref, wup_ref, wdown_ref, g2_ref, b2_ref, o_ref,
                 *, nb, tl, ff_chunk):
    n = nb * tl
    x = x_ref[...].reshape(n, D_MODEL)
    attn = a_ref[...].reshape(n, ATTN_WIDTH)
    ssm = jnp.concatenate([s_ref[:, b * SSM_WIDTH:(b + 1) * SSM_WIDTH] for b in range(nb)], axis=0)
    mix = jnp.dot(jnp.concatenate([attn, ssm], axis=1), wout_ref[...], preferred_element_type=F32)
    x1 = _layer_norm(DEEPNORM_ALPHA * x + mix, g1_ref[...], b1_ref[...])
    x1b = x1.astype(BF16)
    ff = jnp.zeros((n, D_MODEL), F32)
    for c in range(D_FF // ff_chunk):
        h = jnp.dot(x1b, wup_ref[:, c * ff_chunk:(c + 1) * ff_chunk], preferred_element_type=F32)
        h = jnp.square(jnp.maximum(h, 0.0)).astype(BF16)
        ff = ff + jnp.dot(h, wdown_ref[c * ff_chunk:(c + 1) * ff_chunk, :], preferred_element_type=F32)
    x2 = _layer_norm(DEEPNORM_ALPHA * x1 + ff, g2_ref[...], b2_ref[...])
    o_ref[...] = x2.reshape(nb, tl, D_MODEL)


def _post(x, attn, ssm_t, w_out, g1, b1, w_up, w_down, g2, b2, nb, tl):
    bsz, seq, _ = x.shape
    per_blk = SUBLANES // nb
    grid = (bsz // nb, seq // tl)
    act_spec = lambda width: pl.BlockSpec((nb, tl, width), lambda b, i: (b, i, 0))
    const = lambda shape: pl.BlockSpec(shape, lambda b, i: (0,) * len(shape), pipeline_mode=pl.Buffered(1))
    return pl.pallas_call(
        functools.partial(_post_kernel, nb=nb, tl=tl, ff_chunk=1024),
        grid=grid,
        in_specs=[act_spec(D_MODEL), act_spec(ATTN_WIDTH),
                  pl.BlockSpec((None, tl, nb * SSM_WIDTH), lambda b, i: (b // per_blk, i, b % per_blk)),
                  const((D_MODEL, D_MODEL)), const((1, D_MODEL)), const((1, D_MODEL)),
                  const((D_MODEL, D_FF)), const((D_FF, D_MODEL)), const((1, D_MODEL)), const((1, D_MODEL))],
        out_specs=act_spec(D_MODEL),
        out_shape=jax.ShapeDtypeStruct((bsz, seq, D_MODEL), F32),
        compiler_params=_params(("parallel", "parallel")),
        name="post",
    )(x, attn, ssm_t, w_out, g1, b1, w_up, w_down, g2, b2)


def kernel(x_prompt, x_sample, cache_k, cache_v, state_ssm_re, state_ssm_im, w_in, lambda_q1, lambda_k1,
           lambda_q2, lambda_k2, subln_g, ssm_a_re, ssm_a_im, ssm_log_dt, ssm_b_re, ssm_b_im, ssm_c_re,
           ssm_c_im, ssm_d, w_glu, b_glu, w_out, ln1_g, ln1_b, w_up, w_down, ln2_g, ln2_b):
    assert w_in.shape[0] == DEPTH
    l = 0
    lam_init = _lambda_init(l)
    bp, seq, _ = x_prompt.shape
    bs, dec_seq, _ = x_sample.shape
    past = cache_k.shape[2]

    lb_re, lb_im, bbt_re, bbt_im, lam = _prep(ssm_a_re[l], ssm_a_im[l], ssm_log_dt[l], ssm_b_re[l],
                                             ssm_b_im[l], lambda_q1[l], lambda_k1[l], lambda_q2[l],
                                             lambda_k2[l], lam_init)
    lb_re = lb_re.reshape(1, SSM_LANES)
    lb_im = lb_im.reshape(1, SSM_LANES)
    bre = _block_diag_slabs(bbt_re).astype(BF16)
    bim = _block_diag_slabs(bbt_im).astype(BF16)
    cre = _block_diag_slabs(jnp.swapaxes(ssm_c_re[l], 1, 2)).astype(BF16)
    cim = _block_diag_slabs(jnp.swapaxes(ssm_c_im[l], 1, 2)).astype(BF16)
    d = ssm_d[l].reshape(1, SSM_WIDTH)
    g128 = jnp.concatenate([subln_g[l]] * HEADS_PER_SLAB).reshape(1, LANES)
    w_in_bf = w_in[l].astype(BF16)
    w_glu_bf = w_glu[l].astype(BF16)
    w_out_bf = w_out[l].astype(BF16)
    w_up_bf = w_up[l].astype(BF16)
    w_down_bf = w_down[l].astype(BF16)
    b_glu2 = b_glu[l].reshape(1, 2 * SSM_WIDTH)
    ln = [a[l].reshape(1, D_MODEL) for a in (ln1_g, ln1_b, ln2_g, ln2_b)]

    def layer(x, positions, cache, h0_re, h0_im, nb, tl, tt):
        bsz, n, _ = x.shape
        q, k, v, u_t = _inproj(x, w_in_bf, _rope_tables(positions), nb, tl)
        if cache is None:
            attn = _attn_prompt(lam, q, k, v, g128, 512, lam_init)
        else:
            attn = _attn_sample(lam, q, cache[0], cache[1], k, v, g128, lam_init)
        u_rows = u_t.reshape(bsz // SUBLANES, n * SUBLANES, SSM_WIDTH)
        ssm_rows, h_re, h_im = _ssm(u_rows, h0_re, h0_im, lb_re, lb_im, bre, bim, cre, cim, d,
                                    w_glu_bf, b_glu2, tt)
        ssm_t = ssm_rows.reshape(bsz // SUBLANES, n, SUBLANES * SSM_WIDTH)
        y = _post(x, attn, ssm_t, w_out_bf, ln[0], ln[1], w_up_bf, w_down_bf, ln[2], ln[3], nb, tl)
        shape_kv = (1, bsz, n, N_HEADS, HEAD_DIM)
        shape_h = (1, bsz, N_SSM_GROUPS, SSM_STATE)
        return y, k.reshape(shape_kv), v.reshape(shape_kv), h_re.reshape(shape_h), h_im.reshape(shape_h)

    zeros = jnp.zeros((bp, SSM_LANES), F32)
    yp, kp, vp, rp, ip = layer(x_prompt, np.arange(seq), None, zeros, zeros, nb=1, tl=512, tt=64)
    cache = (cache_k[l].reshape(bs, past, ATTN_WIDTH), cache_v[l].reshape(bs, past, ATTN_WIDTH))
    ys, ks, vs, rs, is_ = layer(x_sample, past + np.arange(dec_seq), cache,
                                state_ssm_re[l].reshape(bs, SSM_LANES), state_ssm_im[l].reshape(bs, SSM_LANES),
                                nb=SUBLANES, tl=dec_seq, tt=dec_seq)
    return (yp, ys, kp, vp, rp, ip, ks, vs, rs, is_)
```

```python
import functools
import math

import numpy as np
import jax
import jax.numpy as jnp
from jax import lax
from jax.experimental import pallas as pl
from jax.experimental.pallas import tpu as pltpu

D_MODEL = 1024
DEPTH = 1
CHUNK = 64
ATTN_WIDTH = 512
SSM_WIDTH = 512
N_HEADS = 8
HEAD_DIM = 64
HALF_DIM = 32
ROT_DIM = 8
ROPE_THETA = 500000.0
SSM_GROUP = 16
N_SSM_GROUPS = 32
SSM_STATE = 64
D_FF = 4 * D_MODEL
LN_EPS = 1e-5
SUBLN_EPS = 1e-5
DEEPNORM_ALPHA = (2 * DEPTH) ** 0.25
QK_SCALE = HALF_DIM ** -0.5
LOG2_E = math.log2(math.e)

SUBLANES = 8
LANES = 128
SSM_LANES = N_SSM_GROUPS * SSM_STATE
GROUPS_PER_SLAB = LANES // SSM_GROUP
N_SLABS = SSM_WIDTH // LANES
SLAB_STATES = GROUPS_PER_SLAB * SSM_STATE
HEADS_PER_SLAB = LANES // HEAD_DIM
MAPS_PER_SLAB = LANES // HALF_DIM
VMEM_LIMIT_BYTES = 56 * 1024 * 1024

F32 = jnp.float32
BF16 = jnp.bfloat16


def _lambda_init(layer_idx):
    return 0.8 - 0.6 * math.exp(-0.3 * layer_idx)


def _params(semantics):
    return pltpu.CompilerParams(dimension_semantics=semantics, vmem_limit_bytes=VMEM_LIMIT_BYTES)


def _prep_kernel(are_ref, aim_ref, logdt_ref, bre_ref, bim_ref, lq1_ref, lk1_ref, lq2_ref, lk2_ref,
                 lbre_ref, lbim_ref, bbre_ref, bbim_ref, lam_ref, *, lam_init):
    dt = jnp.exp(logdt_ref[...])
    ar = are_ref[...]
    ai = aim_ref[...]
    mag = jnp.exp(ar * dt)
    lb_re = mag * jnp.cos(ai * dt)
    lb_im = mag * jnp.sin(ai * dt)
    nr = lb_re - 1.0
    ni = lb_im
    den = ar * ar + ai * ai
    f_re = (nr * ar + ni * ai) / den
    f_im = (ni * ar - nr * ai) / den
    lbre_ref[...] = lb_re
    lbim_ref[...] = lb_im
    br = bre_ref[...]
    bi = bim_ref[...]
    bbre_ref[...] = f_re[:, None, :] * br - f_im[:, None, :] * bi
    bbim_ref[...] = f_re[:, None, :] * bi + f_im[:, None, :] * br
    s1 = jnp.sum(lq1_ref[...] * lk1_ref[...], axis=-1, keepdims=True)
    s2 = jnp.sum(lq2_ref[...] * lk2_ref[...], axis=-1, keepdims=True)
    lam_ref[...] = jnp.exp(s1) - jnp.exp(s2) + lam_init


def _prep(a_re, a_im, log_dt, b_re, b_im, lq1, lk1, lq2, lk2, lam_init):
    g, p, c = N_SSM_GROUPS, SSM_STATE, SSM_GROUP
    out_shape = (jax.ShapeDtypeStruct((g, p), F32), jax.ShapeDtypeStruct((g, p), F32),
                 jax.ShapeDtypeStruct((g, c, p), F32), jax.ShapeDtypeStruct((g, c, p), F32),
                 jax.ShapeDtypeStruct((1, 1), F32))
    return pl.pallas_call(
        functools.partial(_prep_kernel, lam_init=lam_init),
        out_shape=out_shape, name="prep",
    )(a_re, a_im, log_dt.reshape(g, 1), jnp.swapaxes(b_re, 1, 2), jnp.swapaxes(b_im, 1, 2),
      lq1.reshape(1, HALF_DIM), lk1.reshape(1, HALF_DIM), lq2.reshape(1, HALF_DIM), lk2.reshape(1, HALF_DIM))


def _block_diag_slabs(blocks):
    _, r, c = blocks.shape
    b4 = blocks.reshape(N_SLABS, GROUPS_PER_SLAB, r, c)
    eye = jnp.eye(GROUPS_PER_SLAB, dtype=blocks.dtype)
    out = jnp.einsum('sgrc,gh->sgrhc', b4, eye)
    return out.reshape(N_SLABS, GROUPS_PER_SLAB * r, GROUPS_PER_SLAB * c)


def _rope_tables(positions):
    inv = ROPE_THETA ** (-np.arange(0, ROT_DIM, 2, dtype=np.float64) / ROT_DIM)
    ang = np.asarray(positions, np.float64)[:, None] * inv[None, :]
    r = np.arange(LANES) % HALF_DIM
    half = ROT_DIM // 2
    idx = r % half
    cos = np.where(r[None, :] < ROT_DIM, np.cos(ang)[:, idx], 1.0)
    sin = np.sin(ang)[:, idx]
    s_up = np.where(r[None, :] < half, -sin, 0.0)
    s_dn = np.where((r[None, :] >= half) & (r[None, :] < ROT_DIM), sin, 0.0)
    return (jnp.asarray(cos, F32), jnp.asarray(s_up, F32), jnp.asarray(s_dn, F32))


def _inproj_kernel(x_ref, w_ref, cos_ref, sup_ref, sdn_ref, q_ref, k_ref, v_ref, u_ref, *, nb, tl):
    half = ROT_DIM // 2
    x = x_ref[...].reshape(nb * tl, D_MODEL).astype(BF16)
    cos = jnp.concatenate([cos_ref[...]] * nb, axis=0)
    sup = jnp.concatenate([sup_ref[...]] * nb, axis=0)
    sdn = jnp.concatenate([sdn_ref[...]] * nb, axis=0)

    def rope(t):
        slabs = []
        for j in range(ATTN_WIDTH // LANES):
            s = t[:, j * LANES:(j + 1) * LANES]
            slabs.append(s * cos + pltpu.roll(s, LANES - half, 1) * sup + pltpu.roll(s, half, 1) * sdn)
        return jnp.concatenate(slabs, axis=1)

    q = jnp.dot(x, w_ref[:, 0:ATTN_WIDTH], preferred_element_type=F32)
    q_ref[...] = (rope(q) * (QK_SCALE * LOG2_E)).astype(BF16).reshape(nb, tl, ATTN_WIDTH)
    k = jnp.dot(x, w_ref[:, ATTN_WIDTH:2 * ATTN_WIDTH], preferred_element_type=F32)
    k_ref[...] = rope(k).reshape(nb, tl, ATTN_WIDTH)
    v = jnp.dot(x, w_ref[:, 2 * ATTN_WIDTH:3 * ATTN_WIDTH], preferred_element_type=F32)
    v_ref[...] = v.reshape(nb, tl, ATTN_WIDTH)
    u = jnp.dot(x, w_ref[:, 3 * ATTN_WIDTH:], preferred_element_type=F32)
    first_slot = (pl.program_id(1) * nb) % SUBLANES
    for b in range(nb):
        for s in range(N_SLABS):
            u_ref[s, pl.ds(first_slot + b, tl, stride=SUBLANES), :] = u[b * tl:(b + 1) * tl,
                                                                        s * LANES:(s + 1) * LANES]


def _inproj(x, w_bf, tables, nb, tl):
    bsz, seq, _ = x.shape
    nblk = bsz // SUBLANES
    per_blk = SUBLANES // nb
    grid = (seq // tl, bsz // nb)
    tab_spec = pl.BlockSpec((tl, LANES), lambda i, b: (i, 0))
    act_spec = lambda width: pl.BlockSpec((nb, tl, width), lambda i, b: (b, i, 0))
    out_shape = (jax.ShapeDtypeStruct((bsz, seq, ATTN_WIDTH), BF16),
                 jax.ShapeDtypeStruct((bsz, seq, ATTN_WIDTH), F32),
                 jax.ShapeDtypeStruct((bsz, seq, ATTN_WIDTH), F32),
                 jax.ShapeDtypeStruct((nblk, N_SLABS, seq * SUBLANES, LANES), F32))
    return pl.pallas_call(
        functools.partial(_inproj_kernel, nb=nb, tl=tl),
        grid=grid,
        in_specs=[act_spec(D_MODEL),
                  pl.BlockSpec((D_MODEL, 4 * ATTN_WIDTH), lambda i, b: (0, 0)),
                  tab_spec, tab_spec, tab_spec],
        out_specs=(act_spec(ATTN_WIDTH), act_spec(ATTN_WIDTH), act_spec(ATTN_WIDTH),
                   pl.BlockSpec((None, N_SLABS, tl * SUBLANES, LANES), lambda i, b: (b // per_blk, 0, i, 0))),
        out_shape=out_shape,
        compiler_params=_params(("parallel", "arbitrary")),
        name="inproj",
    )(x, w_bf, *tables)


def _expand_maps(q):
    lane = lax.broadcasted_iota(jnp.int32, q.shape, 1)
    zero = jnp.zeros_like(q)
    return jnp.concatenate(
        [jnp.where((lane >= m * HALF_DIM) & (lane < (m + 1) * HALF_DIM), q, zero) for m in range(MAPS_PER_SLAB)],
        axis=0)


def _diff_combine(o, lam, g, n, lam_init):
    lane = lax.broadcasted_iota(jnp.int32, (n, LANES), 1)
    first = lane < HEAD_DIM
    d = jnp.where(first, o[0:n] - lam * o[n:2 * n], o[2 * n:3 * n] - lam * o[3 * n:4 * n])
    sq = d * d
    ss_a = jnp.sum(jnp.where(first, sq, 0.0), axis=-1, keepdims=True)
    ss_b = jnp.sum(jnp.where(first, 0.0, sq), axis=-1, keepdims=True)
    ms = jnp.where(first, ss_a, ss_b) * (1.0 / HEAD_DIM)
    return d * lax.rsqrt(ms + SUBLN_EPS) * g * (1.0 - lam_init)


def _attn_prompt_kernel(lam_ref, q_ref, k_ref, v_ref, g_ref, o_ref, kb, vb, qx_ref, s_a, s_b, m_ref, acc_ref,
                        *, blk, lam_init):
    i = pl.program_id(2)
    hb = blk // 2
    half = MAPS_PER_SLAB * hb
    rows = 2 * half
    dims = (((1,), (1,)), ((), ()))

    @pl.when(i == 0)
    def _():
        kb[...] = k_ref[...].astype(BF16)
        vb[:, 0:LANES] = v_ref[...].astype(BF16)
        vb[:, LANES:2 * LANES] = jnp.ones(v_ref.shape, BF16)

    qx_ref[0:half, :] = _expand_maps(q_ref[0:hb, :])
    qx_ref[half:rows, :] = _expand_maps(q_ref[hb:blk, :])
    m_ref[...] = jnp.full((rows, LANES), -jnp.inf, F32)
    acc_ref[...] = jnp.zeros((rows, 2 * LANES), F32)

    def scores(j, dst):
        start = pl.multiple_of(j * blk, blk)
        dst[...] = lax.dot_general(qx_ref[...], kb[pl.ds(start, blk), :], dims, preferred_element_type=F32)

    def update(j, src, h, width, row_offset):
        rs = slice(h * half, (h + 1) * half)
        s = src[rs, 0:width]
        if row_offset is not None:
            r = lax.broadcasted_iota(jnp.int32, (half, width), 0)
            c = lax.broadcasted_iota(jnp.int32, (half, width), 1)
            s = jnp.where((row_offset + r % hb) // CHUNK >= c // CHUNK, s, -jnp.inf)
        m_prev = m_ref[rs, :]
        m_new = jnp.maximum(m_prev, jnp.max(s, axis=-1, keepdims=True))
        alpha = jnp.exp2(m_prev - m_new)
        p = jnp.exp2(s - jnp.concatenate([m_new] * (width // LANES), axis=1)).astype(BF16)
        start = pl.multiple_of(j * blk, blk)
        pv = jnp.dot(p, vb[pl.ds(start, width), :], preferred_element_type=F32)
        acc_ref[rs, :] = jnp.concatenate([alpha, alpha], axis=1) * acc_ref[rs, :] + pv
        m_ref[rs, :] = m_new

    def update_full(j, src):
        update(j, src, 0, blk, None)
        update(j, src, 1, blk, None)

    def update_diag(j, src):
        update(j, src, 0, hb, 0)
        update(j, src, 1, blk, hb)

    scores(0, s_a)

    def pair(jj, carry):
        j = 2 * jj
        scores(j + 1, s_b)
        update_full(j, s_a)
        scores(j + 2, s_a)
        update_full(j + 1, s_b)
        return carry

    lax.fori_loop(0, i // 2, pair, 0)

    @pl.when(i % 2 == 0)
    def _():
        update_diag(i, s_a)

    @pl.when(i % 2 == 1)
    def _():
        scores(i, s_b)
        update_full(i - 1, s_a)
        update_diag(i, s_b)

    for h in range(2):
        acc = acc_ref[h * half:(h + 1) * half, :]
        o = acc[:, 0:LANES] / acc[:, LANES:2 * LANES]
        o_ref[h * hb:(h + 1) * hb, :] = _diff_combine(o, lam_ref[0, 0], g_ref[...], hb, lam_init).astype(BF16)


def _attn_prompt(lam, q, k, v, g128, blk, lam_init):
    bsz, seq, _ = q.shape
    rows = MAPS_PER_SLAB * blk
    grid = (bsz, ATTN_WIDTH // LANES, seq // blk)
    kv_spec = pl.BlockSpec((None, seq, LANES), lambda b, j, i: (b, 0, j))
    return pl.pallas_call(
        functools.partial(_attn_prompt_kernel, blk=blk, lam_init=lam_init),
        grid=grid,
        in_specs=[pl.BlockSpec(memory_space=pltpu.SMEM),
                  pl.BlockSpec((None, blk, LANES), lambda b, j, i: (b, i, j)),
                  kv_spec, kv_spec,
                  pl.BlockSpec((1, LANES), lambda b, j, i: (0, 0))],
        out_specs=pl.BlockSpec((None, blk, LANES), lambda b, j, i: (b, i, j)),
        out_shape=jax.ShapeDtypeStruct((bsz, seq, ATTN_WIDTH), BF16),
        scratch_shapes=[pltpu.VMEM((seq, LANES), BF16), pltpu.VMEM((seq, 2 * LANES), BF16),
                        pltpu.VMEM((rows, LANES), BF16),
                        pltpu.VMEM((rows, blk), F32), pltpu.VMEM((rows, blk), F32),
                        pltpu.VMEM((rows, LANES), F32), pltpu.VMEM((rows, 2 * LANES), F32)],
        compiler_params=_params(("parallel", "parallel", "arbitrary")),
        name="attn_prompt",
    )(lam, q, k, v, g128)


def _attn_sample_kernel(lam_ref, q_ref, kc_ref, vc_ref, kn_ref, vn_ref, g_ref, o_ref, *, n, lam_init):
    qx = _expand_maps(q_ref[...])
    dims = (((1,), (1,)), ((), ()))
    s_c = jnp.dot(qx, kc_ref[...].astype(BF16), preferred_element_type=F32)
    s_n = lax.dot_general(qx, kn_ref[...].astype(BF16), dims, preferred_element_type=F32)
    m = jnp.maximum(jnp.max(s_c, axis=-1, keepdims=True), jnp.max(s_n, axis=-1, keepdims=True))
    p_c = jnp.exp2(s_c - m)
    p_n = jnp.exp2(s_n - m)
    l = jnp.sum(p_c, axis=-1, keepdims=True) + jnp.sum(p_n, axis=-1, keepdims=True)
    acc = (lax.dot_general(p_c.astype(BF16), vc_ref[...].astype(BF16), dims, preferred_element_type=F32)
           + jnp.dot(p_n.astype(BF16), vn_ref[...].astype(BF16), preferred_element_type=F32))
    o_ref[...] = _diff_combine(acc / l, lam_ref[0, 0], g_ref[...], n, lam_init).astype(BF16)


def _attn_sample(lam, q, cache_kt, cache_vt, k_new, v_new, g128, lam_init):
    bsz, n, _ = q.shape
    past = cache_kt.shape[2]
    assert past % CHUNK == 0 and n <= CHUNK
    grid = (bsz, ATTN_WIDTH // LANES)
    cache_spec = pl.BlockSpec((None, LANES, past), lambda b, j: (b, j, 0))
    new_spec = pl.BlockSpec((None, n, LANES), lambda b, j: (b, 0, j))
    return pl.pallas_call(
        functools.partial(_attn_sample_kernel, n=n, lam_init=lam_init),
        grid=grid,
        in_specs=[pl.BlockSpec(memory_space=pltpu.SMEM), new_spec, cache_spec, cache_spec, new_spec, new_spec,
                  pl.BlockSpec((1, LANES), lambda b, j: (0, 0))],
        out_specs=new_spec,
        out_shape=jax.ShapeDtypeStruct((bsz, n, ATTN_WIDTH), BF16),
        compiler_params=_params(("parallel", "parallel")),
        name="attn_sample",
    )(lam, q, cache_kt, cache_vt, k_new, v_new, g128)


def _ssm_kernel(u_ref, h0re_ref, h0im_ref, are_ref, aim_ref, bre_ref, bim_ref, cre_ref, cim_ref, d_ref,
                wglu_ref, bglu_ref, out_ref, hre_out, him_out, bure, buim, hsre, hsim, hre, him, res,
                *, tt, chunk_lanes):
    i = pl.program_id(1)

    @pl.when(i == 0)
    def _():
        hre[...] = h0re_ref[...]
        him[...] = h0im_ref[...]

    for s in range(N_SLABS):
        us = u_ref[s].astype(BF16)
        cols = slice(s * SLAB_STATES, (s + 1) * SLAB_STATES)
        bure[:, cols] = jnp.dot(us, bre_ref[s], preferred_element_type=F32)
        buim[:, cols] = jnp.dot(us, bim_ref[s], preferred_element_type=F32)

    for c in range(SSM_LANES // chunk_lanes):
        cols = slice(c * chunk_lanes, (c + 1) * chunk_lanes)
        ar = jnp.broadcast_to(are_ref[:, cols], (SUBLANES, chunk_lanes))
        ai = jnp.broadcast_to(aim_ref[:, cols], (SUBLANES, chunk_lanes))

        def body(t, carry):
            hr, hi = carry
            r = pl.ds(pl.multiple_of(t * SUBLANES, SUBLANES), SUBLANES)
            nr = ar * hr - ai * hi + bure[r, cols]
            ni = ar * hi + ai * hr + buim[r, cols]
            hsre[r, cols] = nr
            hsim[r, cols] = ni
            return nr, ni

        hr, hi = lax.fori_loop(0, tt, body, (hre[:, cols], him[:, cols]), unroll=4)
        hre[:, cols] = hr
        him[:, cols] = hi

    ys = []
    for s in range(N_SLABS):
        cols = slice(s * SLAB_STATES, (s + 1) * SLAB_STATES)
        ys.append(jnp.dot(hsre[:, cols].astype(BF16), cre_ref[s], preferred_element_type=F32)
                  - jnp.dot(hsim[:, cols].astype(BF16), cim_ref[s], preferred_element_type=F32)
                  + d_ref[:, s * LANES:(s + 1) * LANES] * u_ref[s])
    y = jnp.concatenate(ys, axis=1)
    z = jnp.dot(jax.nn.gelu(y).astype(BF16), wglu_ref[...], preferred_element_type=F32) + bglu_ref[...]
    gated = z[:, :SSM_WIDTH] * jax.nn.sigmoid(z[:, SSM_WIDTH:])
    for s in range(N_SLABS):
        res[s] = gated[:, s * LANES:(s + 1) * LANES]
    for b in range(SUBLANES):
        for s in range(N_SLABS):
            out_ref[b, :, s * LANES:(s + 1) * LANES] = res[s, pl.ds(b, tt, stride=SUBLANES), :].astype(BF16)

    @pl.when(i == pl.num_programs(1) - 1)
    def _():
        hre_out[...] = hre[...]
        him_out[...] = him[...]


def _ssm(u_t, h0_re, h0_im, lb_re, lb_im, bre, bim, cre, cim, d, w_glu, b_glu, tt):
    nblk, _, rows_total, _ = u_t.shape
    seq = rows_total // SUBLANES
    rows = SUBLANES * tt
    grid = (nblk, seq // tt)
    const = lambda shape: pl.BlockSpec(shape, lambda b, i: (0,) * len(shape))
    state_spec = pl.BlockSpec((SUBLANES, SSM_LANES), lambda b, i: (b, 0))
    return pl.pallas_call(
        functools.partial(_ssm_kernel, tt=tt, chunk_lanes=1024),
        grid=grid,
        in_specs=[pl.BlockSpec((None, N_SLABS, rows, LANES), lambda b, i: (b, 0, i, 0)),
                  state_spec, state_spec,
                  const((1, SSM_LANES)), const((1, SSM_LANES)),
                  const((N_SLABS, LANES, SLAB_STATES)), const((N_SLABS, LANES, SLAB_STATES)),
                  const((N_SLABS, SLAB_STATES, LANES)), const((N_SLABS, SLAB_STATES, LANES)),
                  const((1, SSM_WIDTH)), const((SSM_WIDTH, 2 * SSM_WIDTH)), const((1, 2 * SSM_WIDTH))],
        out_specs=(pl.BlockSpec((SUBLANES, tt, SSM_WIDTH), lambda b, i: (b, i, 0)), state_spec, state_spec),
        out_shape=(jax.ShapeDtypeStruct((nblk * SUBLANES, seq, SSM_WIDTH), BF16),
                   jax.ShapeDtypeStruct((nblk * SUBLANES, SSM_LANES), F32),
                   jax.ShapeDtypeStruct((nblk * SUBLANES, SSM_LANES), F32)),
        scratch_shapes=[pltpu.VMEM((rows, SSM_LANES), F32) for _ in range(4)]
                       + [pltpu.VMEM((SUBLANES, SSM_LANES), F32) for _ in range(2)]
                       + [pltpu.VMEM((N_SLABS, rows, LANES), F32)],
        compiler_params=_params(("parallel", "arbitrary")),
        name="ssm",
    )(u_t, h0_re, h0_im, lb_re, lb_im, bre, bim, cre, cim, d, w_glu, b_glu)


def _layer_norm(x, g, b):
    mu = jnp.mean(x, axis=-1, keepdims=True)
    xc = x - mu
    var = jnp.mean(xc * xc, axis=-1, keepdims=True)
    return xc * lax.rsqrt(var + LN_EPS) * g + b


def _post_kernel(x_ref, a_ref, s_ref, wout_ref, g1_ref, b1_ref, wup_ref, wdown_ref, g2_ref, b2_ref, o_ref,
                 *, nb, tl, ff_chunk):
    n = nb * tl
    x = x_ref[...].reshape(n, D_MODEL)
    attn = a_ref[...].reshape(n, ATTN_WIDTH)
    ssm = s_ref[...].reshape(n, SSM_WIDTH)
    mix =jnp.dot(jnp.concatenate([attn, ssm], axis=1), wout_ref[...], preferred_element_type=F32)
    x1 = _layer_norm(DEEPNORM_ALPHA * x + mix, g1_ref[...], b1_ref[...])
    x1b = x1.astype(BF16)
    ff = jnp.zeros((n, D_MODEL), F32)
    for c in range(D_FF // ff_chunk):
        h = jnp.dot(x1b, wup_ref[:, c * ff_chunk:(c + 1) * ff_chunk], preferred_element_type=F32)
        h = jnp.square(jnp.maximum(h, 0.0)).astype(BF16)
        ff = ff + jnp.dot(h, wdown_ref[c * ff_chunk:(c + 1) * ff_chunk, :], preferred_element_type=F32)
    x2 = _layer_norm(DEEPNORM_ALPHA * x1 + ff, g2_ref[...], b2_ref[...])
    o_ref[...] = x2.reshape(nb, tl, D_MODEL)


def _post(x, attn, ssm, w_out, g1, b1, w_up, w_down, g2, b2, nb, tl):
    bsz, seq, _ = x.shape
    grid = (bsz // nb, seq // tl)
    act_spec = lambda width: pl.BlockSpec((nb, tl, width), lambda b, i: (b, i, 0))
    const = lambda shape: pl.BlockSpec(shape, lambda b, i: (0,) * len(shape), pipeline_mode=pl.Buffered(1))
    return pl.pallas_call(
        functools.partial(_post_kernel, nb=nb, tl=tl, ff_chunk=1024),
        grid=grid,
        in_specs=[act_spec(D_MODEL), act_spec(ATTN_WIDTH), act_spec(SSM_WIDTH),
                  const((D_MODEL, D_MODEL)), const((1, D_MODEL)), const((1, D_MODEL)),
                  const((D_MODEL, D_FF)), const((D_FF, D_MODEL)), const((1, D_MODEL)), const((1, D_MODEL))],
        out_specs=act_spec(D_MODEL),
        out_shape=jax.ShapeDtypeStruct((bsz, seq, D_MODEL), F32),
        compiler_params=_params(("parallel", "parallel")),
        name="post",
    )(x, attn, ssm, w_out, g1, b1, w_up, w_down, g2, b2)


def kernel(x_prompt, x_sample, cache_k, cache_v, state_ssm_re, state_ssm_im, w_in, lambda_q1, lambda_k1,
           lambda_q2, lambda_k2, subln_g, ssm_a_re, ssm_a_im, ssm_log_dt, ssm_b_re, ssm_b_im, ssm_c_re,
           ssm_c_im, ssm_d, w_glu, b_glu, w_out, ln1_g, ln1_b, w_up, w_down, ln2_g, ln2_b):
    assert w_in.shape[0] == DEPTH
    l = 0
    lam_init = _lambda_init(l)
    bp, seq, _ = x_prompt.shape
    bs, dec_seq, _ = x_sample.shape
    past = cache_k.shape[2]

    lb_re, lb_im, bbt_re, bbt_im, lam = _prep(ssm_a_re[l], ssm_a_im[l], ssm_log_dt[l], ssm_b_re[l],
                                             ssm_b_im[l], lambda_q1[l], lambda_k1[l], lambda_q2[l],
                                             lambda_k2[l], lam_init)
    lb_re = lb_re.reshape(1, SSM_LANES)
    lb_im = lb_im.reshape(1, SSM_LANES)
    bre = _block_diag_slabs(bbt_re).astype(BF16)
    bim = _block_diag_slabs(bbt_im).astype(BF16)
    cre = _block_diag_slabs(jnp.swapaxes(ssm_c_re[l], 1, 2)).astype(BF16)
    cim = _block_diag_slabs(jnp.swapaxes(ssm_c_im[l], 1, 2)).astype(BF16)
    d = ssm_d[l].reshape(1, SSM_WIDTH)
    g128 = jnp.concatenate([subln_g[l]] * HEADS_PER_SLAB).reshape(1, LANES)
    w_in_bf = w_in[l].astype(BF16)
    w_glu_bf = w_glu[l].astype(BF16)
    w_out_bf = w_out[l].astype(BF16)
    w_up_bf = w_up[l].astype(BF16)
    w_down_bf = w_down[l].astype(BF16)
    b_glu2 = b_glu[l].reshape(1, 2 * SSM_WIDTH)
    ln = [a[l].reshape(1, D_MODEL) for a in (ln1_g, ln1_b, ln2_g, ln2_b)]

    def layer(x, positions, cache, h0_re, h0_im, nb, tl, tt):
        bsz, n, _ = x.shape
        q, k, v, u_t = _inproj(x, w_in_bf, _rope_tables(positions), nb, tl)
        if cache is None:
            attn = _attn_prompt(lam, q, k, v, g128, 512, lam_init)
        else:
            attn = _attn_sample(lam, q, cache[0], cache[1], k, v, g128, lam_init)
        ssm, h_re, h_im = _ssm(u_t, h0_re, h0_im, lb_re, lb_im, bre, bim, cre, cim, d, w_glu_bf, b_glu2, tt)
        y = _post(x, attn, ssm, w_out_bf, ln[0], ln[1], w_up_bf, w_down_bf, ln[2], ln[3], nb, tl)
        shape_kv = (1, bsz, n, N_HEADS, HEAD_DIM)
        shape_h = (1, bsz, N_SSM_GROUPS, SSM_STATE)
        return y, k.reshape(shape_kv), v.reshape(shape_kv), h_re.reshape(shape_h), h_im.reshape(shape_h)

    zeros = jnp.zeros((bp, SSM_LANES), F32)
    yp, kp, vp, rp, ip = layer(x_prompt, np.arange(seq), None, zeros, zeros, nb=1, tl=512, tt=64)
    cache = tuple(jnp.transpose(c[l], (0, 2, 3, 1)).reshape(bs, ATTN_WIDTH, past) for c in (cache_k, cache_v))
    ys, ks, vs, rs, is_ = layer(x_sample, past + np.arange(dec_seq), cache,
                                state_ssm_re[l].reshape(bs, SSM_LANES), state_ssm_im[l].reshape(bs, SSM_LANES),
                                nb=SUBLANES, tl=dec_seq, tt=dec_seq)
    return (yp, ys, kp, vp, rp, ip, ks, vs, rs, is_)
```

```python
import functools
import math

import numpy as np
import jax
import jax.numpy as jnp
from jax import lax
from jax.experimental import pallas as pl
from jax.experimental.pallas import tpu as pltpu

D_MODEL = 1024
DEPTH = 1
CHUNK = 64
ATTN_WIDTH = 512
SSM_WIDTH = 512
N_HEADS = 8
HEAD_DIM = 64
HALF_DIM = 32
ROT_DIM = 8
ROPE_THETA = 500000.0
SSM_GROUP = 16
N_SSM_GROUPS = 32
SSM_STATE = 64
D_FF = 4 * D_MODEL
LN_EPS = 1e-5
SUBLN_EPS = 1e-5
DEEPNORM_ALPHA = (2 * DEPTH) ** 0.25
QK_SCALE = HALF_DIM ** -0.5
LOG2_E = math.log2(math.e)

SUBLANES = 8
LANES = 128
SSM_LANES = N_SSM_GROUPS * SSM_STATE
GROUPS_PER_SLAB = LANES // SSM_GROUP
N_SLABS = SSM_WIDTH // LANES
SLAB_STATES = GROUPS_PER_SLAB * SSM_STATE
HEADS_PER_SLAB = LANES // HEAD_DIM
MAPS_PER_SLAB = LANES // HALF_DIM
VMEM_LIMIT_BYTES = 56 * 1024 * 1024

F32 = jnp.float32
BF16 = jnp.bfloat16


def _lambda_init(layer_idx):
    return 0.8 - 0.6 * math.exp(-0.3 * layer_idx)


def _params(semantics):
    return pltpu.CompilerParams(dimension_semantics=semantics, vmem_limit_bytes=VMEM_LIMIT_BYTES)


def _prep_kernel(are_ref, aim_ref, logdt_ref, bre_ref, bim_ref, lq1_ref, lk1_ref, lq2_ref, lk2_ref,
                 lbre_ref, lbim_ref, bbre_ref, bbim_ref, lam_ref, *, lam_init):
    dt = jnp.exp(logdt_ref[...])
    ar = are_ref[...]
    ai = aim_ref[...]
    mag = jnp.exp(ar * dt)
    lb_re = mag * jnp.cos(ai * dt)
    lb_im = mag * jnp.sin(ai * dt)
    nr = lb_re - 1.0
    ni = lb_im
    den = ar * ar + ai * ai
    f_re = (nr * ar + ni * ai) / den
    f_im = (ni * ar - nr * ai) / den
    lbre_ref[...] = lb_re
    lbim_ref[...] = lb_im
    br = bre_ref[...]
    bi = bim_ref[...]
    bbre_ref[...] = f_re[:, None, :] * br - f_im[:, None, :] * bi
    bbim_ref[...] = f_re[:, None, :] * bi + f_im[:, None, :] * br
    s1 = jnp.sum(lq1_ref[...] * lk1_ref[...], axis=-1, keepdims=True)
    s2 = jnp.sum(lq2_ref[...] * lk2_ref[...], axis=-1, keepdims=True)
    lam_ref[...] = jnp.exp(s1) - jnp.exp(s2) + lam_init


def _prep(a_re, a_im, log_dt, b_re, b_im, lq1, lk1, lq2, lk2, lam_init):
    g, p, c = N_SSM_GROUPS, SSM_STATE, SSM_GROUP
    out_shape = (jax.ShapeDtypeStruct((g, p), F32), jax.ShapeDtypeStruct((g, p), F32),
                 jax.ShapeDtypeStruct((g, c, p), F32), jax.ShapeDtypeStruct((g, c, p), F32),
                 jax.ShapeDtypeStruct((1, 1), F32))
    return pl.pallas_call(
        functools.partial(_prep_kernel, lam_init=lam_init),
        out_shape=out_shape, name="prep",
    )(a_re, a_im, log_dt.reshape(g, 1), jnp.swapaxes(b_re, 1, 2), jnp.swapaxes(b_im, 1, 2),
      lq1.reshape(1, HALF_DIM), lk1.reshape(1, HALF_DIM), lq2.reshape(1, HALF_DIM), lk2.reshape(1, HALF_DIM))


def _block_diag_slabs(blocks):
    _, r, c = blocks.shape
    b4 = blocks.reshape(N_SLABS, GROUPS_PER_SLAB, r, c)
    eye = jnp.eye(GROUPS_PER_SLAB, dtype=blocks.dtype)
    out = jnp.einsum('sgrc,gh->sgrhc', b4, eye)
    return out.reshape(N_SLABS, GROUPS_PER_SLAB * r, GROUPS_PER_SLAB * c)


def _rope_tables(positions):
    inv = ROPE_THETA ** (-np.arange(0, ROT_DIM, 2, dtype=np.float64) / ROT_DIM)
    ang = np.asarray(positions, np.float64)[:, None] * inv[None, :]
    r = np.arange(LANES) % HALF_DIM
    half = ROT_DIM // 2
    idx = r % half
    cos = np.where(r[None, :] < ROT_DIM, np.cos(ang)[:, idx], 1.0)
    sin = np.sin(ang)[:, idx]
    s_up = np.where(r[None, :] < half, -sin, 0.0)
    s_dn = np.where((r[None, :] >= half) & (r[None, :] < ROT_DIM), sin, 0.0)
    return (jnp.asarray(cos, F32), jnp.asarray(s_up, F32), jnp.asarray(s_dn, F32))


def _inproj_kernel(x_ref, w_ref, cos_ref, sup_ref, sdn_ref, q_ref, k_ref, v_ref, u_ref, *, nb, tl):
    half = ROT_DIM // 2
    x = x_ref[...].reshape(nb * tl, D_MODEL).astype(BF16)
    cos = jnp.concatenate([cos_ref[...]] * nb, axis=0)
    sup = jnp.concatenate([sup_ref[...]] * nb, axis=0)
    sdn = jnp.concatenate([sdn_ref[...]] * nb, axis=0)

    def rope(t):
        slabs = []
        for j in range(ATTN_WIDTH // LANES):
            s = t[:, j * LANES:(j + 1) * LANES]
            slabs.append(s * cos + pltpu.roll(s, LANES - half, 1) * sup + pltpu.roll(s, half, 1) * sdn)
        return jnp.concatenate(slabs, axis=1)

    q = jnp.dot(x, w_ref[:, 0:ATTN_WIDTH], preferred_element_type=F32)
    q_ref[...] = (rope(q) * (QK_SCALE * LOG2_E)).astype(BF16).reshape(nb, tl, ATTN_WIDTH)
    k = jnp.dot(x, w_ref[:, ATTN_WIDTH:2 * ATTN_WIDTH], preferred_element_type=F32)
    k_ref[...] = rope(k).reshape(nb, tl, ATTN_WIDTH)
    v = jnp.dot(x, w_ref[:, 2 * ATTN_WIDTH:3 * ATTN_WIDTH], preferred_element_type=F32)
    v_ref[...] = v.reshape(nb, tl, ATTN_WIDTH)
    u = jnp.dot(x, w_ref[:, 3 * ATTN_WIDTH:], preferred_element_type=F32)
    first_slot = (pl.program_id(1) * nb) % SUBLANES
    for b in range(nb):
        for s in range(N_SLABS):
            u_ref[s, pl.ds(first_slot + b, tl, stride=SUBLANES), :] = u[b * tl:(b + 1) * tl,
                                                                        s * LANES:(s + 1) * LANES]


def _inproj(x, w_bf, tables, nb, tl):
    bsz, seq, _ = x.shape
    nblk = bsz // SUBLANES
    per_blk = SUBLANES // nb
    grid = (seq // tl, bsz // nb)
    tab_spec = pl.BlockSpec((tl, LANES), lambda i, b: (i, 0))
    act_spec = lambda width: pl.BlockSpec((nb, tl, width), lambda i, b: (b, i, 0))
    out_shape = (jax.ShapeDtypeStruct((bsz, seq, ATTN_WIDTH), BF16),
                 jax.ShapeDtypeStruct((bsz, seq, ATTN_WIDTH), F32),
                 jax.ShapeDtypeStruct((bsz, seq, ATTN_WIDTH), F32),
                 jax.ShapeDtypeStruct((nblk, N_SLABS, seq * SUBLANES, LANES), F32))
    return pl.pallas_call(
        functools.partial(_inproj_kernel, nb=nb, tl=tl),
        grid=grid,
        in_specs=[act_spec(D_MODEL),
                  pl.BlockSpec((D_MODEL, 4 * ATTN_WIDTH), lambda i, b: (0, 0)),
                  tab_spec, tab_spec, tab_spec],
        out_specs=(act_spec(ATTN_WIDTH), act_spec(ATTN_WIDTH), act_spec(ATTN_WIDTH),
                   pl.BlockSpec((None, N_SLABS, tl * SUBLANES, LANES), lambda i, b: (b // per_blk, 0, i, 0))),
        out_shape=out_shape,
        compiler_params=_params(("parallel", "arbitrary")),
        name="inproj",
    )(x, w_bf, *tables)


def _expand_maps(q):
    lane = lax.broadcasted_iota(jnp.int32, q.shape, 1)
    zero = jnp.zeros_like(q)
    return jnp.concatenate(
        [jnp.where((lane >= m * HALF_DIM) & (lane < (m + 1) * HALF_DIM), q, zero) for m in range(MAPS_PER_SLAB)],
        axis=0)


def _diff_combine(o, lam, g, n, lam_init):
    lane = lax.broadcasted_iota(jnp.int32, (n, LANES), 1)
    first = lane < HEAD_DIM
    d = jnp.where(first, o[0:n] - lam * o[n:2 * n], o[2 * n:3 * n] - lam * o[3 * n:4 * n])
    sq = d * d
    ss_a = jnp.sum(jnp.where(first, sq, 0.0), axis=-1, keepdims=True)
    ss_b = jnp.sum(jnp.where(first, 0.0, sq), axis=-1, keepdims=True)
    ms = jnp.where(first, ss_a, ss_b) * (1.0 / HEAD_DIM)
    return d * lax.rsqrt(ms + SUBLN_EPS) * g * (1.0 - lam_init)


def _attn_prompt_kernel(lam_ref, q_ref, k_ref, v_ref, g_ref, o_ref, kb, vb, qx_ref, s_a, s_b, m_ref, acc_ref,
                        *, blk, lam_init):
    i = pl.program_id(2)
    hb = blk // 2
    half = MAPS_PER_SLAB * hb
    rows = 2 * half
    dims = (((1,), (1,)), ((), ()))

    @pl.when(i == 0)
    def _():
        kb[...] = k_ref[...].astype(BF16)
        vb[:, 0:LANES] = v_ref[...].astype(BF16)
        vb[:, LANES:2 * LANES] = jnp.ones(v_ref.shape, BF16)

    qx_ref[0:half, :] = _expand_maps(q_ref[0:hb, :])
    qx_ref[half:rows, :] = _expand_maps(q_ref[hb:blk, :])
    m_ref[...] = jnp.full((rows, LANES), -jnp.inf, F32)
    acc_ref[...] = jnp.zeros((rows, 2 * LANES), F32)

    def scores(j, dst):
        start = pl.multiple_of(j * blk, blk)
        dst[...] = lax.dot_general(qx_ref[...], kb[pl.ds(start, blk), :], dims, preferred_element_type=F32)

    def update(j, src, h, width, row_offset):
        rs = slice(h * half, (h + 1) * half)
        s = src[rs, 0:width]
        if row_offset is not None:
            r = lax.broadcasted_iota(jnp.int32, (half, width), 0)
            c = lax.broadcasted_iota(jnp.int32, (half, width), 1)
            s = jnp.where((row_offset + r % hb) // CHUNK >= c // CHUNK, s, -jnp.inf)
        m_prev = m_ref[rs, :]
        m_new = jnp.maximum(m_prev, jnp.max(s, axis=-1, keepdims=True))
        alpha = jnp.exp2(m_prev - m_new)
        p = jnp.exp2(s - jnp.concatenate([m_new] * (width // LANES), axis=1)).astype(BF16)
        start = pl.multiple_of(j * blk, blk)
        pv = jnp.dot(p, vb[pl.ds(start, width), :], preferred_element_type=F32)
        acc_ref[rs, :] = jnp.concatenate([alpha, alpha], axis=1) * acc_ref[rs, :] + pv
        m_ref[rs, :] = m_new

    def update_full(j, src):
        update(j, src, 0, blk, None)
        update(j, src, 1, blk, None)

    def update_diag(j, src):
        update(j, src, 0, hb, 0)
        update(j, src, 1, blk, hb)

    scores(0, s_a)

    def pair(jj, carry):
        j = 2 * jj
        scores(j + 1, s_b)
        update_full(j, s_a)
        scores(j + 2, s_a)
        update_full(j + 1, s_b)
        return carry

    lax.fori_loop(0, i // 2, pair, 0)

    @pl.when(i % 2 == 0)
    def _():
        update_diag(i, s_a)

    @pl.when(i % 2 == 1)
    def _():
        scores(i, s_b)
        update_full(i - 1, s_a)
        update_diag(i, s_b)

    for h in range(2):
        acc = acc_ref[h * half:(h + 1) * half, :]
        o = acc[:, 0:LANES] / acc[:, LANES:2 * LANES]
        o_ref[h * hb:(h + 1) * hb, :] = _diff_combine(o, lam_ref[0, 0], g_ref[...], hb, lam_init).astype(BF16)


def _attn_prompt(lam, q, k, v, g128, blk, lam_init):
    bsz, seq, _ = q.shape
    rows = MAPS_PER_SLAB * blk
    grid = (bsz, ATTN_WIDTH // LANES, seq // blk)
    kv_spec = pl.BlockSpec((None, seq, LANES), lambda b, j, i: (b, 0, j))
    return pl.pallas_call(
        functools.partial(_attn_prompt_kernel, blk=blk, lam_init=lam_init),
        grid=grid,
        in_specs=[pl.BlockSpec(memory_space=pltpu.SMEM),
                  pl.BlockSpec((None, blk, LANES), lambda b, j, i: (b, i, j)),
                  kv_spec, kv_spec,
                  pl.BlockSpec((1, LANES), lambda b, j, i: (0, 0))],
        out_specs=pl.BlockSpec((None, blk, LANES), lambda b, j, i: (b, i, j)),
        out_shape=jax.ShapeDtypeStruct((bsz, seq, ATTN_WIDTH), BF16),
        scratch_shapes=[pltpu.VMEM((seq, LANES), BF16), pltpu.VMEM((seq, 2 * LANES), BF16),
                        pltpu.VMEM((rows, LANES), BF16),
                        pltpu.VMEM((rows, blk), F32), pltpu.VMEM((rows, blk), F32),
                        pltpu.VMEM((rows, LANES), F32), pltpu.VMEM((rows, 2 * LANES), F32)],
        compiler_params=_params(("parallel", "parallel", "arbitrary")),
        name="attn_prompt",
    )(lam, q, k, v, g128)


def _attn_sample_kernel(lam_ref, q_ref, kc_ref, vc_ref, kn_ref, vn_ref, g_ref, o_ref, *, n, lam_init):
    dims = (((1,), (1,)), ((), ()))
    for j in range(ATTN_WIDTH // LANES):
        cols = slice(j * LANES, (j + 1) * LANES)
        qx = _expand_maps(q_ref[:, cols])
        s_c = jnp.dot(qx, kc_ref[cols, :].astype(BF16), preferred_element_type=F32)
        s_n = lax.dot_general(qx, kn_ref[:, cols].astype(BF16), dims, preferred_element_type=F32)
        m = jnp.maximum(jnp.max(s_c, axis=-1, keepdims=True), jnp.max(s_n, axis=-1, keepdims=True))
        p_c = jnp.exp2(s_c - m)
        p_n = jnp.exp2(s_n - m)
        l = jnp.sum(p_c, axis=-1, keepdims=True) + jnp.sum(p_n, axis=-1, keepdims=True)
        acc = (lax.dot_general(p_c.astype(BF16), vc_ref[cols, :].astype(BF16), dims, preferred_element_type=F32)
               + jnp.dot(p_n.astype(BF16), vn_ref[:, cols].astype(BF16), preferred_element_type=F32))
        o_ref[:, cols] = _diff_combine(acc / l, lam_ref[0, 0], g_ref[...], n, lam_init).astype(BF16)


def _attn_sample(lam, q, cache_kt, cache_vt, k_new, v_new, g128, lam_init):
    bsz, n, _ = q.shape
    past = cache_kt.shape[2]
    assert past % CHUNK == 0 and n <= CHUNK
    cache_spec = pl.BlockSpec((None, ATTN_WIDTH, past), lambda b: (b, 0, 0))
    new_spec = pl.BlockSpec((None, n, ATTN_WIDTH), lambda b: (b, 0, 0))
    return pl.pallas_call(
        functools.partial(_attn_sample_kernel, n=n, lam_init=lam_init),
        grid=(bsz,),
        in_specs=[pl.BlockSpec(memory_space=pltpu.SMEM), new_spec, cache_spec, cache_spec, new_spec, new_spec,
                  pl.BlockSpec((1, LANES), lambda b: (0, 0))],
        out_specs=new_spec,
        out_shape=jax.ShapeDtypeStruct((bsz, n, ATTN_WIDTH), BF16),
        compiler_params=_params(("parallel",)),
        name="attn_sample",
    )(lam, q, cache_kt, cache_vt, k_new, v_new, g128)


def _ssm_kernel(u_ref, h0re_ref, h0im_ref, are_ref, aim_ref, bre_ref, bim_ref, cre_ref, cim_ref, d_ref,
                wglu_ref, bglu_ref, out_ref, hre_out, him_out, bure, buim, hsre, hsim, hre, him, res,
                *, tt, chunk_lanes):
    i = pl.program_id(1)

    @pl.when(i == 0)
    def _():
        hre[...] = h0re_ref[...]
        him[...] = h0im_ref[...]

    ys = []
    for s in range(N_SLABS):
        us = u_ref[s].astype(BF16)
        cols = slice(s * SLAB_STATES, (s + 1) * SLAB_STATES)
        bure[:, cols] = jnp.dot(us, bre_ref[s], preferred_element_type=F32)
        buim[:, cols] = jnp.dot(us, bim_ref[s], preferred_element_type=F32)
        ar = jnp.broadcast_to(are_ref[:, cols], (SUBLANES, SLAB_STATES))
        ai = jnp.broadcast_to(aim_ref[:, cols], (SUBLANES, SLAB_STATES))
        hr = hre[:, cols]
        hi = him[:, cols]
        for t in range(tt):
            r = slice(t * SUBLANES, (t + 1) * SUBLANES)
            hr, hi = ar * hr - ai * hi + bure[r, cols], ar * hi + ai * hr + buim[r, cols]
            hsre[r, cols] = hr
            hsim[r, cols] = hi
        hre[:, cols] = hr
        him[:, cols] = hi
        ys.append(jnp.dot(hsre[:, cols].astype(BF16), cre_ref[s], preferred_element_type=F32)
                  - jnp.dot(hsim[:, cols].astype(BF16), cim_ref[s], preferred_element_type=F32)
                  + d_ref[:, s * LANES:(s + 1) * LANES] * u_ref[s])
    y = jnp.concatenate(ys, axis=1)
    z = jnp.dot(jax.nn.gelu(y).astype(BF16), wglu_ref[...], preferred_element_type=F32) + bglu_ref[...]
    gated = z[:, :SSM_WIDTH] * jax.nn.sigmoid(z[:, SSM_WIDTH:])
    for s in range(N_SLABS):
        res[s] = gated[:, s * LANES:(s + 1) * LANES]
    for b in range(SUBLANES):
        for s in range(N_SLABS):
            out_ref[b, :, s * LANES:(s + 1) * LANES] = res[s, pl.ds(b, tt, stride=SUBLANES), :].astype(BF16)

    @pl.when(i == pl.num_programs(1) - 1)
    def _():
        hre_out[...] = hre[...]
        him_out[...] = him[...]


def _ssm(u_t, h0_re, h0_im, lb_re, lb_im, bre, bim, cre, cim, d, w_glu, b_glu, tt):
    nblk, _, rows_total, _ = u_t.shape
    seq = rows_total // SUBLANES
    rows = SUBLANES * tt
    grid = (nblk, seq // tt)
    const = lambda shape: pl.BlockSpec(shape, lambda b, i: (0,) * len(shape))
    state_spec = pl.BlockSpec((SUBLANES, SSM_LANES), lambda b, i: (b, 0))
    return pl.pallas_call(
        functools.partial(_ssm_kernel, tt=tt, chunk_lanes=1024),
        grid=grid,
        in_specs=[pl.BlockSpec((None, N_SLABS, rows, LANES), lambda b, i: (b, 0, i, 0)),
                  state_spec, state_spec,
                  const((1, SSM_LANES)), const((1, SSM_LANES)),
                  const((N_SLABS, LANES, SLAB_STATES)), const((N_SLABS, LANES, SLAB_STATES)),
                  const((N_SLABS, SLAB_STATES, LANES)), const((N_SLABS, SLAB_STATES, LANES)),
                  const((1, SSM_WIDTH)), const((SSM_WIDTH, 2 * SSM_WIDTH)), const((1, 2 * SSM_WIDTH))],
        out_specs=(pl.BlockSpec((SUBLANES, tt, SSM_WIDTH), lambda b, i: (b, i, 0)), state_spec, state_spec),
        out_shape=(jax.ShapeDtypeStruct((nblk * SUBLANES, seq, SSM_WIDTH), BF16),
                   jax.ShapeDtypeStruct((nblk * SUBLANES, SSM_LANES), F32),
                   jax.ShapeDtypeStruct((nblk * SUBLANES, SSM_LANES), F32)),
        scratch_shapes=[pltpu.VMEM((rows, SSM_LANES), F32) for _ in range(4)]
                       + [pltpu.VMEM((SUBLANES, SSM_LANES), F32) for _ in range(2)]
                       + [pltpu.VMEM((N_SLABS, rows, LANES), F32)],
        compiler_params=_params(("parallel", "arbitrary")),
        name="ssm",
    )(u_t, h0_re, h0_im, lb_re, lb_im, bre, bim, cre, cim, d, w_glu, b_glu)


def _layer_norm(x, g, b):
    mu = jnp.mean(x, axis=-1, keepdims=True)
    xc = x - mu
    var = jnp.mean(xc * xc, axis=-1, keepdims=True)
    return xc * lax.rsqrt(var + LN_EPS) * g + b


def _post_kernel(x_ref, a_ref, s_ref, wout_ref, g1_ref, b1_ref, wup_ref, wdown_ref, g2_ref, b2_ref, o_ref,
                 *, nb, tl, ff_chunk):
    n = nb * tl
    x = x_ref[...].reshape(n, D_MODEL)
    attn = a_ref[...].reshape(n, ATTN_WIDTH)
    ssm = s_ref[...].reshape(n, SSM_WIDTH)
    mix =jnp.dot(jnp.concatenate([attn, ssm], axis=1), wout_ref[...], preferred_element_type=F32)
    x1 = _layer_norm(DEEPNORM_ALPHA * x + mix, g1_ref[...], b1_ref[...])
    x1b = x1.astype(BF16)
    ff = jnp.zeros((n, D_MODEL), F32)
    for c in range(D_FF // ff_chunk):
        h = jnp.dot(x1b, wup_ref[:, c * ff_chunk:(c + 1) * ff_chunk], preferred_element_type=F32)
        h = jnp.square(jnp.maximum(h, 0.0)).astype(BF16)
        ff = ff + jnp.dot(h, wdown_ref[c * ff_chunk:(c + 1) * ff_chunk, :], preferred_element_type=F32)
    x2 = _layer_norm(DEEPNORM_ALPHA * x1 + ff, g2_ref[...], b2_ref[...])
    o_ref[...] = x2.reshape(nb, tl, D_MODEL)


def _post(x, attn, ssm, w_out, g1, b1, w_up, w_down, g2, b2, nb, tl):
    bsz, seq, _ = x.shape
    grid = (bsz // nb, seq // tl)
    act_spec = lambda width: pl.BlockSpec((nb, tl, width), lambda b, i: (b, i, 0))
    const = lambda shape: pl.BlockSpec(shape, lambda b, i: (0,) * len(shape), pipeline_mode=pl.Buffered(1))
    return pl.pallas_call(
        functools.partial(_post_kernel, nb=nb, tl=tl, ff_chunk=1024),
        grid=grid,
        in_specs=[act_spec(D_MODEL), act_spec(ATTN_WIDTH), act_spec(SSM_WIDTH),
                  const((D_MODEL, D_MODEL)), const((1, D_MODEL)), const((1, D_MODEL)),
                  const((D_MODEL, D_FF)), const((D_FF, D_MODEL)), const((1, D_MODEL)), const((1, D_MODEL))],
        out_specs=act_spec(D_MODEL),
        out_shape=jax.ShapeDtypeStruct((bsz, seq, D_MODEL), F32),
        compiler_params=_params(("parallel", "parallel")),
        name="post",
    )(x, attn, ssm, w_out, g1, b1, w_up, w_down, g2, b2)


def kernel(x_prompt, x_sample, cache_k, cache_v, state_ssm_re, state_ssm_im, w_in, lambda_q1, lambda_k1,
           lambda_q2, lambda_k2, subln_g, ssm_a_re, ssm_a_im, ssm_log_dt, ssm_b_re, ssm_b_im, ssm_c_re,
           ssm_c_im, ssm_d, w_glu, b_glu, w_out, ln1_g, ln1_b, w_up, w_down, ln2_g, ln2_b):
    assert w_in.shape[0] == DEPTH
    l = 0
    lam_init = _lambda_init(l)
    bp, seq, _ = x_prompt.shape
    bs, dec_seq, _ = x_sample.shape
    past = cache_k.shape[2]

    lb_re, lb_im, bbt_re, bbt_im, lam = _prep(ssm_a_re[l], ssm_a_im[l], ssm_log_dt[l], ssm_b_re[l],
                                             ssm_b_im[l], lambda_q1[l], lambda_k1[l], lambda_q2[l],
                                             lambda_k2[l], lam_init)
    lb_re = lb_re.reshape(1, SSM_LANES)
    lb_im = lb_im.reshape(1, SSM_LANES)
    bre = _block_diag_slabs(bbt_re).astype(BF16)
    bim = _block_diag_slabs(bbt_im).astype(BF16)
    cre = _block_diag_slabs(jnp.swapaxes(ssm_c_re[l], 1, 2)).astype(BF16)
    cim = _block_diag_slabs(jnp.swapaxes(ssm_c_im[l], 1, 2)).astype(BF16)
    d = ssm_d[l].reshape(1, SSM_WIDTH)
    g128 = jnp.concatenate([subln_g[l]] * HEADS_PER_SLAB).reshape(1, LANES)
    w_in_bf = w_in[l].astype(BF16)
    w_glu_bf = w_glu[l].astype(BF16)
    w_out_bf = w_out[l].astype(BF16)
    w_up_bf = w_up[l].astype(BF16)
    w_down_bf = w_down[l].astype(BF16)
    b_glu2 = b_glu[l].reshape(1, 2 * SSM_WIDTH)
    ln = [a[l].reshape(1, D_MODEL) for a in (ln1_g, ln1_b, ln2_g, ln2_b)]

    def layer(x, positions, cache, h0_re, h0_im, nb, tl, tt):
        bsz, n, _ = x.shape
        q, k, v, u_t = _inproj(x, w_in_bf, _rope_tables(positions), nb, tl)
        if cache is None:
            attn = _attn_prompt(lam, q, k, v, g128, 512, lam_init)
        else:
            attn = _attn_sample(lam, q, cache[0], cache[1], k, v, g128, lam_init)
        ssm, h_re, h_im = _ssm(u_t, h0_re, h0_im, lb_re, lb_im, bre, bim, cre, cim, d, w_glu_bf, b_glu2, tt)
        y = _post(x, attn, ssm, w_out_bf, ln[0], ln[1], w_up_bf, w_down_bf, ln[2], ln[3], nb, tl)
        shape_kv = (1, bsz, n, N_HEADS, HEAD_DIM)
        shape_h = (1, bsz, N_SSM_GROUPS, SSM_STATE)
        return y, k.reshape(shape_kv), v.reshape(shape_kv), h_re.reshape(shape_h), h_im.reshape(shape_h)

    zeros = jnp.zeros((bp, SSM_LANES), F32)
    yp, kp, vp, rp, ip = layer(x_prompt, np.arange(seq), None, zeros, zeros, nb=1, tl=512, tt=64)
    cache = tuple(jnp.transpose(c[l], (0, 2, 3, 1)).reshape(bs, ATTN_WIDTH, past) for c in (cache_k, cache_v))
    ys, ks, vs, rs, is_ = layer(x_sample, past + np.arange(dec_seq), cache,
                                state_ssm_re[l].reshape(bs, SSM_LANES), state_ssm_im[l].reshape(bs, SSM_LANES),
                                nb=SUBLANES, tl=dec_seq, tt=dec_seq)
    return (yp, ys, kp, vp, rp, ip, ks, vs, rs, is_)
```

```python
import functools
import math

import numpy as np
import jax
import jax.numpy as jnp
from jax import lax
from jax.experimental import pallas as pl
from jax.experimental.pallas import tpu as pltpu

D_MODEL = 1024
DEPTH = 1
CHUNK = 64
ATTN_WIDTH = 512
SSM_WIDTH = 512
N_HEADS = 8
HEAD_DIM = 64
HALF_DIM = 32
ROT_DIM = 8
ROPE_THETA = 500000.0
SSM_GROUP = 16
N_SSM_GROUPS = 32
SSM_STATE = 64
D_FF = 4 * D_MODEL
LN_EPS = 1e-5
SUBLN_EPS = 1e-5
DEEPNORM_ALPHA = (2 * DEPTH) ** 0.25
QK_SCALE = HALF_DIM ** -0.5
LOG2_E = math.log2(math.e)

SUBLANES = 8
LANES = 128
SSM_LANES = N_SSM_GROUPS * SSM_STATE
GROUPS_PER_SLAB = LANES // SSM_GROUP
N_SLABS = SSM_WIDTH // LANES
SLAB_STATES = GROUPS_PER_SLAB * SSM_STATE
HEADS_PER_SLAB = LANES // HEAD_DIM
MAPS_PER_SLAB = LANES // HALF_DIM
VMEM_LIMIT_BYTES = 56 * 1024 * 1024

F32 = jnp.float32
BF16 = jnp.bfloat16


def _lambda_init(layer_idx):
    return 0.8 - 0.6 * math.exp(-0.3 * layer_idx)


def _params(semantics):
    return pltpu.CompilerParams(dimension_semantics=semantics, vmem_limit_bytes=VMEM_LIMIT_BYTES)


def _prep_kernel(are_ref, aim_ref, logdt_ref, bre_ref, bim_ref, lq1_ref, lk1_ref, lq2_ref, lk2_ref,
                 lbre_ref, lbim_ref, bbre_ref, bbim_ref, lam_ref, *, lam_init):
    dt = jnp.exp(logdt_ref[...])
    ar = are_ref[...]
    ai = aim_ref[...]
    mag = jnp.exp(ar * dt)
    lb_re = mag * jnp.cos(ai * dt)
    lb_im = mag * jnp.sin(ai * dt)
    nr = lb_re - 1.0
    ni = lb_im
    den = ar * ar + ai * ai
    f_re = (nr * ar + ni * ai) / den
    f_im = (ni * ar - nr * ai) / den
    lbre_ref[...] = lb_re
    lbim_ref[...] = lb_im
    br = bre_ref[...]
    bi = bim_ref[...]
    bbre_ref[...] = f_re[:, None, :] * br - f_im[:, None, :] * bi
    bbim_ref[...] = f_re[:, None, :] * bi + f_im[:, None, :] * br
    s1 = jnp.sum(lq1_ref[...] * lk1_ref[...], axis=-1, keepdims=True)
    s2 = jnp.sum(lq2_ref[...] * lk2_ref[...], axis=-1, keepdims=True)
    lam_ref[...] = jnp.exp(s1) - jnp.exp(s2) + lam_init


def _prep(a_re, a_im, log_dt, b_re, b_im, lq1, lk1, lq2, lk2, lam_init):
    g, p, c = N_SSM_GROUPS, SSM_STATE, SSM_GROUP
    out_shape = (jax.ShapeDtypeStruct((g, p), F32), jax.ShapeDtypeStruct((g, p), F32),
                 jax.ShapeDtypeStruct((g, c, p), F32), jax.ShapeDtypeStruct((g, c, p), F32),
                 jax.ShapeDtypeStruct((1, 1), F32))
    return pl.pallas_call(
        functools.partial(_prep_kernel, lam_init=lam_init),
        out_shape=out_shape, name="prep",
    )(a_re, a_im, log_dt.reshape(g, 1), jnp.swapaxes(b_re, 1, 2), jnp.swapaxes(b_im, 1, 2),
      lq1.reshape(1, HALF_DIM), lk1.reshape(1, HALF_DIM), lq2.reshape(1, HALF_DIM), lk2.reshape(1, HALF_DIM))


def _block_diag_slabs(blocks):
    _, r, c = blocks.shape
    b4 = blocks.reshape(N_SLABS, GROUPS_PER_SLAB, r, c)
    eye = jnp.eye(GROUPS_PER_SLAB, dtype=blocks.dtype)
    out = jnp.einsum('sgrc,gh->sgrhc', b4, eye)
    return out.reshape(N_SLABS, GROUPS_PER_SLAB * r, GROUPS_PER_SLAB * c)


def _rope_tables(positions):
    inv = ROPE_THETA ** (-np.arange(0, ROT_DIM, 2, dtype=np.float64) / ROT_DIM)
    ang = np.asarray(positions, np.float64)[:, None] * inv[None, :]
    r = np.arange(LANES) % HALF_DIM
    half = ROT_DIM // 2
    idx = r % half
    cos = np.where(r[None, :] < ROT_DIM, np.cos(ang)[:, idx], 1.0)
    sin = np.sin(ang)[:, idx]
    s_up = np.where(r[None, :] < half, -sin, 0.0)
    s_dn = np.where((r[None, :] >= half) & (r[None, :] < ROT_DIM), sin, 0.0)
    return (jnp.asarray(cos, F32), jnp.asarray(s_up, F32), jnp.asarray(s_dn, F32))


def _inproj_kernel(x_ref, w_ref, cos_ref, sup_ref, sdn_ref, q_ref, k_ref, v_ref, u_ref, *, nb, tl):
    half = ROT_DIM // 2
    x = x_ref[...].reshape(nb * tl, D_MODEL).astype(BF16)
    cos = jnp.concatenate([cos_ref[...]] * nb, axis=0)
    sup = jnp.concatenate([sup_ref[...]] * nb, axis=0)
    sdn = jnp.concatenate([sdn_ref[...]] * nb, axis=0)

    def rope(t):
        slabs = []
        for j in range(ATTN_WIDTH // LANES):
            s = t[:, j * LANES:(j + 1) * LANES]
            slabs.append(s * cos + pltpu.roll(s, LANES - half, 1) * sup + pltpu.roll(s, half, 1) * sdn)
        return jnp.concatenate(slabs, axis=1)

    q = jnp.dot(x, w_ref[:, 0:ATTN_WIDTH], preferred_element_type=F32)
    q_ref[...] = (rope(q) * (QK_SCALE * LOG2_E)).astype(BF16).reshape(nb, tl, ATTN_WIDTH)
    k = jnp.dot(x, w_ref[:, ATTN_WIDTH:2 * ATTN_WIDTH], preferred_element_type=F32)
    k_ref[...] = rope(k).reshape(nb, tl, ATTN_WIDTH)
    v = jnp.dot(x, w_ref[:, 2 * ATTN_WIDTH:3 * ATTN_WIDTH], preferred_element_type=F32)
    v_ref[...] = v.reshape(nb, tl, ATTN_WIDTH)
    u = jnp.dot(x, w_ref[:, 3 * ATTN_WIDTH:], preferred_element_type=F32)
    first_slot = (pl.program_id(1) * nb) % SUBLANES
    for b in range(nb):
        for s in range(N_SLABS):
            u_ref[s, pl.ds(first_slot + b, tl, stride=SUBLANES), :] = u[b * tl:(b + 1) * tl,
                                                                        s * LANES:(s + 1) * LANES]


def _inproj(x, w_bf, tables, nb, tl):
    bsz, seq, _ = x.shape
    nblk = bsz // SUBLANES
    per_blk = SUBLANES // nb
    grid = (seq // tl, bsz // nb)
    tab_spec = pl.BlockSpec((tl, LANES), lambda i, b: (i, 0))
    act_spec = lambda width: pl.BlockSpec((nb, tl, width), lambda i, b: (b, i, 0))
    out_shape = (jax.ShapeDtypeStruct((bsz, seq, ATTN_WIDTH), BF16),
                 jax.ShapeDtypeStruct((bsz, seq, ATTN_WIDTH), F32),
                 jax.ShapeDtypeStruct((bsz, seq, ATTN_WIDTH), F32),
                 jax.ShapeDtypeStruct((nblk, N_SLABS, seq * SUBLANES, LANES), F32))
    return pl.pallas_call(
        functools.partial(_inproj_kernel, nb=nb, tl=tl),
        grid=grid,
        in_specs=[act_spec(D_MODEL),
                  pl.BlockSpec((D_MODEL, 4 * ATTN_WIDTH), lambda i, b: (0, 0)),
                  tab_spec, tab_spec, tab_spec],
        out_specs=(act_spec(ATTN_WIDTH), act_spec(ATTN_WIDTH), act_spec(ATTN_WIDTH),
                   pl.BlockSpec((None, N_SLABS, tl * SUBLANES, LANES), lambda i, b: (b // per_blk, 0, i, 0))),
        out_shape=out_shape,
        compiler_params=_params(("parallel", "arbitrary")),
        name="inproj",
    )(x, w_bf, *tables)


def _expand_maps(q):
    lane = lax.broadcasted_iota(jnp.int32, q.shape, 1)
    zero = jnp.zeros_like(q)
    return jnp.concatenate(
        [jnp.where((lane >= m * HALF_DIM) & (lane < (m + 1) * HALF_DIM), q, zero) for m in range(MAPS_PER_SLAB)],
        axis=0)


def _diff_combine(o, lam, g, n, lam_init):
    lane = lax.broadcasted_iota(jnp.int32, (n, LANES), 1)
    first = lane < HEAD_DIM
    d = jnp.where(first, o[0:n] - lam * o[n:2 * n], o[2 * n:3 * n] - lam * o[3 * n:4 * n])
    sq = d * d
    ss_a = jnp.sum(jnp.where(first, sq, 0.0), axis=-1, keepdims=True)
    ss_b = jnp.sum(jnp.where(first, 0.0, sq), axis=-1, keepdims=True)
    ms = jnp.where(first, ss_a, ss_b) * (1.0 / HEAD_DIM)
    return d * lax.rsqrt(ms + SUBLN_EPS) * g * (1.0 - lam_init)


def _attn_prompt_kernel(lam_ref, q_ref, k_ref, v_ref, g_ref, o_ref, kb, vb, qx_a, qx_b, s_a, s_b, m_ref, acc_ref,
                        *, blk, lam_init):
    seq = q_ref.shape[0]
    hb = blk // 2
    half = MAPS_PER_SLAB * hb
    rows = 2 * half
    dims = (((1,), (1,)), ((), ()))

    kb[...] = k_ref[...].astype(BF16)
    vb[:, 0:LANES] = v_ref[...].astype(BF16)
    vb[:, LANES:2 * LANES] = jnp.ones(v_ref.shape, BF16)

    steps = [(i, j) for i in range(seq // blk) for j in range(i + 1)]
    qx_bufs = (qx_a, qx_b)
    s_bufs = (s_a, s_b)

    def expand_queries(i):
        qx = qx_bufs[i % 2]
        qx[0:half, :] = _expand_maps(q_ref[i * blk:i * blk + hb, :])
        qx[half:rows, :] = _expand_maps(q_ref[i * blk + hb:(i + 1) * blk, :])

    def scores(n):
        i, j = steps[n]
        s_bufs[n % 2][...] = lax.dot_general(qx_bufs[i % 2][...], kb[j * blk:(j + 1) * blk, :], dims,
                                             preferred_element_type=F32)

    def update(j, src, h, width, row_offset):
        rs = slice(h * half, (h + 1) * half)
        s = src[rs, 0:width]
        if row_offset is not None:
            r = lax.broadcasted_iota(jnp.int32, (half, width), 0)
            c = lax.broadcasted_iota(jnp.int32, (half, width), 1)
            s = jnp.where((row_offset + r % hb) // CHUNK >= c // CHUNK, s, -jnp.inf)
        m_prev = m_ref[rs, :]
        m_new = jnp.maximum(m_prev, jnp.max(s, axis=-1, keepdims=True))
        alpha = jnp.exp2(m_prev - m_new)
        p = jnp.exp2(s - jnp.concatenate([m_new] * (width // LANES), axis=1)).astype(BF16)
        pv = jnp.dot(p, vb[j * blk:j * blk + width, :], preferred_element_type=F32)
        acc_ref[rs, :] = jnp.concatenate([alpha, alpha], axis=1) * acc_ref[rs, :] + pv
        m_ref[rs, :] = m_new

    expand_queries(0)
    scores(0)
    for n, (i, j) in enumerate(steps):
        if n + 1 < len(steps):
            if steps[n + 1][1] == 0:
                expand_queries(steps[n + 1][0])
            scores(n + 1)
        if j == 0:
            m_ref[...] = jnp.full((rows, LANES), -jnp.inf, F32)
            acc_ref[...] = jnp.zeros((rows, 2 * LANES), F32)
        src = s_bufs[n % 2]
        if j < i:
            update(j, src, 0, blk, None)
            update(j, src, 1, blk, None)
        else:
            update(j, src, 0, hb, 0)
            update(j, src, 1, blk, hb)
            for h in range(2):
                acc = acc_ref[h * half:(h + 1) * half, :]
                o = acc[:, 0:LANES] / acc[:, LANES:2 * LANES]
                o_ref[i * blk + h * hb:i * blk + (h + 1) * hb, :] = _diff_combine(
                    o, lam_ref[0, 0], g_ref[...], hb, lam_init).astype(BF16)


def _attn_prompt(lam, q, k, v, g128, blk, lam_init):
    bsz, seq, _ = q.shape
    rows = MAPS_PER_SLAB * blk
    slab_spec = pl.BlockSpec((None, seq, LANES), lambda b, j: (b, 0, j))
    return pl.pallas_call(
        functools.partial(_attn_prompt_kernel, blk=blk, lam_init=lam_init),
        grid=(bsz, ATTN_WIDTH // LANES),
        in_specs=[pl.BlockSpec(memory_space=pltpu.SMEM), slab_spec, slab_spec, slab_spec,
                  pl.BlockSpec((1, LANES), lambda b, j: (0, 0))],
        out_specs=slab_spec,
        out_shape=jax.ShapeDtypeStruct((bsz, seq, ATTN_WIDTH), BF16),
        scratch_shapes=[pltpu.VMEM((seq, LANES), BF16), pltpu.VMEM((seq, 2 * LANES), BF16),
                        pltpu.VMEM((rows, LANES), BF16), pltpu.VMEM((rows, LANES), BF16),
                        pltpu.VMEM((rows, blk), F32), pltpu.VMEM((rows, blk), F32),
                        pltpu.VMEM((rows, LANES), F32), pltpu.VMEM((rows, 2 * LANES), F32)],
        compiler_params=_params(("parallel", "parallel")),
        name="attn_prompt",
    )(lam, q, k, v, g128)


def _attn_sample_kernel(lam_ref, q_ref, kc_ref, vc_ref, kn_ref, vn_ref, g_ref, o_ref, *, n, lam_init):
    dims = (((1,), (1,)), ((), ()))
    for j in range(ATTN_WIDTH // LANES):
        cols = slice(j * LANES, (j + 1) * LANES)
        qx = _expand_maps(q_ref[:, cols])
        s_c = jnp.dot(qx, kc_ref[cols, :].astype(BF16), preferred_element_type=F32)
        s_n = lax.dot_general(qx, kn_ref[:, cols].astype(BF16), dims, preferred_element_type=F32)
        m = jnp.maximum(jnp.max(s_c, axis=-1, keepdims=True), jnp.max(s_n, axis=-1, keepdims=True))
        p_c = jnp.exp2(s_c - m)
        p_n = jnp.exp2(s_n - m)
        l = jnp.sum(p_c, axis=-1, keepdims=True) + jnp.sum(p_n, axis=-1, keepdims=True)
        acc = (lax.dot_general(p_c.astype(BF16), vc_ref[cols, :].astype(BF16), dims, preferred_element_type=F32)
               + jnp.dot(p_n.astype(BF16), vn_ref[:, cols].astype(BF16), preferred_element_type=F32))
        o_ref[:, cols] = _diff_combine(acc / l, lam_ref[0, 0], g_ref[...], n, lam_init).astype(BF16)


def _attn_sample(lam, q, cache_kt, cache_vt, k_new, v_new, g128, lam_init):
    bsz, n, _ = q.shape
    past = cache_kt.shape[2]
    assert past % CHUNK == 0 and n <= CHUNK
    cache_spec = pl.BlockSpec((None, ATTN_WIDTH, past), lambda b: (b, 0, 0))
    new_spec = pl.BlockSpec((None, n, ATTN_WIDTH), lambda b: (b, 0, 0))
    return pl.pallas_call(
        functools.partial(_attn_sample_kernel, n=n, lam_init=lam_init),
        grid=(bsz,),
        in_specs=[pl.BlockSpec(memory_space=pltpu.SMEM), new_spec, cache_spec, cache_spec, new_spec, new_spec,
                  pl.BlockSpec((1, LANES), lambda b: (0, 0))],
        out_specs=new_spec,
        out_shape=jax.ShapeDtypeStruct((bsz, n, ATTN_WIDTH), BF16),
        compiler_params=_params(("parallel",)),
        name="attn_sample",
    )(lam, q, cache_kt, cache_vt, k_new, v_new, g128)


def _ssm_kernel(u_ref, h0re_ref, h0im_ref, are_ref, aim_ref, bre_ref, bim_ref, cre_ref, cim_ref, d_ref,
                wglu_ref, bglu_ref, out_ref, hre_out, him_out, bure, buim, hsre, hsim, hre, him, res,
                *, tt, chunk_lanes):
    i = pl.program_id(1)

    @pl.when(i == 0)
    def _():
        hre[...] = h0re_ref[...]
        him[...] = h0im_ref[...]

    ys = []
    for s in range(N_SLABS):
        us = u_ref[s].astype(BF16)
        cols = slice(s * SLAB_STATES, (s + 1) * SLAB_STATES)
        bure[:, cols] = jnp.dot(us, bre_ref[s], preferred_element_type=F32)
        buim[:, cols] = jnp.dot(us, bim_ref[s], preferred_element_type=F32)
        ar = jnp.broadcast_to(are_ref[:, cols], (SUBLANES, SLAB_STATES))
        ai = jnp.broadcast_to(aim_ref[:, cols], (SUBLANES, SLAB_STATES))
        hr = hre[:, cols]
        hi = him[:, cols]
        for t in range(tt):
            r = slice(t * SUBLANES, (t + 1) * SUBLANES)
            hr, hi = ar * hr - ai * hi + bure[r, cols], ar * hi + ai * hr + buim[r, cols]
            hsre[r, cols] = hr
            hsim[r, cols] = hi
        hre[:, cols] = hr
        him[:, cols] = hi
        ys.append(jnp.dot(hsre[:, cols].astype(BF16), cre_ref[s], preferred_element_type=F32)
                  - jnp.dot(hsim[:, cols].astype(BF16), cim_ref[s], preferred_element_type=F32)
                  + d_ref[:, s * LANES:(s + 1) * LANES] * u_ref[s])
    y = jnp.concatenate(ys, axis=1)
    z = jnp.dot(jax.nn.gelu(y).astype(BF16), wglu_ref[...], preferred_element_type=F32) + bglu_ref[...]
    gated = z[:, :SSM_WIDTH] * jax.nn.sigmoid(z[:, SSM_WIDTH:])
    for s in range(N_SLABS):
        res[s] = gated[:, s * LANES:(s + 1) * LANES]
    for b in range(SUBLANES):
        for s in range(N_SLABS):
            out_ref[b, :, s * LANES:(s + 1) * LANES] = res[s, pl.ds(b, tt, stride=SUBLANES), :].astype(BF16)

    @pl.when(i == pl.num_programs(1) - 1)
    def _():
        hre_out[...] = hre[...]
        him_out[...] = him[...]


def _ssm(u_t, h0_re, h0_im, lb_re, lb_im, bre, bim, cre, cim, d, w_glu, b_glu, tt):
    nblk, _, rows_total, _ = u_t.shape
    seq = rows_total // SUBLANES
    rows = SUBLANES * tt
    grid = (nblk, seq // tt)
    const = lambda shape: pl.BlockSpec(shape, lambda b, i: (0,) * len(shape))
    state_spec = pl.BlockSpec((SUBLANES, SSM_LANES), lambda b, i: (b, 0))
    return pl.pallas_call(
        functools.partial(_ssm_kernel, tt=tt, chunk_lanes=1024),
        grid=grid,
        in_specs=[pl.BlockSpec((None, N_SLABS, rows, LANES), lambda b, i: (b, 0, i, 0)),
                  state_spec, state_spec,
                  const((1, SSM_LANES)), const((1, SSM_LANES)),
                  const((N_SLABS, LANES, SLAB_STATES)), const((N_SLABS, LANES, SLAB_STATES)),
                  const((N_SLABS, SLAB_STATES, LANES)), const((N_SLABS, SLAB_STATES, LANES)),
                  const((1, SSM_WIDTH)), const((SSM_WIDTH, 2 * SSM_WIDTH)), const((1, 2 * SSM_WIDTH))],
        out_specs=(pl.BlockSpec((SUBLANES, tt, SSM_WIDTH), lambda b, i: (b, i, 0)), state_spec, state_spec),
        out_shape=(jax.ShapeDtypeStruct((nblk * SUBLANES, seq, SSM_WIDTH), BF16),
                   jax.ShapeDtypeStruct((nblk * SUBLANES, SSM_LANES), F32),
                   jax.ShapeDtypeStruct((nblk * SUBLANES, SSM_LANES), F32)),
        scratch_shapes=[pltpu.VMEM((rows, SSM_LANES), F32) for _ in range(4)]
                       + [pltpu.VMEM((SUBLANES, SSM_LANES), F32) for _ in range(2)]
                       + [pltpu.VMEM((N_SLABS, rows, LANES), F32)],
        compiler_params=_params(("parallel", "arbitrary")),
        name="ssm",
    )(u_t, h0_re, h0_im, lb_re, lb_im, bre, bim, cre, cim, d, w_glu, b_glu)


def _layer_norm(x, g, b):
    mu = jnp.mean(x, axis=-1, keepdims=True)
    xc = x - mu
    var = jnp.mean(xc * xc, axis=-1, keepdims=True)
    return xc * lax.rsqrt(var + LN_EPS) * g + b


def _post_kernel(x_ref, a_ref, s_ref, wout_ref, g1_ref, b1_ref, wup_ref, wdown_ref, g2_ref, b2_ref, o_ref,
                 *, nb, tl, ff_chunk):
    n = nb * tl
    x = x_ref[...].reshape(n, D_MODEL)
    attn = a_ref[...].reshape(n, ATTN_WIDTH)
    ssm = s_ref[...].reshape(n, SSM_WIDTH)
    mix =jnp.dot(jnp.concatenate([attn, ssm], axis=1), wout_ref[...], preferred_element_type=F32)
    x1 = _layer_norm(DEEPNORM_ALPHA * x + mix, g1_ref[...], b1_ref[...])
    x1b = x1.astype(BF16)
    ff = jnp.zeros((n, D_MODEL), F32)
    for c in range(D_FF // ff_chunk):
        h = jnp.dot(x1b, wup_ref[:, c * ff_chunk:(c + 1) * ff_chunk], preferred_element_type=F32)
        h = jnp.square(jnp.maximum(h, 0.0)).astype(BF16)
        ff = ff + jnp.dot(h, wdown_ref[c * ff_chunk:(c + 1) * ff_chunk, :], preferred_element_type=F32)
    x2 = _layer_norm(DEEPNORM_ALPHA * x1 + ff, g2_ref[...], b2_ref[...])
    o_ref[...] = x2.reshape(nb, tl, D_MODEL)


def _post(x, attn, ssm, w_out, g1, b1, w_up, w_down, g2, b2, nb, tl):
    bsz, seq, _ = x.shape
    grid = (bsz // nb, seq // tl)
    act_spec = lambda width: pl.BlockSpec((nb, tl, width), lambda b, i: (b, i, 0))
    const = lambda shape: pl.BlockSpec(shape, lambda b, i: (0,) * len(shape), pipeline_mode=pl.Buffered(1))
    return pl.pallas_call(
        functools.partial(_post_kernel, nb=nb, tl=tl, ff_chunk=1024),
        grid=grid,
        in_specs=[act_spec(D_MODEL), act_spec(ATTN_WIDTH), act_spec(SSM_WIDTH),
                  const((D_MODEL, D_MODEL)), const((1, D_MODEL)), const((1, D_MODEL)),
                  const((D_MODEL, D_FF)), const((D_FF, D_MODEL)), const((1, D_MODEL)), const((1, D_MODEL))],
        out_specs=act_spec(D_MODEL),
        out_shape=jax.ShapeDtypeStruct((bsz, seq, D_MODEL), F32),
        compiler_params=_params(("parallel", "parallel")),
        name="post",
    )(x, attn, ssm, w_out, g1, b1, w_up, w_down, g2, b2)


def kernel(x_prompt, x_sample, cache_k, cache_v, state_ssm_re, state_ssm_im, w_in, lambda_q1, lambda_k1,
           lambda_q2, lambda_k2, subln_g, ssm_a_re, ssm_a_im, ssm_log_dt, ssm_b_re, ssm_b_im, ssm_c_re,
           ssm_c_im, ssm_d, w_glu, b_glu, w_out, ln1_g, ln1_b, w_up, w_down, ln2_g, ln2_b):
    assert w_in.shape[0] == DEPTH
    l = 0
    lam_init = _lambda_init(l)
    bp, seq, _ = x_prompt.shape
    bs, dec_seq, _ = x_sample.shape
    past = cache_k.shape[2]

    lb_re, lb_im, bbt_re, bbt_im, lam = _prep(ssm_a_re[l], ssm_a_im[l], ssm_log_dt[l], ssm_b_re[l],
                                             ssm_b_im[l], lambda_q1[l], lambda_k1[l], lambda_q2[l],
                                             lambda_k2[l], lam_init)
    lb_re = lb_re.reshape(1, SSM_LANES)
    lb_im = lb_im.reshape(1, SSM_LANES)
    bre = _block_diag_slabs(bbt_re).astype(BF16)
    bim = _block_diag_slabs(bbt_im).astype(BF16)
    cre = _block_diag_slabs(jnp.swapaxes(ssm_c_re[l], 1, 2)).astype(BF16)
    cim = _block_diag_slabs(jnp.swapaxes(ssm_c_im[l], 1, 2)).astype(BF16)
    d = ssm_d[l].reshape(1, SSM_WIDTH)
    g128 = jnp.concatenate([subln_g[l]] * HEADS_PER_SLAB).reshape(1, LANES)
    w_in_bf = w_in[l].astype(BF16)
    w_glu_bf = w_glu[l].astype(BF16)
    w_out_bf = w_out[l].astype(BF16)
    w_up_bf = w_up[l].astype(BF16)
    w_down_bf = w_down[l].astype(BF16)
    b_glu2 = b_glu[l].reshape(1, 2 * SSM_WIDTH)
    ln = [a[l].reshape(1, D_MODEL) for a in (ln1_g, ln1_b, ln2_g, ln2_b)]

    def layer(x, positions, cache, h0_re, h0_im, nb, tl, tt):
        bsz, n, _ = x.shape
        q, k, v, u_t = _inproj(x, w_in_bf, _rope_tables(positions), nb, tl)
        if cache is None:
            attn = _attn_prompt(lam, q, k, v, g128, 512, lam_init)
        else:
            attn = _attn_sample(lam, q, cache[0], cache[1], k, v, g128, lam_init)
        ssm, h_re, h_im = _ssm(u_t, h0_re, h0_im, lb_re, lb_im, bre, bim, cre, cim, d, w_glu_bf, b_glu2, tt)
        y = _post(x, attn, ssm, w_out_bf, ln[0], ln[1], w_up_bf, w_down_bf, ln[2], ln[3], nb, tl)
        shape_kv = (1, bsz, n, N_HEADS, HEAD_DIM)
        shape_h = (1, bsz, N_SSM_GROUPS, SSM_STATE)
        return y, k.reshape(shape_kv), v.reshape(shape_kv), h_re.reshape(shape_h), h_im.reshape(shape_h)

    zeros = jnp.zeros((bp, SSM_LANES), F32)
    yp, kp, vp, rp, ip = layer(x_prompt, np.arange(seq), None, zeros, zeros, nb=1, tl=512, tt=64)
    cache = tuple(jnp.transpose(c[l], (0, 2, 3, 1)).reshape(bs, ATTN_WIDTH, past) for c in (cache_k, cache_v))
    ys, ks, vs, rs, is_ = layer(x_sample, past + np.arange(dec_seq), cache,
                                state_ssm_re[l].reshape(bs, SSM_LANES), state_ssm_im[l].reshape(bs, SSM_LANES),
                                nb=SUBLANES, tl=dec_seq, tt=dec_seq)
    return (yp, ys, kp, vp, rp, ip, ks, vs, rs, is_)
```

```python
import functools
import math

import numpy as np
import jax
import jax.numpy as jnp
from jax import lax
from jax.experimental import pallas as pl
from jax.experimental.pallas import tpu as pltpu

D_MODEL = 1024
DEPTH = 1
CHUNK = 64
ATTN_WIDTH = 512
SSM_WIDTH = 512
N_HEADS = 8
HEAD_DIM = 64
HALF_DIM = 32
ROT_DIM = 8
ROPE_THETA = 500000.0
SSM_GROUP = 16
N_SSM_GROUPS = 32
SSM_STATE = 64
D_FF = 4 * D_MODEL
LN_EPS = 1e-5
SUBLN_EPS = 1e-5
DEEPNORM_ALPHA = (2 * DEPTH) ** 0.25
QK_SCALE = HALF_DIM ** -0.5
LOG2_E = math.log2(math.e)

SUBLANES = 8
LANES = 128
SSM_LANES = N_SSM_GROUPS * SSM_STATE
GROUPS_PER_SLAB = LANES // SSM_GROUP
N_SLABS = SSM_WIDTH // LANES
SLAB_STATES = GROUPS_PER_SLAB * SSM_STATE
HEADS_PER_SLAB = LANES // HEAD_DIM
MAPS_PER_SLAB = LANES // HALF_DIM
VMEM_LIMIT_BYTES = 56 * 1024 * 1024

F32 = jnp.float32
BF16 = jnp.bfloat16


def _lambda_init(layer_idx):
    return 0.8 - 0.6 * math.exp(-0.3 * layer_idx)


def _params(semantics):
    return pltpu.CompilerParams(dimension_semantics=semantics, vmem_limit_bytes=VMEM_LIMIT_BYTES)


def _prep_kernel(are_ref, aim_ref, logdt_ref, bre_ref, bim_ref, lq1_ref, lk1_ref, lq2_ref, lk2_ref,
                 lbre_ref, lbim_ref, bbre_ref, bbim_ref, lam_ref, *, lam_init):
    dt = jnp.exp(logdt_ref[...])
    ar = are_ref[...]
    ai = aim_ref[...]
    mag = jnp.exp(ar * dt)
    lb_re = mag * jnp.cos(ai * dt)
    lb_im = mag * jnp.sin(ai * dt)
    nr = lb_re - 1.0
    ni = lb_im
    den = ar * ar + ai * ai
    f_re = (nr * ar + ni * ai) / den
    f_im = (ni * ar - nr * ai) / den
    lbre_ref[...] = lb_re
    lbim_ref[...] = lb_im
    br = bre_ref[...]
    bi = bim_ref[...]
    bbre_ref[...] = f_re[:, None, :] * br - f_im[:, None, :] * bi
    bbim_ref[...] = f_re[:, None, :] * bi + f_im[:, None, :] * br
    s1 = jnp.sum(lq1_ref[...] * lk1_ref[...], axis=-1, keepdims=True)
    s2 = jnp.sum(lq2_ref[...] * lk2_ref[...], axis=-1, keepdims=True)
    lam_ref[...] = jnp.exp(s1) - jnp.exp(s2) + lam_init


def _prep(a_re, a_im, log_dt, b_re, b_im, lq1, lk1, lq2, lk2, lam_init):
    g, p, c = N_SSM_GROUPS, SSM_STATE, SSM_GROUP
    out_shape = (jax.ShapeDtypeStruct((g, p), F32), jax.ShapeDtypeStruct((g, p), F32),
                 jax.ShapeDtypeStruct((g, c, p), F32), jax.ShapeDtypeStruct((g, c, p), F32),
                 jax.ShapeDtypeStruct((1, 1), F32))
    return pl.pallas_call(
        functools.partial(_prep_kernel, lam_init=lam_init),
        out_shape=out_shape, name="prep",
    )(a_re, a_im, log_dt.reshape(g, 1), jnp.swapaxes(b_re, 1, 2), jnp.swapaxes(b_im, 1, 2),
      lq1.reshape(1, HALF_DIM), lk1.reshape(1, HALF_DIM), lq2.reshape(1, HALF_DIM), lk2.reshape(1, HALF_DIM))


def _block_diag_slabs(blocks):
    _, r, c = blocks.shape
    b4 = blocks.reshape(N_SLABS, GROUPS_PER_SLAB, r, c)
    eye = jnp.eye(GROUPS_PER_SLAB, dtype=blocks.dtype)
    out = jnp.einsum('sgrc,gh->sgrhc', b4, eye)
    return out.reshape(N_SLABS, GROUPS_PER_SLAB * r, GROUPS_PER_SLAB * c)


def _rope_tables(positions):
    inv = ROPE_THETA ** (-np.arange(0, ROT_DIM, 2, dtype=np.float64) / ROT_DIM)
    ang = np.asarray(positions, np.float64)[:, None] * inv[None, :]
    r = np.arange(LANES) % HALF_DIM
    half = ROT_DIM // 2
    idx = r % half
    cos = np.where(r[None, :] < ROT_DIM, np.cos(ang)[:, idx], 1.0)
    sin = np.sin(ang)[:, idx]
    s_up = np.where(r[None, :] < half, -sin, 0.0)
    s_dn = np.where((r[None, :] >= half) & (r[None, :] < ROT_DIM), sin, 0.0)
    return (jnp.asarray(cos, F32), jnp.asarray(s_up, F32), jnp.asarray(s_dn, F32))


def _inproj_kernel(x_ref, w_ref, cos_ref, sup_ref, sdn_ref, q_ref, k_ref, v_ref, u_ref, *, nb, tl):
    half = ROT_DIM // 2
    x = x_ref[...].reshape(nb * tl, D_MODEL).astype(BF16)
    cos = jnp.concatenate([cos_ref[...]] * nb, axis=0)
    sup = jnp.concatenate([sup_ref[...]] * nb, axis=0)
    sdn = jnp.concatenate([sdn_ref[...]] * nb, axis=0)

    def rope(t):
        slabs = []
        for j in range(ATTN_WIDTH // LANES):
            s = t[:, j * LANES:(j + 1) * LANES]
            slabs.append(s * cos + pltpu.roll(s, LANES - half, 1) * sup + pltpu.roll(s, half, 1) * sdn)
        return jnp.concatenate(slabs, axis=1)

    q = jnp.dot(x, w_ref[:, 0:ATTN_WIDTH], preferred_element_type=F32)
    q_ref[...] = (rope(q) * (QK_SCALE * LOG2_E)).astype(BF16).reshape(nb, tl, ATTN_WIDTH)
    k = jnp.dot(x, w_ref[:, ATTN_WIDTH:2 * ATTN_WIDTH], preferred_element_type=F32)
    k_ref[...] = rope(k).reshape(nb, tl, ATTN_WIDTH)
    v = jnp.dot(x, w_ref[:, 2 * ATTN_WIDTH:3 * ATTN_WIDTH], preferred_element_type=F32)
    v_ref[...] = v.reshape(nb, tl, ATTN_WIDTH)
    u = jnp.dot(x, w_ref[:, 3 * ATTN_WIDTH:], preferred_element_type=F32)
    first_slot = (pl.program_id(1) * nb) % SUBLANES
    for b in range(nb):
        for s in range(N_SLABS):
            u_ref[s, pl.ds(first_slot + b, tl, stride=SUBLANES), :] = u[b * tl:(b + 1) * tl,
                                                                        s * LANES:(s + 1) * LANES]


def _inproj(x, w_bf, tables, nb, tl):
    bsz, seq, _ = x.shape
    nblk = bsz // SUBLANES
    per_blk = SUBLANES // nb
    grid = (seq // tl, bsz // nb)
    tab_spec = pl.BlockSpec((tl, LANES), lambda i, b: (i, 0))
    act_spec = lambda width: pl.BlockSpec((nb, tl, width), lambda i, b: (b, i, 0))
    out_shape = (jax.ShapeDtypeStruct((bsz, seq, ATTN_WIDTH), BF16),
                 jax.ShapeDtypeStruct((bsz, seq, ATTN_WIDTH), F32),
                 jax.ShapeDtypeStruct((bsz, seq, ATTN_WIDTH), F32),
                 jax.ShapeDtypeStruct((nblk, N_SLABS, seq * SUBLANES, LANES), F32))
    return pl.pallas_call(
        functools.partial(_inproj_kernel, nb=nb, tl=tl),
        grid=grid,
        in_specs=[act_spec(D_MODEL),
                  pl.BlockSpec((D_MODEL, 4 * ATTN_WIDTH), lambda i, b: (0, 0)),
                  tab_spec, tab_spec, tab_spec],
        out_specs=(act_spec(ATTN_WIDTH), act_spec(ATTN_WIDTH), act_spec(ATTN_WIDTH),
                   pl.BlockSpec((None, N_SLABS, tl * SUBLANES, LANES), lambda i, b: (b // per_blk, 0, i, 0))),
        out_shape=out_shape,
        compiler_params=_params(("parallel", "arbitrary")),
        name="inproj",
    )(x, w_bf, *tables)


def _expand_maps(q):
    lane = lax.broadcasted_iota(jnp.int32, q.shape, 1)
    zero = jnp.zeros_like(q)
    return jnp.concatenate(
        [jnp.where((lane >= m * HALF_DIM) & (lane < (m + 1) * HALF_DIM), q, zero) for m in range(MAPS_PER_SLAB)],
        axis=0)


def _diff_combine(o, lam, g, n, lam_init):
    lane = lax.broadcasted_iota(jnp.int32, (n, LANES), 1)
    first = lane < HEAD_DIM
    d = jnp.where(first, o[0:n] - lam * o[n:2 * n], o[2 * n:3 * n] - lam * o[3 * n:4 * n])
    sq = d * d
    ss_a = jnp.sum(jnp.where(first, sq, 0.0), axis=-1, keepdims=True)
    ss_b = jnp.sum(jnp.where(first, 0.0, sq), axis=-1, keepdims=True)
    ms = jnp.where(first, ss_a, ss_b) * (1.0 / HEAD_DIM)
    return d * lax.rsqrt(ms + SUBLN_EPS) * g * (1.0 - lam_init)


def _attn_prompt_kernel(lam_ref, q_ref, k_ref, v_ref, g_ref, o_ref, kb, vb, qx_a, qx_b, s_a, s_b, m_ref, acc_ref,
                        *, blk, lam_init):
    seq = q_ref.shape[0]
    hb = blk // 2
    half = MAPS_PER_SLAB * hb
    rows = 2 * half
    dims = (((1,), (1,)), ((), ()))

    kb[...] = k_ref[...].astype(BF16)
    vb[:, 0:LANES] = v_ref[...].astype(BF16)
    vb[:, LANES:2 * LANES] = jnp.ones(v_ref.shape, BF16)

    steps = [(i, j) for i in range(seq // blk) for j in range(i + 1)]
    qx_bufs = (qx_a, qx_b)
    s_bufs = (s_a, s_b)

    def expand_queries(i):
        qx = qx_bufs[i % 2]
        qx[0:half, :] = _expand_maps(q_ref[i * blk:i * blk + hb, :])
        qx[half:rows, :] = _expand_maps(q_ref[i * blk + hb:(i + 1) * blk, :])

    def scores(n):
        i, j = steps[n]
        s_bufs[n % 2][...] = lax.dot_general(qx_bufs[i % 2][...], kb[j * blk:(j + 1) * blk, :], dims,
                                             preferred_element_type=F32)

    def update(j, src, h, width, row_offset):
        rs = slice(h * half, (h + 1) * half)
        s = src[rs, 0:width]
        if row_offset is not None:
            r = lax.broadcasted_iota(jnp.int32, (half, width), 0)
            c = lax.broadcasted_iota(jnp.int32, (half, width), 1)
            s = jnp.where((row_offset + r % hb) // CHUNK >= c // CHUNK, s, -jnp.inf)
        m_prev = m_ref[rs, :]
        m_new = jnp.maximum(m_prev, jnp.max(s, axis=-1, keepdims=True))
        alpha = jnp.exp2(m_prev - m_new)
        p = jnp.exp2(s - jnp.concatenate([m_new] * (width // LANES), axis=1)).astype(BF16)
        pv = jnp.dot(p, vb[j * blk:j * blk + width, :], preferred_element_type=F32)
        acc_ref[rs, :] = jnp.concatenate([alpha, alpha], axis=1) * acc_ref[rs, :] + pv
        m_ref[rs, :] = m_new

    expand_queries(0)
    scores(0)
    for n, (i, j) in enumerate(steps):
        if n + 1 < len(steps):
            if steps[n + 1][1] == 0:
                expand_queries(steps[n + 1][0])
            scores(n + 1)
        if j == 0:
            m_ref[...] = jnp.full((rows, LANES), -jnp.inf, F32)
            acc_ref[...] = jnp.zeros((rows, 2 * LANES), F32)
        src = s_bufs[n % 2]
        if j < i:
            update(j, src, 0, blk, None)
            update(j, src, 1, blk, None)
        else:
            update(j, src, 0, hb, 0)
            update(j, src, 1, blk, hb)
            for h in range(2):
                acc = acc_ref[h * half:(h + 1) * half, :]
                o = acc[:, 0:LANES] / acc[:, LANES:2 * LANES]
                o_ref[i * blk + h * hb:i * blk + (h + 1) * hb, :] = _diff_combine(
                    o, lam_ref[0, 0], g_ref[...], hb, lam_init).astype(BF16)


def _attn_prompt(lam, q, k, v, g128, blk, lam_init):
    bsz, seq, _ = q.shape
    rows = MAPS_PER_SLAB * blk
    slab_spec = pl.BlockSpec((None, seq, LANES), lambda b, j: (b, 0, j))
    return pl.pallas_call(
        functools.partial(_attn_prompt_kernel, blk=blk, lam_init=lam_init),
        grid=(bsz, ATTN_WIDTH // LANES),
        in_specs=[pl.BlockSpec(memory_space=pltpu.SMEM), slab_spec, slab_spec, slab_spec,
                  pl.BlockSpec((1, LANES), lambda b, j: (0, 0))],
        out_specs=slab_spec,
        out_shape=jax.ShapeDtypeStruct((bsz, seq, ATTN_WIDTH), BF16),
        scratch_shapes=[pltpu.VMEM((seq, LANES), BF16), pltpu.VMEM((seq, 2 * LANES), BF16),
                        pltpu.VMEM((rows, LANES), BF16), pltpu.VMEM((rows, LANES), BF16),
                        pltpu.VMEM((rows, blk), F32), pltpu.VMEM((rows, blk), F32),
                        pltpu.VMEM((rows, LANES), F32), pltpu.VMEM((rows, 2 * LANES), F32)],
        compiler_params=_params(("parallel", "parallel")),
        name="attn_prompt",
    )(lam, q, k, v, g128)


def _attn_sample_kernel(lam_ref, q_ref, kc_ref, vc_ref, kn_ref, vn_ref, g_ref, o_ref, *, n, lam_init):
    dims = (((1,), (1,)), ((), ()))
    for j in range(ATTN_WIDTH // LANES):
        cols = slice(j * LANES, (j + 1) * LANES)
        qx = _expand_maps(q_ref[:, cols])
        s_c = jnp.dot(qx, kc_ref[cols, :].astype(BF16), preferred_element_type=F32)
        s_n = lax.dot_general(qx, kn_ref[:, cols].astype(BF16), dims, preferred_element_type=F32)
        m = jnp.maximum(jnp.max(s_c, axis=-1, keepdims=True), jnp.max(s_n, axis=-1, keepdims=True))
        p_c = jnp.exp2(s_c - m)
        p_n = jnp.exp2(s_n - m)
        l = jnp.sum(p_c, axis=-1, keepdims=True) + jnp.sum(p_n, axis=-1, keepdims=True)
        acc = (lax.dot_general(p_c.astype(BF16), vc_ref[cols, :].astype(BF16), dims, preferred_element_type=F32)
               + jnp.dot(p_n.astype(BF16), vn_ref[:, cols].astype(BF16), preferred_element_type=F32))
        o_ref[:, cols] = _diff_combine(acc / l, lam_ref[0, 0], g_ref[...], n, lam_init).astype(BF16)


def _attn_sample(lam, q, cache_kt, cache_vt, k_new, v_new, g128, lam_init):
    bsz, n, _ = q.shape
    past = cache_kt.shape[2]
    assert past % CHUNK == 0 and n <= CHUNK
    cache_spec = pl.BlockSpec((None, ATTN_WIDTH, past), lambda b: (b, 0, 0))
    new_spec = pl.BlockSpec((None, n, ATTN_WIDTH), lambda b: (b, 0, 0))
    return pl.pallas_call(
        functools.partial(_attn_sample_kernel, n=n, lam_init=lam_init),
        grid=(bsz,),
        in_specs=[pl.BlockSpec(memory_space=pltpu.SMEM), new_spec, cache_spec, cache_spec, new_spec, new_spec,
                  pl.BlockSpec((1, LANES), lambda b: (0, 0))],
        out_specs=new_spec,
        out_shape=jax.ShapeDtypeStruct((bsz, n, ATTN_WIDTH), BF16),
        compiler_params=_params(("parallel",)),
        name="attn_sample",
    )(lam, q, cache_kt, cache_vt, k_new, v_new, g128)


def _ssm_kernel(u_ref, h0re_ref, h0im_ref, are_ref, aim_ref, bre_ref, bim_ref, cre_ref, cim_ref, d_ref,
                wglu_ref, bglu_ref, out_ref, hre_out, him_out, bure, buim, hsre, hsim, hre, him, res,
                *, tt, chunk_lanes):
    i = pl.program_id(1)

    @pl.when(i == 0)
    def _():
        hre[...] = h0re_ref[...]
        him[...] = h0im_ref[...]

    ys = []
    for s in range(N_SLABS):
        us = u_ref[s].astype(BF16)
        cols = slice(s * SLAB_STATES, (s + 1) * SLAB_STATES)
        bure[:, cols] = jnp.dot(us, bre_ref[s], preferred_element_type=F32)
        buim[:, cols] = jnp.dot(us, bim_ref[s], preferred_element_type=F32)
        ar = jnp.broadcast_to(are_ref[:, cols], (SUBLANES, SLAB_STATES))
        ai = jnp.broadcast_to(aim_ref[:, cols], (SUBLANES, SLAB_STATES))
        hr = hre[:, cols]
        hi = him[:, cols]
        for t in range(tt):
            r = slice(t * SUBLANES, (t + 1) * SUBLANES)
            hr, hi = ar * hr - ai * hi + bure[r, cols], ar * hi + ai * hr + buim[r, cols]
            hsre[r, cols] = hr
            hsim[r, cols] = hi
        hre[:, cols] = hr
        him[:, cols] = hi
        ys.append(jnp.dot(hsre[:, cols].astype(BF16), cre_ref[s], preferred_element_type=F32)
                  - jnp.dot(hsim[:, cols].astype(BF16), cim_ref[s], preferred_element_type=F32)
                  + d_ref[:, s * LANES:(s + 1) * LANES] * u_ref[s])
    y = jnp.concatenate(ys, axis=1)
    z = jnp.dot(jax.nn.gelu(y).astype(BF16), wglu_ref[...], preferred_element_type=F32) + bglu_ref[...]
    gated = z[:, :SSM_WIDTH] * jax.nn.sigmoid(z[:, SSM_WIDTH:])
    for s in range(N_SLABS):
        res[s] = gated[:, s * LANES:(s + 1) * LANES]
    for b in range(SUBLANES):
        for s in range(N_SLABS):
            out_ref[b, :, s * LANES:(s + 1) * LANES] = res[s, pl.ds(b, tt, stride=SUBLANES), :].astype(BF16)

    @pl.when(i == pl.num_programs(1) - 1)
    def _():
        hre_out[...] = hre[...]
        him_out[...] = him[...]


def _ssm(u_t, h0_re, h0_im, lb_re, lb_im, bre, bim, cre, cim, d, w_glu, b_glu, tt):
    nblk, _, rows_total, _ = u_t.shape
    seq = rows_total // SUBLANES
    rows = SUBLANES * tt
    grid = (nblk, seq // tt)
    const = lambda shape: pl.BlockSpec(shape, lambda b, i: (0,) * len(shape))
    state_spec = pl.BlockSpec((SUBLANES, SSM_LANES), lambda b, i: (b, 0))
    return pl.pallas_call(
        functools.partial(_ssm_kernel, tt=tt, chunk_lanes=1024),
        grid=grid,
        in_specs=[pl.BlockSpec((None, N_SLABS, rows, LANES), lambda b, i: (b, 0, i, 0)),
                  state_spec, state_spec,
                  const((1, SSM_LANES)), const((1, SSM_LANES)),
                  const((N_SLABS, LANES, SLAB_STATES)), const((N_SLABS, LANES, SLAB_STATES)),
                  const((N_SLABS, SLAB_STATES, LANES)), const((N_SLABS, SLAB_STATES, LANES)),
                  const((1, SSM_WIDTH)), const((SSM_WIDTH, 2 * SSM_WIDTH)), const((1, 2 * SSM_WIDTH))],
        out_specs=(pl.BlockSpec((SUBLANES, tt, SSM_WIDTH), lambda b, i: (b, i, 0)), state_spec, state_spec),
        out_shape=(jax.ShapeDtypeStruct((nblk * SUBLANES, seq, SSM_WIDTH), BF16),
                   jax.ShapeDtypeStruct((nblk * SUBLANES, SSM_LANES), F32),
                   jax.ShapeDtypeStruct((nblk * SUBLANES, SSM_LANES), F32)),
        scratch_shapes=[pltpu.VMEM((rows, SSM_LANES), F32) for _ in range(4)]
                       + [pltpu.VMEM((SUBLANES, SSM_LANES), F32) for _ in range(2)]
                       + [pltpu.VMEM((N_SLABS, rows, LANES), F32)],
        compiler_params=_params(("parallel", "arbitrary")),
        name="ssm",
    )(u_t, h0_re, h0_im, lb_re, lb_im, bre, bim, cre, cim, d, w_glu, b_glu)


def _layer_norm(x, g, b):
    mu = jnp.mean(x, axis=-1, keepdims=True)
    xc = x - mu
    var = jnp.mean(xc * xc, axis=-1, keepdims=True)
    return xc * lax.rsqrt(var + LN_EPS) * g + b


def _post_kernel(x_ref, a_ref, s_ref, wout_ref, g1_ref, b1_ref, wup_ref, wdown_ref, g2_ref, b2_ref, o_ref,
                 *, nb, tl, ff_chunk):
    chains = (0, 1)
    n = nb * tl // len(chains)
    if nb == 1:
        views = [(slice(None), slice(c * n, (c + 1) * n)) for c in chains]
    else:
        views = [(slice(c * nb // len(chains), (c + 1) * nb // len(chains)), slice(None)) for c in chains]
    xs = [x_ref[v].reshape(n, D_MODEL) for v in views]
    mix = [jnp.dot(jnp.concatenate([a_ref[v].reshape(n, ATTN_WIDTH), s_ref[v].reshape(n, SSM_WIDTH)], axis=1),
                   wout_ref[...], preferred_element_type=F32) for v in views]
    x1 = [None, None]
    x1b = [None, None]
    h = [None, None]
    ff = [jnp.zeros((n, D_MODEL), F32) for _ in chains]
    n_chunks = D_FF // ff_chunk

    def up(c, k):
        return jnp.dot(x1b[c], wup_ref[:, k * ff_chunk:(k + 1) * ff_chunk], preferred_element_type=F32)

    def down(c, k):
        act = jnp.square(jnp.maximum(h[c], 0.0)).astype(BF16)
        return ff[c] + jnp.dot(act, wdown_ref[k * ff_chunk:(k + 1) * ff_chunk, :], preferred_element_type=F32)

    x1[0] = _layer_norm(DEEPNORM_ALPHA * xs[0] + mix[0], g1_ref[...], b1_ref[...])
    x1b[0] = x1[0].astype(BF16)
    h[0] = up(0, 0)
    x1[1] = _layer_norm(DEEPNORM_ALPHA * xs[1] + mix[1], g1_ref[...], b1_ref[...])
    x1b[1] = x1[1].astype(BF16)
    for k in range(n_chunks):
        h[1] = up(1, k)
        ff[0] = down(0, k)
        if k + 1 < n_chunks:
            h[0] = up(0, k + 1)
        ff[1] = down(1, k)
    for c in chains:
        x2 = _layer_norm(DEEPNORM_ALPHA * x1[c] + ff[c], g2_ref[...], b2_ref[...])
        o_ref[views[c]] = x2.reshape(x_ref[views[c]].shape)


def _post(x, attn, ssm, w_out, g1, b1, w_up, w_down, g2, b2, nb, tl):
    bsz, seq, _ = x.shape
    grid = (bsz // nb, seq // tl)
    act_spec = lambda width: pl.BlockSpec((nb, tl, width), lambda b, i: (b, i, 0))
    const = lambda shape: pl.BlockSpec(shape, lambda b, i: (0,) * len(shape), pipeline_mode=pl.Buffered(1))
    return pl.pallas_call(
        functools.partial(_post_kernel, nb=nb, tl=tl, ff_chunk=1024),
        grid=grid,
        in_specs=[act_spec(D_MODEL), act_spec(ATTN_WIDTH), act_spec(SSM_WIDTH),
                  const((D_MODEL, D_MODEL)), const((1, D_MODEL)), const((1, D_MODEL)),
                  const((D_MODEL, D_FF)), const((D_FF, D_MODEL)), const((1, D_MODEL)), const((1, D_MODEL))],
        out_specs=act_spec(D_MODEL),
        out_shape=jax.ShapeDtypeStruct((bsz, seq, D_MODEL), F32),
        compiler_params=_params(("parallel", "parallel")),
        name="post",
    )(x, attn, ssm, w_out, g1, b1, w_up, w_down, g2, b2)


def kernel(x_prompt, x_sample, cache_k, cache_v, state_ssm_re, state_ssm_im, w_in, lambda_q1, lambda_k1,
           lambda_q2, lambda_k2, subln_g, ssm_a_re, ssm_a_im, ssm_log_dt, ssm_b_re, ssm_b_im, ssm_c_re,
           ssm_c_im, ssm_d, w_glu, b_glu, w_out, ln1_g, ln1_b, w_up, w_down, ln2_g, ln2_b):
    assert w_in.shape[0] == DEPTH
    l = 0
    lam_init = _lambda_init(l)
    bp, seq, _ = x_prompt.shape
    bs, dec_seq, _ = x_sample.shape
    past = cache_k.shape[2]

    lb_re, lb_im, bbt_re, bbt_im, lam = _prep(ssm_a_re[l], ssm_a_im[l], ssm_log_dt[l], ssm_b_re[l],
                                             ssm_b_im[l], lambda_q1[l], lambda_k1[l], lambda_q2[l],
                                             lambda_k2[l], lam_init)
    lb_re = lb_re.reshape(1, SSM_LANES)
    lb_im = lb_im.reshape(1, SSM_LANES)
    bre = _block_diag_slabs(bbt_re).astype(BF16)
    bim = _block_diag_slabs(bbt_im).astype(BF16)
    cre = _block_diag_slabs(jnp.swapaxes(ssm_c_re[l], 1, 2)).astype(BF16)
    cim = _block_diag_slabs(jnp.swapaxes(ssm_c_im[l], 1, 2)).astype(BF16)
    d = ssm_d[l].reshape(1, SSM_WIDTH)
    g128 = jnp.concatenate([subln_g[l]] * HEADS_PER_SLAB).reshape(1, LANES)
    w_in_bf = w_in[l].astype(BF16)
    w_glu_bf = w_glu[l].astype(BF16)
    w_out_bf = w_out[l].astype(BF16)
    w_up_bf = w_up[l].astype(BF16)
    w_down_bf = w_down[l].astype(BF16)
    b_glu2 = b_glu[l].reshape(1, 2 * SSM_WIDTH)
    ln = [a[l].reshape(1, D_MODEL) for a in (ln1_g, ln1_b, ln2_g, ln2_b)]

    def layer(x, positions, cache, h0_re, h0_im, nb, tl, tt):
        bsz, n, _ = x.shape
        q, k, v, u_t = _inproj(x, w_in_bf, _rope_tables(positions), nb, tl)
        if cache is None:
            attn = _attn_prompt(lam, q, k, v, g128, 512, lam_init)
        else:
            attn = _attn_sample(lam, q, cache[0], cache[1], k, v, g128, lam_init)
        ssm, h_re, h_im = _ssm(u_t, h0_re, h0_im, lb_re, lb_im, bre, bim, cre, cim, d, w_glu_bf, b_glu2, tt)
        y = _post(x, attn, ssm, w_out_bf, ln[0], ln[1], w_up_bf, w_down_bf, ln[2], ln[3], nb, tl)
        shape_kv = (1, bsz, n, N_HEADS, HEAD_DIM)
        shape_h = (1, bsz, N_SSM_GROUPS, SSM_STATE)
        return y, k.reshape(shape_kv), v.reshape(shape_kv), h_re.reshape(shape_h), h_im.reshape(shape_h)

    zeros = jnp.zeros((bp, SSM_LANES), F32)
    yp, kp, vp, rp, ip = layer(x_prompt, np.arange(seq), None, zeros, zeros, nb=1, tl=512, tt=64)
    cache = tuple(jnp.transpose(c[l], (0, 2, 3, 1)).reshape(bs, ATTN_WIDTH, past) for c in (cache_k, cache_v))
    ys, ks, vs, rs, is_ = layer(x_sample, past + np.arange(dec_seq), cache,
                                state_ssm_re[l].reshape(bs, SSM_LANES), state_ssm_im[l].reshape(bs, SSM_LANES),
                                nb=SUBLANES, tl=dec_seq, tt=dec_seq)
    return (yp, ys, kp, vp, rp, ip, ks, vs, rs, is_)
```

```python
import functools
import math

import numpy as np
import jax
import jax.numpy as jnp
from jax import lax
from jax.experimental import pallas as pl
from jax.experimental.pallas import tpu as pltpu

D_MODEL = 1024
DEPTH = 1
CHUNK = 64
ATTN_WIDTH = 512
SSM_WIDTH = 512
N_HEADS = 8
HEAD_DIM = 64
HALF_DIM = 32
ROT_DIM = 8
ROPE_THETA = 500000.0
SSM_GROUP = 16
N_SSM_GROUPS = 32
SSM_STATE = 64
D_FF = 4 * D_MODEL
LN_EPS = 1e-5
SUBLN_EPS = 1e-5
DEEPNORM_ALPHA = (2 * DEPTH) ** 0.25
QK_SCALE = HALF_DIM ** -0.5
LOG2_E = math.log2(math.e)

SUBLANES = 8
LANES = 128
SSM_LANES = N_SSM_GROUPS * SSM_STATE
GROUPS_PER_SLAB = LANES // SSM_GROUP
N_SLABS = SSM_WIDTH // LANES
SLAB_STATES = GROUPS_PER_SLAB * SSM_STATE
HEADS_PER_SLAB = LANES // HEAD_DIM
MAPS_PER_SLAB = LANES // HALF_DIM
VMEM_LIMIT_BYTES = 56 * 1024 * 1024

F32 = jnp.float32
BF16 = jnp.bfloat16


def _lambda_init(layer_idx):
    return 0.8 - 0.6 * math.exp(-0.3 * layer_idx)


def _params(semantics):
    return pltpu.CompilerParams(dimension_semantics=semantics, vmem_limit_bytes=VMEM_LIMIT_BYTES)


def _lane_concat_rows(x):
    return jnp.concatenate([x[r:r + 1, :] for r in range(x.shape[0])], axis=1)


def _block_diag_slabs(blocks):
    slabs = []
    for s in range(N_SLABS):
        rows = []
        for gi in range(GROUPS_PER_SLAB):
            pieces = []
            if gi > 0:
                pieces.append(jnp.zeros((SSM_GROUP, gi * SSM_STATE), F32))
            pieces.append(blocks[s * GROUPS_PER_SLAB + gi])
            if gi < GROUPS_PER_SLAB - 1:
                pieces.append(jnp.zeros((SSM_GROUP, (GROUPS_PER_SLAB - 1 - gi) * SSM_STATE), F32))
            rows.append(jnp.concatenate(pieces, axis=1))
        slabs.append(jnp.concatenate(rows, axis=0))
    return slabs


def _prep_kernel(are_ref, aim_ref, logdt_ref, bre_ref, bim_ref, cre_ref, cim_ref, d_ref, g_ref,
                 lq1_ref, lk1_ref, lq2_ref, lk2_ref,
                 lbre_ref, lbim_ref, bbre_ref, bbim_ref, ccre_ref, ccim_ref, dd_ref, gg_ref, lam_ref,
                 *, lam_init):
    g = N_SSM_GROUPS
    eye = lax.broadcasted_iota(jnp.int32, (g, g), 0) == lax.broadcasted_iota(jnp.int32, (g, g), 1)
    logdt = jnp.sum(jnp.where(eye, jnp.broadcast_to(logdt_ref[...], (g, g)), 0.0), axis=-1, keepdims=True)
    dt = jnp.exp(logdt)
    ar = are_ref[...]
    ai = aim_ref[...]
    mag = jnp.exp(ar * dt)
    lb_re = mag * jnp.cos(ai * dt)
    lb_im = mag * jnp.sin(ai * dt)
    nr = lb_re - 1.0
    ni = lb_im
    den = ar * ar + ai * ai
    f_re = (nr * ar + ni * ai) / den
    f_im = (ni * ar - nr * ai) / den
    lbre_ref[...] = _lane_concat_rows(lb_re)
    lbim_ref[...] = _lane_concat_rows(lb_im)
    br = bre_ref[...]
    bi = bim_ref[...]
    bb_re = f_re[:, None, :] * br - f_im[:, None, :] * bi
    bb_im = f_re[:, None, :] * bi + f_im[:, None, :] * br
    for out_ref, blocks in ((bbre_ref, bb_re), (bbim_ref, bb_im), (ccre_ref, cre_ref[...]), (ccim_ref, cim_ref[...])):
        for s, slab in enumerate(_block_diag_slabs(blocks)):
            out_ref[s] = slab.astype(BF16)
    dd_ref[...] = _lane_concat_rows(d_ref[...])
    gg_ref[...] = jnp.concatenate([g_ref[...]] * HEADS_PER_SLAB, axis=1)
    s1 = jnp.sum(lq1_ref[...] * lk1_ref[...], axis=-1, keepdims=True)
    s2 = jnp.sum(lq2_ref[...] * lk2_ref[...], axis=-1, keepdims=True)
    lam_ref[...] = jnp.exp(s1) - jnp.exp(s2) + lam_init


def _prep(l, a_re, a_im, log_dt, b_re, b_im, c_re, c_im, d, subln_g, lq1, lk1, lq2, lk2, lam_init):
    g, p, c = N_SSM_GROUPS, SSM_STATE, SSM_GROUP
    layer = lambda *shape: pl.BlockSpec((None,) + shape, lambda: (l,) + (0,) * len(shape))
    row = lambda a: a.reshape(a.shape[0], 1, a.shape[1])
    slab = jax.ShapeDtypeStruct((N_SLABS, LANES, SLAB_STATES), BF16)
    out_shape = (jax.ShapeDtypeStruct((1, SSM_LANES), F32), jax.ShapeDtypeStruct((1, SSM_LANES), F32),
                 slab, slab, slab, slab,
                 jax.ShapeDtypeStruct((1, SSM_WIDTH), F32), jax.ShapeDtypeStruct((1, LANES), F32),
                 jax.ShapeDtypeStruct((1, 1), F32))
    return pl.pallas_call(
        functools.partial(_prep_kernel, lam_init=lam_init),
        in_specs=[layer(g, p), layer(g, p), layer(1, g), layer(g, c, p), layer(g, c, p), layer(g, c, p),
                  layer(g, c, p), layer(g, c), layer(1, HEAD_DIM)] + [layer(1, HALF_DIM)] * 4,
        out_shape=out_shape, name="prep",
    )(a_re, a_im, row(log_dt), jnp.swapaxes(b_re, 2, 3), jnp.swapaxes(b_im, 2, 3), c_re, c_im, d,
      row(subln_g), row(lq1), row(lk1), row(lq2), row(lk2))


def _rope_tables(positions):
    inv = ROPE_THETA ** (-np.arange(0, ROT_DIM, 2, dtype=np.float64) / ROT_DIM)
    ang = np.asarray(positions, np.float64)[:, None] * inv[None, :]
    r = np.arange(LANES) % HALF_DIM
    half = ROT_DIM // 2
    idx = r % half
    cos = np.where(r[None, :] < ROT_DIM, np.cos(ang)[:, idx], 1.0)
    sin = np.sin(ang)[:, idx]
    s_up = np.where(r[None, :] < half, -sin, 0.0)
    s_dn = np.where((r[None, :] >= half) & (r[None, :] < ROT_DIM), sin, 0.0)
    return (jnp.asarray(cos, F32), jnp.asarray(s_up, F32), jnp.asarray(s_dn, F32))


def _inproj_kernel(x_ref, w_ref, cos_ref, sup_ref, sdn_ref, q_ref, k_ref, v_ref, u_ref, *, nb, tl):
    half = ROT_DIM // 2
    x = x_ref[...].reshape(nb * tl, D_MODEL).astype(BF16)
    cos = jnp.concatenate([cos_ref[...]] * nb, axis=0)
    sup = jnp.concatenate([sup_ref[...]] * nb, axis=0)
    sdn = jnp.concatenate([sdn_ref[...]] * nb, axis=0)

    def rope(t):
        slabs = []
        for j in range(ATTN_WIDTH // LANES):
            s = t[:, j * LANES:(j + 1) * LANES]
            slabs.append(s * cos + pltpu.roll(s, LANES - half, 1) * sup + pltpu.roll(s, half, 1) * sdn)
        return jnp.concatenate(slabs, axis=1)

    q = jnp.dot(x, w_ref[:, 0:ATTN_WIDTH], preferred_element_type=F32)
    q_ref[...] = (rope(q) * (QK_SCALE * LOG2_E)).astype(BF16).reshape(nb, tl, ATTN_WIDTH)
    k = jnp.dot(x, w_ref[:, ATTN_WIDTH:2 * ATTN_WIDTH], preferred_element_type=F32)
    k_ref[...] = rope(k).reshape(nb, tl, ATTN_WIDTH)
    v = jnp.dot(x, w_ref[:, 2 * ATTN_WIDTH:3 * ATTN_WIDTH], preferred_element_type=F32)
    v_ref[...] = v.reshape(nb, tl, ATTN_WIDTH)
    u = jnp.dot(x, w_ref[:, 3 * ATTN_WIDTH:], preferred_element_type=F32)
    first_slot = (pl.program_id(1) * nb) % SUBLANES
    for b in range(nb):
        for s in range(N_SLABS):
            u_ref[s, pl.ds(first_slot + b, tl, stride=SUBLANES), :] = u[b * tl:(b + 1) * tl,
                                                                        s * LANES:(s + 1) * LANES]


def _inproj(l, x, w_bf, tables, nb, tl):
    bsz, seq, _ = x.shape
    nblk = bsz // SUBLANES
    per_blk = SUBLANES // nb
    grid = (seq // tl, bsz // nb)
    tab_spec = pl.BlockSpec((tl, LANES), lambda i, b: (i, 0))
    act_spec = lambda width: pl.BlockSpec((nb, tl, width), lambda i, b: (b, i, 0))
    out_shape = (jax.ShapeDtypeStruct((bsz, seq, ATTN_WIDTH), BF16),
                 jax.ShapeDtypeStruct((bsz, seq, ATTN_WIDTH), F32),
                 jax.ShapeDtypeStruct((bsz, seq, ATTN_WIDTH), F32),
                 jax.ShapeDtypeStruct((nblk, N_SLABS, seq * SUBLANES, LANES), F32))
    return pl.pallas_call(
        functools.partial(_inproj_kernel, nb=nb, tl=tl),
        grid=grid,
        in_specs=[act_spec(D_MODEL),
                  pl.BlockSpec((None, D_MODEL, 4 * ATTN_WIDTH), lambda i, b: (l, 0, 0)),
                  tab_spec, tab_spec, tab_spec],
        out_specs=(act_spec(ATTN_WIDTH), act_spec(ATTN_WIDTH), act_spec(ATTN_WIDTH),
                   pl.BlockSpec((None, N_SLABS, tl * SUBLANES, LANES), lambda i, b: (b // per_blk, 0, i, 0))),
        out_shape=out_shape,
        compiler_params=_params(("parallel", "arbitrary")),
        name="inproj",
    )(x, w_bf, *tables)


def _expand_maps(q):
    lane = lax.broadcasted_iota(jnp.int32, q.shape, 1)
    zero = jnp.zeros_like(q)
    return jnp.concatenate(
        [jnp.where((lane >= m * HALF_DIM) & (lane < (m + 1) * HALF_DIM), q, zero) for m in range(MAPS_PER_SLAB)],
        axis=0)


def _diff_combine(o, lam, g, n, lam_init):
    lane = lax.broadcasted_iota(jnp.int32, (n, LANES), 1)
    first = lane < HEAD_DIM
    d = jnp.where(first, o[0:n] - lam * o[n:2 * n], o[2 * n:3 * n] - lam * o[3 * n:4 * n])
    sq = d * d
    ss_a = jnp.sum(jnp.where(first, sq, 0.0), axis=-1, keepdims=True)
    ss_b = jnp.sum(jnp.where(first, 0.0, sq), axis=-1, keepdims=True)
    ms = jnp.where(first, ss_a, ss_b) * (1.0 / HEAD_DIM)
    return d * lax.rsqrt(ms + SUBLN_EPS) * g * (1.0 - lam_init)


def _attn_prompt_kernel(lam_ref, q_ref, k_ref, v_ref, g_ref, o_ref, kb, vb, qx_a, qx_b, s_a, s_b, m_ref, acc_ref,
                        *, blk, lam_init):
    seq = q_ref.shape[0]
    hb = blk // 2
    half = MAPS_PER_SLAB * hb
    rows = 2 * half
    dims = (((1,), (1,)), ((), ()))

    kb[...] = k_ref[...].astype(BF16)
    vb[:, 0:LANES] = v_ref[...].astype(BF16)
    vb[:, LANES:2 * LANES] = jnp.ones(v_ref.shape, BF16)

    steps = [(i, j) for i in range(seq // blk) for j in range(i + 1)]
    qx_bufs = (qx_a, qx_b)
    s_bufs = (s_a, s_b)

    def expand_queries(i):
        qx = qx_bufs[i % 2]
        qx[0:half, :] = _expand_maps(q_ref[i * blk:i * blk + hb, :])
        qx[half:rows, :] = _expand_maps(q_ref[i * blk + hb:(i + 1) * blk, :])

    def scores(n):
        i, j = steps[n]
        s_bufs[n % 2][...] = lax.dot_general(qx_bufs[i % 2][...], kb[j * blk:(j + 1) * blk, :], dims,
                                             preferred_element_type=F32)

    def update(j, src, h, width, row_offset):
        rs = slice(h * half, (h + 1) * half)
        s = src[rs, 0:width]
        if row_offset is not None:
            r = lax.broadcasted_iota(jnp.int32, (half, width), 0)
            c = lax.broadcasted_iota(jnp.int32, (half, width), 1)
            s = jnp.where((row_offset + r % hb) // CHUNK >= c // CHUNK, s, -jnp.inf)
        m_prev = m_ref[rs, :]
        m_new = jnp.maximum(m_prev, jnp.max(s, axis=-1, keepdims=True))
        alpha = jnp.exp2(m_prev - m_new)
        p = jnp.exp2(s - jnp.concatenate([m_new] * (width // LANES), axis=1)).astype(BF16)
        pv = jnp.dot(p, vb[j * blk:j * blk + width, :], preferred_element_type=F32)
        acc_ref[rs, :] = jnp.concatenate([alpha, alpha], axis=1) * acc_ref[rs, :] + pv
        m_ref[rs, :] = m_new

    expand_queries(0)
    scores(0)
    for n, (i, j) in enumerate(steps):
        if n + 1 < len(steps):
            if steps[n + 1][1] == 0:
                expand_queries(steps[n + 1][0])
            scores(n + 1)
        if j == 0:
            m_ref[...] = jnp.full((rows, LANES), -jnp.inf, F32)
            acc_ref[...] = jnp.zeros((rows, 2 * LANES), F32)
        src = s_bufs[n % 2]
        if j < i:
            update(j, src, 0, blk, None)
            update(j, src, 1, blk, None)
        else:
            update(j, src, 0, hb, 0)
            update(j, src, 1, blk, hb)
            for h in range(2):
                acc = acc_ref[h * half:(h + 1) * half, :]
                o = acc[:, 0:LANES] / acc[:, LANES:2 * LANES]
                o_ref[i * blk + h * hb:i * blk + (h + 1) * hb, :] = _diff_combine(
                    o, lam_ref[0, 0], g_ref[...], hb, lam_init).astype(BF16)


def _attn_prompt(lam, q, k, v, g128, blk, lam_init):
    bsz, seq, _ = q.shape
    rows = MAPS_PER_SLAB * blk
    slab_spec = pl.BlockSpec((None, seq, LANES), lambda b, j: (b, 0, j))
    return pl.pallas_call(
        functools.partial(_attn_prompt_kernel, blk=blk, lam_init=lam_init),
        grid=(bsz, ATTN_WIDTH // LANES),
        in_specs=[pl.BlockSpec(memory_space=pltpu.SMEM), slab_spec, slab_spec, slab_spec,
                  pl.BlockSpec((1, LANES), lambda b, j: (0, 0))],
        out_specs=slab_spec,
        out_shape=jax.ShapeDtypeStruct((bsz, seq, ATTN_WIDTH), BF16),
        scratch_shapes=[pltpu.VMEM((seq, LANES), BF16), pltpu.VMEM((seq, 2 * LANES), BF16),
                        pltpu.VMEM((rows, LANES), BF16), pltpu.VMEM((rows, LANES), BF16),
                        pltpu.VMEM((rows, blk), F32), pltpu.VMEM((rows, blk), F32),
                        pltpu.VMEM((rows, LANES), F32), pltpu.VMEM((rows, 2 * LANES), F32)],
        compiler_params=_params(("parallel", "parallel")),
        name="attn_prompt",
    )(lam, q, k, v, g128)


def _attn_sample_kernel(lam_ref, q_ref, kc_ref, vc_ref, kn_ref, vn_ref, g_ref, o_ref, *, n, lam_init):
    dims = (((1,), (1,)), ((), ()))
    for j in range(ATTN_WIDTH // LANES):
        cols = slice(j * LANES, (j + 1) * LANES)
        qx = _expand_maps(q_ref[:, cols])
        s_c = jnp.dot(qx, kc_ref[cols, :].astype(BF16), preferred_element_type=F32)
        s_n = lax.dot_general(qx, kn_ref[:, cols].astype(BF16), dims, preferred_element_type=F32)
        m = jnp.maximum(jnp.max(s_c, axis=-1, keepdims=True), jnp.max(s_n, axis=-1, keepdims=True))
        p_c = jnp.exp2(s_c - m)
        p_n = jnp.exp2(s_n - m)
        l = jnp.sum(p_c, axis=-1, keepdims=True) + jnp.sum(p_n, axis=-1, keepdims=True)
        acc = (lax.dot_general(p_c.astype(BF16), vc_ref[cols, :].astype(BF16), dims, preferred_element_type=F32)
               + jnp.dot(p_n.astype(BF16), vn_ref[:, cols].astype(BF16), preferred_element_type=F32))
        o_ref[:, cols] = _diff_combine(acc / l, lam_ref[0, 0], g_ref[...], n, lam_init).astype(BF16)


def _attn_sample(lam, q, cache_kt, cache_vt, k_new, v_new, g128, lam_init):
    bsz, n, _ = q.shape
    past = cache_kt.shape[2]
    assert past % CHUNK == 0 and n <= CHUNK
    cache_spec = pl.BlockSpec((None, ATTN_WIDTH, past), lambda b: (b, 0, 0))
    new_spec = pl.BlockSpec((None, n, ATTN_WIDTH), lambda b: (b, 0, 0))
    return pl.pallas_call(
        functools.partial(_attn_sample_kernel, n=n, lam_init=lam_init),
        grid=(bsz,),
        in_specs=[pl.BlockSpec(memory_space=pltpu.SMEM), new_spec, cache_spec, cache_spec, new_spec, new_spec,
                  pl.BlockSpec((1, LANES), lambda b: (0, 0))],
        out_specs=new_spec,
        out_shape=jax.ShapeDtypeStruct((bsz, n, ATTN_WIDTH), BF16),
        compiler_params=_params(("parallel",)),
        name="attn_sample",
    )(lam, q, cache_kt, cache_vt, k_new, v_new, g128)


def _ssm_kernel(u_ref, h0re_ref, h0im_ref, are_ref, aim_ref, bre_ref, bim_ref, cre_ref, cim_ref, d_ref,
                wglu_ref, bglu_ref, out_ref, hre_out, him_out, bure, buim, hsre, hsim, hre, him, res,
                *, tt):
    i = pl.program_id(1)
    dims = (((1,), (1,)), ((), ()))

    @pl.when(i == 0)
    def _():
        for b in range(SUBLANES):
            hre[b:b + 1, :] = _lane_concat_rows(h0re_ref[b])
            him[b:b + 1, :] = _lane_concat_rows(h0im_ref[b])

    ys = []
    for s in range(N_SLABS):
        us = u_ref[s].astype(BF16)
        cols = slice(s * SLAB_STATES, (s + 1) * SLAB_STATES)
        bure[:, cols] = jnp.dot(us, bre_ref[s], preferred_element_type=F32)
        buim[:, cols] = jnp.dot(us, bim_ref[s], preferred_element_type=F32)
        ar = jnp.broadcast_to(are_ref[:, cols], (SUBLANES, SLAB_STATES))
        ai = jnp.broadcast_to(aim_ref[:, cols], (SUBLANES, SLAB_STATES))
        hr = hre[:, cols]
        hi = him[:, cols]
        for t in range(tt):
            r = slice(t * SUBLANES, (t + 1) * SUBLANES)
            hr, hi = ar * hr - ai * hi + bure[r, cols], ar * hi + ai * hr + buim[r, cols]
            hsre[r, cols] = hr
            hsim[r, cols] = hi
        hre[:, cols] = hr
        him[:, cols] = hi
        ys.append(lax.dot_general(hsre[:, cols].astype(BF16), cre_ref[s], dims, preferred_element_type=F32)
                  - lax.dot_general(hsim[:, cols].astype(BF16), cim_ref[s], dims, preferred_element_type=F32)
                  + d_ref[:, s * LANES:(s + 1) * LANES] * u_ref[s])
    y = jnp.concatenate(ys, axis=1)
    z = jnp.dot(jax.nn.gelu(y).astype(BF16), wglu_ref[...], preferred_element_type=F32) + bglu_ref[...]
    gated = z[:, :SSM_WIDTH] * jax.nn.sigmoid(z[:, SSM_WIDTH:])
    for s in range(N_SLABS):
        res[s] = gated[:, s * LANES:(s + 1) * LANES]
    for b in range(SUBLANES):
        for s in range(N_SLABS):
            out_ref[b, :, s * LANES:(s + 1) * LANES] = res[s, pl.ds(b, tt, stride=SUBLANES), :].astype(BF16)

    @pl.when(i == pl.num_programs(1) - 1)
    def _():
        for b in range(SUBLANES):
            for g in range(N_SSM_GROUPS):
                lanes = slice(g * SSM_STATE, (g + 1) * SSM_STATE)
                hre_out[b, g:g + 1, :] = hre[b:b + 1, lanes]
                him_out[b, g:g + 1, :] = him[b:b + 1, lanes]


def _ssm(l, u_t, h0_re, h0_im, lb_re, lb_im, bre, bim, cre, cim, d, w_glu, b_glu, tt):
    nblk, _, rows_total, _ = u_t.shape
    seq = rows_total // SUBLANES
    rows = SUBLANES * tt
    grid = (nblk, seq // tt)
    const = lambda shape: pl.BlockSpec(shape, lambda b, i: (0,) * len(shape))
    layer = lambda *shape: pl.BlockSpec((None,) + shape, lambda b, i: (l,) + (0,) * len(shape))
    slab_spec = const((N_SLABS, LANES, SLAB_STATES))
    state_spec = pl.BlockSpec((SUBLANES, N_SSM_GROUPS, SSM_STATE), lambda b, i: (b, 0, 0))
    state_shape = jax.ShapeDtypeStruct((nblk * SUBLANES, N_SSM_GROUPS, SSM_STATE), F32)
    return pl.pallas_call(
        functools.partial(_ssm_kernel, tt=tt),
        grid=grid,
        in_specs=[pl.BlockSpec((None, N_SLABS, rows, LANES), lambda b, i: (b, 0, i, 0)),
                  state_spec, state_spec,
                  const((1, SSM_LANES)), const((1, SSM_LANES)),
                  slab_spec, slab_spec, slab_spec, slab_spec,
                  const((1, SSM_WIDTH)), layer(SSM_WIDTH, 2 * SSM_WIDTH), layer(1, 2 * SSM_WIDTH)],
        out_specs=(pl.BlockSpec((SUBLANES, tt, SSM_WIDTH), lambda b, i: (b, i, 0)), state_spec, state_spec),
        out_shape=(jax.ShapeDtypeStruct((nblk * SUBLANES, seq, SSM_WIDTH), BF16), state_shape, state_shape),
        scratch_shapes=[pltpu.VMEM((rows, SSM_LANES), F32) for _ in range(4)]
                       + [pltpu.VMEM((SUBLANES, SSM_LANES), F32) for _ in range(2)]
                       + [pltpu.VMEM((N_SLABS, rows, LANES), F32)],
        compiler_params=_params(("parallel", "arbitrary")),
        name="ssm",
    )(u_t, h0_re, h0_im, lb_re, lb_im, bre, bim, cre, cim, d, w_glu, b_glu)


def _layer_norm(x, g, b):
    mu = jnp.mean(x, axis=-1, keepdims=True)
    xc = x - mu
    var = jnp.mean(xc * xc, axis=-1, keepdims=True)
    return xc * lax.rsqrt(var + LN_EPS) * g + b


def _post_kernel(x_ref, a_ref, s_ref, wout_ref, g1_ref, b1_ref, wup_ref, wdown_ref, g2_ref, b2_ref, o_ref,
                 *, nb, tl, ff_chunk):
    chains = (0, 1)
    n = nb * tl // len(chains)
    if nb == 1:
        views = [(slice(None), slice(c * n, (c + 1) * n)) for c in chains]
    else:
        views = [(slice(c * nb // len(chains), (c + 1) * nb // len(chains)), slice(None)) for c in chains]
    xs = [x_ref[v].reshape(n, D_MODEL) for v in views]
    mix = [jnp.dot(jnp.concatenate([a_ref[v].reshape(n, ATTN_WIDTH), s_ref[v].reshape(n, SSM_WIDTH)], axis=1),
                   wout_ref[...], preferred_element_type=F32) for v in views]
    x1 = [None, None]
    x1b = [None, None]
    h = [None, None]
    ff = [jnp.zeros((n, D_MODEL), F32) for _ in chains]
    n_chunks = D_FF // ff_chunk

    def up(c, k):
        return jnp.dot(x1b[c], wup_ref[:, k * ff_chunk:(k + 1) * ff_chunk], preferred_element_type=F32)

    def down(c, k):
        act = jnp.square(jnp.maximum(h[c], 0.0)).astype(BF16)
        return ff[c] + jnp.dot(act, wdown_ref[k * ff_chunk:(k + 1) * ff_chunk, :], preferred_element_type=F32)

    x1[0] = _layer_norm(DEEPNORM_ALPHA * xs[0] + mix[0], g1_ref[...], b1_ref[...])
    x1b[0] = x1[0].astype(BF16)
    h[0] = up(0, 0)
    x1[1] = _layer_norm(DEEPNORM_ALPHA * xs[1] + mix[1], g1_ref[...], b1_ref[...])
    x1b[1] = x1[1].astype(BF16)
    for k in range(n_chunks):
        h[1] = up(1, k)
        ff[0] = down(0, k)
        if k + 1 < n_chunks:
            h[0] = up(0, k + 1)
        ff[1] = down(1, k)
    for c in chains:
        x2 = _layer_norm(DEEPNORM_ALPHA * x1[c] + ff[c], g2_ref[...], b2_ref[...])
        o_ref[views[c]] = x2.reshape(x_ref[views[c]].shape)


def _post(l, x, attn, ssm, w_out, g1, b1, w_up, w_down, g2, b2, nb, tl):
    bsz, seq, _ = x.shape
    grid = (bsz // nb, seq // tl)
    act_spec = lambda width: pl.BlockSpec((nb, tl, width), lambda b, i: (b, i, 0))
    const = lambda shape: pl.BlockSpec((None,) + shape, lambda b, i: (l,) + (0,) * len(shape),
                                       pipeline_mode=pl.Buffered(1))
    return pl.pallas_call(
        functools.partial(_post_kernel, nb=nb, tl=tl, ff_chunk=1024),
        grid=grid,
        in_specs=[act_spec(D_MODEL), act_spec(ATTN_WIDTH), act_spec(SSM_WIDTH),
                  const((D_MODEL, D_MODEL)), const((1, D_MODEL)), const((1, D_MODEL)),
                  const((D_MODEL, D_FF)), const((D_FF, D_MODEL)), const((1, D_MODEL)), const((1, D_MODEL))],
        out_specs=act_spec(D_MODEL),
        out_shape=jax.ShapeDtypeStruct((bsz, seq, D_MODEL), F32),
        compiler_params=_params(("parallel", "parallel")),
        name="post",
    )(x, attn, ssm, w_out, g1, b1, w_up, w_down, g2, b2)


def kernel(x_prompt, x_sample, cache_k, cache_v, state_ssm_re, state_ssm_im, w_in, lambda_q1, lambda_k1,
           lambda_q2, lambda_k2, subln_g, ssm_a_re, ssm_a_im, ssm_log_dt, ssm_b_re, ssm_b_im, ssm_c_re,
           ssm_c_im, ssm_d, w_glu, b_glu, w_out, ln1_g, ln1_b, w_up, w_down, ln2_g, ln2_b):
    assert w_in.shape[0] == DEPTH
    l = 0
    lam_init = _lambda_init(l)
    bp, seq, _ = x_prompt.shape
    bs, dec_seq, _ = x_sample.shape
    past = cache_k.shape[2]

    lb_re, lb_im, bre, bim, cre, cim, d, g128, lam = _prep(
        l, ssm_a_re, ssm_a_im, ssm_log_dt, ssm_b_re, ssm_b_im, ssm_c_re, ssm_c_im, ssm_d, subln_g,
        lambda_q1, lambda_k1, lambda_q2, lambda_k2, lam_init)
    w_in_bf, w_glu_bf, w_out_bf, w_up_bf, w_down_bf = (w.astype(BF16) for w in (w_in, w_glu, w_out, w_up, w_down))
    row = lambda a: a.reshape(a.shape[0], 1, a.shape[1])
    b_glu3 = row(b_glu)
    ln = [row(a) for a in (ln1_g, ln1_b, ln2_g, ln2_b)]

    def layer(x, positions, cache, h0_re, h0_im, nb, tl, tt):
        bsz, n, _ = x.shape
        q, k, v, u_t = _inproj(l, x, w_in_bf, _rope_tables(positions), nb, tl)
        if cache is None:
            attn = _attn_prompt(lam, q, k, v, g128, 512, lam_init)
        else:
            attn = _attn_sample(lam, q, cache[0], cache[1], k, v, g128, lam_init)
        ssm, h_re, h_im = _ssm(l, u_t, h0_re, h0_im, lb_re, lb_im, bre, bim, cre, cim, d, w_glu_bf, b_glu3, tt)
        y = _post(l, x, attn, ssm, w_out_bf, ln[0], ln[1], w_up_bf, w_down_bf, ln[2], ln[3], nb, tl)
        shape_kv = (1, bsz, n, N_HEADS, HEAD_DIM)
        return y, k.reshape(shape_kv), v.reshape(shape_kv), h_re[None], h_im[None]

    zeros = jnp.zeros((bp, N_SSM_GROUPS, SSM_STATE), F32)
    yp, kp, vp, rp, ip = layer(x_prompt, np.arange(seq), None, zeros, zeros, nb=1, tl=512, tt=64)
    cache = tuple(jnp.transpose(c[l], (0, 2, 3, 1)).reshape(bs, ATTN_WIDTH, past) for c in (cache_k, cache_v))
    ys, ks, vs, rs, is_ = layer(x_sample, past + np.arange(dec_seq), cache, state_ssm_re[l], state_ssm_im[l],
                                nb=SUBLANES, tl=dec_seq, tt=dec_seq)
    return (yp, ys, kp, vp, rp, ip, ks, vs, rs, is_)
```

```python
import functools
import math

import numpy as np
import jax
import jax.numpy as jnp
from jax import lax
from jax.experimental import pallas as pl
from jax.experimental.pallas import tpu as pltpu

D_MODEL = 1024
DEPTH = 1
CHUNK = 64
ATTN_WIDTH = 512
SSM_WIDTH = 512
N_HEADS = 8
HEAD_DIM = 64
HALF_DIM = 32
ROT_DIM = 8
ROPE_THETA = 500000.0
SSM_GROUP = 16
N_SSM_GROUPS = 32
SSM_STATE = 64
D_FF = 4 * D_MODEL
LN_EPS = 1e-5
SUBLN_EPS = 1e-5
DEEPNORM_ALPHA = (2 * DEPTH) ** 0.25
QK_SCALE = HALF_DIM ** -0.5
LOG2_E = math.log2(math.e)

SUBLANES = 8
LANES = 128
SSM_LANES = N_SSM_GROUPS * SSM_STATE
GROUPS_PER_SLAB = LANES // SSM_GROUP
N_SLABS = SSM_WIDTH // LANES
SLAB_STATES = GROUPS_PER_SLAB * SSM_STATE
HEADS_PER_SLAB = LANES // HEAD_DIM
MAPS_PER_SLAB = LANES // HALF_DIM
VMEM_LIMIT_BYTES = 56 * 1024 * 1024

F32 = jnp.float32
BF16 = jnp.bfloat16


def _lambda_init(layer_idx):
    return 0.8 - 0.6 * math.exp(-0.3 * layer_idx)


def _params(semantics):
    return pltpu.CompilerParams(dimension_semantics=semantics, vmem_limit_bytes=VMEM_LIMIT_BYTES)


def _lane_concat_rows(x):
    return jnp.concatenate([x[r:r + 1, :] for r in range(x.shape[0])], axis=1)


def _block_diag_slabs(blocks):
    slabs = []
    for s in range(N_SLABS):
        rows = []
        for gi in range(GROUPS_PER_SLAB):
            pieces = []
            if gi > 0:
                pieces.append(jnp.zeros((SSM_GROUP, gi * SSM_STATE), F32))
            pieces.append(blocks[s * GROUPS_PER_SLAB + gi])
            if gi < GROUPS_PER_SLAB - 1:
                pieces.append(jnp.zeros((SSM_GROUP, (GROUPS_PER_SLAB - 1 - gi) * SSM_STATE), F32))
            rows.append(jnp.concatenate(pieces, axis=1))
        slabs.append(jnp.concatenate(rows, axis=0))
    return slabs


def _prep_kernel(are_ref, aim_ref, logdt_ref, bre_ref, bim_ref, cre_ref, cim_ref, d_ref, g_ref,
                 lq1_ref, lk1_ref, lq2_ref, lk2_ref,
                 lbre_ref, lbim_ref, bbre_ref, bbim_ref, ccre_ref, ccim_ref, dd_ref, gg_ref, lam_ref,
                 *, lam_init):
    g = N_SSM_GROUPS
    eye = lax.broadcasted_iota(jnp.int32, (g, g), 0) == lax.broadcasted_iota(jnp.int32, (g, g), 1)
    logdt = jnp.sum(jnp.where(eye, jnp.broadcast_to(logdt_ref[...], (g, g)), 0.0), axis=-1, keepdims=True)
    dt = jnp.exp(logdt)
    ar = are_ref[...]
    ai = aim_ref[...]
    mag = jnp.exp(ar * dt)
    lb_re = mag * jnp.cos(ai * dt)
    lb_im = mag * jnp.sin(ai * dt)
    nr = lb_re - 1.0
    ni = lb_im
    den = ar * ar + ai * ai
    f_re = (nr * ar + ni * ai) / den
    f_im = (ni * ar - nr * ai) / den
    lbre_ref[...] = _lane_concat_rows(lb_re)
    lbim_ref[...] = _lane_concat_rows(lb_im)
    br = bre_ref[...]
    bi = bim_ref[...]
    bb_re = f_re[:, None, :] * br - f_im[:, None, :] * bi
    bb_im = f_re[:, None, :] * bi + f_im[:, None, :] * br
    for out_ref, blocks in ((bbre_ref, bb_re), (bbim_ref, bb_im), (ccre_ref, cre_ref[...]), (ccim_ref, cim_ref[...])):
        for s, slab in enumerate(_block_diag_slabs(blocks)):
            out_ref[s] = slab.astype(BF16)
    dd_ref[...] = _lane_concat_rows(d_ref[...])
    gg_ref[...] = jnp.concatenate([g_ref[...]] * HEADS_PER_SLAB, axis=1)
    s1 = jnp.sum(lq1_ref[...] * lk1_ref[...], axis=-1, keepdims=True)
    s2 = jnp.sum(lq2_ref[...] * lk2_ref[...], axis=-1, keepdims=True)
    lam_ref[...] = jnp.exp(s1) - jnp.exp(s2) + lam_init


def _prep(l, a_re, a_im, log_dt, b_re, b_im, c_re, c_im, d, subln_g, lq1, lk1, lq2, lk2, lam_init):
    g, p, c = N_SSM_GROUPS, SSM_STATE, SSM_GROUP
    layer = lambda *shape: pl.BlockSpec((None,) + shape, lambda: (l,) + (0,) * len(shape))
    row = lambda a: a.reshape(a.shape[0], 1, a.shape[1])
    slab = jax.ShapeDtypeStruct((N_SLABS, LANES, SLAB_STATES), BF16)
    out_shape = (jax.ShapeDtypeStruct((1, SSM_LANES), F32), jax.ShapeDtypeStruct((1, SSM_LANES), F32),
                 slab, slab, slab, slab,
                 jax.ShapeDtypeStruct((1, SSM_WIDTH), F32), jax.ShapeDtypeStruct((1, LANES), F32),
                 jax.ShapeDtypeStruct((1, 1), F32))
    return pl.pallas_call(
        functools.partial(_prep_kernel, lam_init=lam_init),
        in_specs=[layer(g, p), layer(g, p), layer(1, g), layer(g, c, p), layer(g, c, p), layer(g, c, p),
                  layer(g, c, p), layer(g, c), layer(1, HEAD_DIM)] + [layer(1, HALF_DIM)] * 4,
        out_shape=out_shape, name="prep",
    )(a_re, a_im, row(log_dt), jnp.swapaxes(b_re, 2, 3), jnp.swapaxes(b_im, 2, 3), c_re, c_im, d,
      row(subln_g), row(lq1), row(lk1), row(lq2), row(lk2))


def _rope_tables(positions):
    inv = ROPE_THETA ** (-np.arange(0, ROT_DIM, 2, dtype=np.float64) / ROT_DIM)
    ang = np.asarray(positions, np.float64)[:, None] * inv[None, :]
    r = np.arange(LANES) % HALF_DIM
    half = ROT_DIM // 2
    idx = r % half
    cos = np.where(r[None, :] < ROT_DIM, np.cos(ang)[:, idx], 1.0)
    sin = np.sin(ang)[:, idx]
    s_up = np.where(r[None, :] < half, -sin, 0.0)
    s_dn = np.where((r[None, :] >= half) & (r[None, :] < ROT_DIM), sin, 0.0)
    return (jnp.asarray(cos, F32), jnp.asarray(s_up, F32), jnp.asarray(s_dn, F32))


def _inproj_kernel(x_ref, w_ref, cos_ref, sup_ref, sdn_ref, q_ref, k_ref, v_ref, u_ref, *, nb, tl):
    half = ROT_DIM // 2
    x = x_ref[...].reshape(nb * tl, D_MODEL).astype(BF16)
    cos = jnp.concatenate([cos_ref[...]] * nb, axis=0)
    sup = jnp.concatenate([sup_ref[...]] * nb, axis=0)
    sdn = jnp.concatenate([sdn_ref[...]] * nb, axis=0)

    def rope(t):
        slabs = []
        for j in range(ATTN_WIDTH // LANES):
            s = t[:, j * LANES:(j + 1) * LANES]
            slabs.append(s * cos + pltpu.roll(s, LANES - half, 1) * sup + pltpu.roll(s, half, 1) * sdn)
        return jnp.concatenate(slabs, axis=1)

    q = jnp.dot(x, w_ref[:, 0:ATTN_WIDTH], preferred_element_type=F32)
    q_ref[...] = (rope(q) * (QK_SCALE * LOG2_E)).astype(BF16).reshape(nb, tl, ATTN_WIDTH)
    k = jnp.dot(x, w_ref[:, ATTN_WIDTH:2 * ATTN_WIDTH], preferred_element_type=F32)
    k_ref[...] = rope(k).reshape(nb, tl, ATTN_WIDTH)
    v = jnp.dot(x, w_ref[:, 2 * ATTN_WIDTH:3 * ATTN_WIDTH], preferred_element_type=F32)
    v_ref[...] = v.reshape(nb, tl, ATTN_WIDTH)
    u = jnp.dot(x, w_ref[:, 3 * ATTN_WIDTH:], preferred_element_type=F32)
    first_slot = (pl.program_id(1) * nb) % SUBLANES
    for b in range(nb):
        for s in range(N_SLABS):
            u_ref[s, pl.ds(first_slot + b, tl, stride=SUBLANES), :] = u[b * tl:(b + 1) * tl,
                                                                        s * LANES:(s + 1) * LANES]


def _inproj(x, w_bf, tables, nb, tl):
    bsz, seq, _ = x.shape
    nblk = bsz // SUBLANES
    per_blk = SUBLANES // nb
    grid = (seq // tl, bsz // nb)
    tab_spec = pl.BlockSpec((tl, LANES), lambda i, b: (i, 0))
    act_spec = lambda width: pl.BlockSpec((nb, tl, width), lambda i, b: (b, i, 0))
    out_shape = (jax.ShapeDtypeStruct((bsz, seq, ATTN_WIDTH), BF16),
                 jax.ShapeDtypeStruct((bsz, seq, ATTN_WIDTH), F32),
                 jax.ShapeDtypeStruct((bsz, seq, ATTN_WIDTH), F32),
                 jax.ShapeDtypeStruct((nblk, N_SLABS, seq * SUBLANES, LANES), F32))
    return pl.pallas_call(
        functools.partial(_inproj_kernel, nb=nb, tl=tl),
        grid=grid,
        in_specs=[act_spec(D_MODEL),
                  pl.BlockSpec((D_MODEL, 4 * ATTN_WIDTH), lambda i, b: (0, 0)),
                  tab_spec, tab_spec, tab_spec],
        out_specs=(act_spec(ATTN_WIDTH), act_spec(ATTN_WIDTH), act_spec(ATTN_WIDTH),
                   pl.BlockSpec((None, N_SLABS, tl * SUBLANES, LANES), lambda i, b: (b // per_blk, 0, i, 0))),
        out_shape=out_shape,
        compiler_params=_params(("parallel", "arbitrary")),
        name="inproj",
    )(x, w_bf, *tables)


def _expand_maps(q):
    lane = lax.broadcasted_iota(jnp.int32, q.shape, 1)
    zero = jnp.zeros_like(q)
    return jnp.concatenate(
        [jnp.where((lane >= m * HALF_DIM) & (lane < (m + 1) * HALF_DIM), q, zero) for m in range(MAPS_PER_SLAB)],
        axis=0)


def _diff_combine(o, lam, g, n, lam_init):
    lane = lax.broadcasted_iota(jnp.int32, (n, LANES), 1)
    first = lane < HEAD_DIM
    d = jnp.where(first, o[0:n] - lam * o[n:2 * n], o[2 * n:3 * n] - lam * o[3 * n:4 * n])
    sq = d * d
    ss_a = jnp.sum(jnp.where(first, sq, 0.0), axis=-1, keepdims=True)
    ss_b = jnp.sum(jnp.where(first, 0.0, sq), axis=-1, keepdims=True)
    ms = jnp.where(first, ss_a, ss_b) * (1.0 / HEAD_DIM)
    return d * lax.rsqrt(ms + SUBLN_EPS) * g * (1.0 - lam_init)


def _attn_prompt_kernel(lam_ref, q_ref, k_ref, v_ref, g_ref, o_ref, kb, vb, qx_a, qx_b, s_a, s_b, m_ref, acc_ref,
                        *, blk, lam_init):
    seq = q_ref.shape[0]
    hb = blk // 2
    half = MAPS_PER_SLAB * hb
    rows = 2 * half
    dims = (((1,), (1,)), ((), ()))

    kb[...] = k_ref[...].astype(BF16)
    vb[:, 0:LANES] = v_ref[...].astype(BF16)
    vb[:, LANES:2 * LANES] = jnp.ones(v_ref.shape, BF16)

    steps = [(i, j) for i in range(seq // blk) for j in range(i + 1)]
    qx_bufs = (qx_a, qx_b)
    s_bufs = (s_a, s_b)

    def expand_queries(i):
        qx = qx_bufs[i % 2]
        qx[0:half, :] = _expand_maps(q_ref[i * blk:i * blk + hb, :])
        qx[half:rows, :] = _expand_maps(q_ref[i * blk + hb:(i + 1) * blk, :])

    def scores(n):
        i, j = steps[n]
        s_bufs[n % 2][...] = lax.dot_general(qx_bufs[i % 2][...], kb[j * blk:(j + 1) * blk, :], dims,
                                             preferred_element_type=F32)

    def update(j, src, h, width, row_offset):
        rs = slice(h * half, (h + 1) * half)
        s = src[rs, 0:width]
        if row_offset is not None:
            r = lax.broadcasted_iota(jnp.int32, (half, width), 0)
            c = lax.broadcasted_iota(jnp.int32, (half, width), 1)
            s = jnp.where((row_offset + r % hb) // CHUNK >= c // CHUNK, s, -jnp.inf)
        m_prev = m_ref[rs, :]
        m_new = jnp.maximum(m_prev, jnp.max(s, axis=-1, keepdims=True))
        alpha = jnp.exp2(m_prev - m_new)
        p = jnp.exp2(s - jnp.concatenate([m_new] * (width // LANES), axis=1)).astype(BF16)
        pv = jnp.dot(p, vb[j * blk:j * blk + width, :], preferred_element_type=F32)
        acc_ref[rs, :] = jnp.concatenate([alpha, alpha], axis=1) * acc_ref[rs, :] + pv
        m_ref[rs, :] = m_new

    expand_queries(0)
    scores(0)
    for n, (i, j) in enumerate(steps):
        if n + 1 < len(steps):
            if steps[n + 1][1] == 0:
                expand_queries(steps[n + 1][0])
            scores(n + 1)
        if j == 0:
            m_ref[...] = jnp.full((rows, LANES), -jnp.inf, F32)
            acc_ref[...] = jnp.zeros((rows, 2 * LANES), F32)
        src = s_bufs[n % 2]
        if j < i:
            update(j, src, 0, blk, None)
            update(j, src, 1, blk, None)
        else:
            update(j, src, 0, hb, 0)
            update(j, src, 1, blk, hb)
            for h in range(2):
                acc = acc_ref[h * half:(h + 1) * half, :]
                o = acc[:, 0:LANES] / acc[:, LANES:2 * LANES]
                o_ref[i * blk + h * hb:i * blk + (h + 1) * hb, :] = _diff_combine(
                    o, lam_ref[0, 0], g_ref[...], hb, lam_init).astype(BF16)


def _attn_prompt(lam, q, k, v, g128, blk, lam_init):
    bsz, seq, _ = q.shape
    rows = MAPS_PER_SLAB * blk
    slab_spec = pl.BlockSpec((None, seq, LANES), lambda b, j: (b, 0, j))
    return pl.pallas_call(
        functools.partial(_attn_prompt_kernel, blk=blk, lam_init=lam_init),
        grid=(bsz, ATTN_WIDTH // LANES),
        in_specs=[pl.BlockSpec(memory_space=pltpu.SMEM), slab_spec, slab_spec, slab_spec,
                  pl.BlockSpec((1, LANES), lambda b, j: (0, 0))],
        out_specs=slab_spec,
        out_shape=jax.ShapeDtypeStruct((bsz, seq, ATTN_WIDTH), BF16),
        scratch_shapes=[pltpu.VMEM((seq, LANES), BF16), pltpu.VMEM((seq, 2 * LANES), BF16),
                        pltpu.VMEM((rows, LANES), BF16), pltpu.VMEM((rows, LANES), BF16),
                        pltpu.VMEM((rows, blk), F32), pltpu.VMEM((rows, blk), F32),
                        pltpu.VMEM((rows, LANES), F32), pltpu.VMEM((rows, 2 * LANES), F32)],
        compiler_params=_params(("parallel", "parallel")),
        name="attn_prompt",
    )(lam, q, k, v, g128)


def _attn_sample_kernel(lam_ref, q_ref, kc_ref, vc_ref, kn_ref, vn_ref, g_ref, o_ref, *, n, lam_init):
    dims = (((1,), (1,)), ((), ()))
    for j in range(ATTN_WIDTH // LANES):
        cols = slice(j * LANES, (j + 1) * LANES)
        qx = _expand_maps(q_ref[:, cols])
        s_c = jnp.dot(qx, kc_ref[cols, :].astype(BF16), preferred_element_type=F32)
        s_n = lax.dot_general(qx, kn_ref[:, cols].astype(BF16), dims, preferred_element_type=F32)
        m = jnp.maximum(jnp.max(s_c, axis=-1, keepdims=True), jnp.max(s_n, axis=-1, keepdims=True))
        p_c = jnp.exp2(s_c - m)
        p_n = jnp.exp2(s_n - m)
        l = jnp.sum(p_c, axis=-1, keepdims=True) + jnp.sum(p_n, axis=-1, keepdims=True)
        acc = (lax.dot_general(p_c.astype(BF16), vc_ref[cols, :].astype(BF16), dims, preferred_element_type=F32)
               + jnp.dot(p_n.astype(BF16), vn_ref[:, cols].astype(BF16), preferred_element_type=F32))
        o_ref[:, cols] = _diff_combine(acc / l, lam_ref[0, 0], g_ref[...], n, lam_init).astype(BF16)


def _attn_sample(lam, q, cache_kt, cache_vt, k_new, v_new, g128, lam_init):
    bsz, n, _ = q.shape
    past = cache_kt.shape[2]
    assert past % CHUNK == 0 and n <= CHUNK
    cache_spec = pl.BlockSpec((None, ATTN_WIDTH, past), lambda b: (b, 0, 0))
    new_spec = pl.BlockSpec((None, n, ATTN_WIDTH), lambda b: (b, 0, 0))
    return pl.pallas_call(
        functools.partial(_attn_sample_kernel, n=n, lam_init=lam_init),
        grid=(bsz,),
        in_specs=[pl.BlockSpec(memory_space=pltpu.SMEM), new_spec, cache_spec, cache_spec, new_spec, new_spec,
                  pl.BlockSpec((1, LANES), lambda b: (0, 0))],
        out_specs=new_spec,
        out_shape=jax.ShapeDtypeStruct((bsz, n, ATTN_WIDTH), BF16),
        compiler_params=_params(("parallel",)),
        name="attn_sample",
    )(lam, q, cache_kt, cache_vt, k_new, v_new, g128)


def _ssm_kernel(u_ref, h0re_ref, h0im_ref, are_ref, aim_ref, bre_ref, bim_ref, cre_ref, cim_ref, d_ref,
                wglu_ref, bglu_ref, out_ref, hre_out, him_out, bure, buim, hsre, hsim, hre, him, res,
                *, tt):
    i = pl.program_id(1)
    dims = (((1,), (1,)), ((), ()))

    @pl.when(i == 0)
    def _():
        for b in range(SUBLANES):
            hre[b:b + 1, :] = _lane_concat_rows(h0re_ref[b])
            him[b:b + 1, :] = _lane_concat_rows(h0im_ref[b])

    ys = []
    for s in range(N_SLABS):
        us = u_ref[s].astype(BF16)
        cols = slice(s * SLAB_STATES, (s + 1) * SLAB_STATES)
        bure[:, cols] = jnp.dot(us, bre_ref[s], preferred_element_type=F32)
        buim[:, cols] = jnp.dot(us, bim_ref[s], preferred_element_type=F32)
        ar = jnp.broadcast_to(are_ref[:, cols], (SUBLANES, SLAB_STATES))
        ai = jnp.broadcast_to(aim_ref[:, cols], (SUBLANES, SLAB_STATES))
        hr = hre[:, cols]
        hi = him[:, cols]
        for t in range(tt):
            r = slice(t * SUBLANES, (t + 1) * SUBLANES)
            hr, hi = ar * hr - ai * hi + bure[r, cols], ar * hi + ai * hr + buim[r, cols]
            hsre[r, cols] = hr
            hsim[r, cols] = hi
        hre[:, cols] = hr
        him[:, cols] = hi
        ys.append(lax.dot_general(hsre[:, cols].astype(BF16), cre_ref[s], dims, preferred_element_type=F32)
                  - lax.dot_general(hsim[:, cols].astype(BF16), cim_ref[s], dims, preferred_element_type=F32)
                  + d_ref[:, s * LANES:(s + 1) * LANES] * u_ref[s])
    y = jnp.concatenate(ys, axis=1)
    z = jnp.dot(jax.nn.gelu(y).astype(BF16), wglu_ref[...], preferred_element_type=F32) + bglu_ref[...]
    gated = z[:, :SSM_WIDTH] * jax.nn.sigmoid(z[:, SSM_WIDTH:])
    for s in range(N_SLABS):
        res[s] = gated[:, s * LANES:(s + 1) * LANES]
    for b in range(SUBLANES):
        for s in range(N_SLABS):
            out_ref[b, :, s * LANES:(s + 1) * LANES] = res[s, pl.ds(b, tt, stride=SUBLANES), :].astype(BF16)

    @pl.when(i == pl.num_programs(1) - 1)
    def _():
        for b in range(SUBLANES):
            for g in range(N_SSM_GROUPS):
                lanes = slice(g * SSM_STATE, (g + 1) * SSM_STATE)
                hre_out[b, g:g + 1, :] = hre[b:b + 1, lanes]
                him_out[b, g:g + 1, :] = him[b:b + 1, lanes]


def _ssm(l, u_t, h0_re, h0_im, lb_re, lb_im, bre, bim, cre, cim, d, w_glu, b_glu, tt):
    nblk, _, rows_total, _ = u_t.shape
    seq = rows_total // SUBLANES
    rows = SUBLANES * tt
    grid = (nblk, seq // tt)
    const = lambda shape: pl.BlockSpec(shape, lambda b, i: (0,) * len(shape))
    layer = lambda *shape: pl.BlockSpec((None,) + shape, lambda b, i: (l,) + (0,) * len(shape))
    slab_spec = const((N_SLABS, LANES, SLAB_STATES))
    state_spec = pl.BlockSpec((SUBLANES, N_SSM_GROUPS, SSM_STATE), lambda b, i: (b, 0, 0))
    state_shape = jax.ShapeDtypeStruct((nblk * SUBLANES, N_SSM_GROUPS, SSM_STATE), F32)
    return pl.pallas_call(
        functools.partial(_ssm_kernel, tt=tt),
        grid=grid,
        in_specs=[pl.BlockSpec((None, N_SLABS, rows, LANES), lambda b, i: (b, 0, i, 0)),
                  state_spec, state_spec,
                  const((1, SSM_LANES)), const((1, SSM_LANES)),
                  slab_spec, slab_spec, slab_spec, slab_spec,
                  const((1, SSM_WIDTH)), const((SSM_WIDTH, 2 * SSM_WIDTH)), layer(1, 2 * SSM_WIDTH)],
        out_specs=(pl.BlockSpec((SUBLANES, tt, SSM_WIDTH), lambda b, i: (b, i, 0)), state_spec, state_spec),
        out_shape=(jax.ShapeDtypeStruct((nblk * SUBLANES, seq, SSM_WIDTH), BF16), state_shape, state_shape),
        scratch_shapes=[pltpu.VMEM((rows, SSM_LANES), F32) for _ in range(4)]
                       + [pltpu.VMEM((SUBLANES, SSM_LANES), F32) for _ in range(2)]
                       + [pltpu.VMEM((N_SLABS, rows, LANES), F32)],
        compiler_params=_params(("parallel", "arbitrary")),
        name="ssm",
    )(u_t, h0_re, h0_im, lb_re, lb_im, bre, bim, cre, cim, d, w_glu, b_glu)


def _layer_norm(x, g, b):
    mu = jnp.mean(x, axis=-1, keepdims=True)
    xc = x - mu
    var = jnp.mean(xc * xc, axis=-1, keepdims=True)
    return xc * lax.rsqrt(var + LN_EPS) * g + b


def _post_kernel(x_ref, a_ref, s_ref, wout_ref, g1_ref, b1_ref, wup_ref, wdown_ref, g2_ref, b2_ref, o_ref,
                 *, nb, tl, ff_chunk):
    chains = (0, 1)
    n = nb * tl // len(chains)
    if nb == 1:
        views = [(slice(None), slice(c * n, (c + 1) * n)) for c in chains]
    else:
        views = [(slice(c * nb // len(chains), (c + 1) * nb // len(chains)), slice(None)) for c in chains]
    xs = [x_ref[v].reshape(n, D_MODEL) for v in views]
    mix = [jnp.dot(jnp.concatenate([a_ref[v].reshape(n, ATTN_WIDTH), s_ref[v].reshape(n, SSM_WIDTH)], axis=1),
                   wout_ref[...], preferred_element_type=F32) for v in views]
    x1 = [None, None]
    x1b = [None, None]
    h = [None, None]
    ff = [jnp.zeros((n, D_MODEL), F32) for _ in chains]
    n_chunks = D_FF // ff_chunk

    def up(c, k):
        return jnp.dot(x1b[c], wup_ref[:, k * ff_chunk:(k + 1) * ff_chunk], preferred_element_type=F32)

    def down(c, k):
        act = jnp.square(jnp.maximum(h[c], 0.0)).astype(BF16)
        return ff[c] + jnp.dot(act, wdown_ref[k * ff_chunk:(k + 1) * ff_chunk, :], preferred_element_type=F32)

    x1[0] = _layer_norm(DEEPNORM_ALPHA * xs[0] + mix[0], g1_ref[...], b1_ref[...])
    x1b[0] = x1[0].astype(BF16)
    h[0] = up(0, 0)
    x1[1] = _layer_norm(DEEPNORM_ALPHA * xs[1] + mix[1], g1_ref[...], b1_ref[...])
    x1b[1] = x1[1].astype(BF16)
    for k in range(n_chunks):
        h[1] = up(1, k)
        ff[0] = down(0, k)
        if k + 1 < n_chunks:
            h[0] = up(0, k + 1)
        ff[1] = down(1, k)
    for c in chains:
        x2 = _layer_norm(DEEPNORM_ALPHA * x1[c] + ff[c], g2_ref[...], b2_ref[...])
        o_ref[views[c]] = x2.reshape(x_ref[views[c]].shape)


def _post(l, x, attn, ssm, w_out, g1, b1, w_up, w_down, g2, b2, nb, tl):
    bsz, seq, _ = x.shape
    grid = (bsz // nb, seq // tl)
    act_spec = lambda width: pl.BlockSpec((nb, tl, width), lambda b, i: (b, i, 0))
    const = lambda shape: pl.BlockSpec(shape, lambda b, i: (0,) * len(shape), pipeline_mode=pl.Buffered(1))
    layer = lambda *shape: pl.BlockSpec((None,) + shape, lambda b, i: (l,) + (0,) * len(shape))
    return pl.pallas_call(
        functools.partial(_post_kernel, nb=nb, tl=tl, ff_chunk=1024),
        grid=grid,
        in_specs=[act_spec(D_MODEL), act_spec(ATTN_WIDTH), act_spec(SSM_WIDTH),
                  const((D_MODEL, D_MODEL)), layer(1, D_MODEL), layer(1, D_MODEL),
                  const((D_MODEL, D_FF)), const((D_FF, D_MODEL)), layer(1, D_MODEL), layer(1, D_MODEL)],
        out_specs=act_spec(D_MODEL),
        out_shape=jax.ShapeDtypeStruct((bsz, seq, D_MODEL), F32),
        compiler_params=_params(("parallel", "parallel")),
        name="post",
    )(x, attn, ssm, w_out, g1, b1, w_up, w_down, g2, b2)


def kernel(x_prompt, x_sample, cache_k, cache_v, state_ssm_re, state_ssm_im, w_in, lambda_q1, lambda_k1,
           lambda_q2, lambda_k2, subln_g, ssm_a_re, ssm_a_im, ssm_log_dt, ssm_b_re, ssm_b_im, ssm_c_re,
           ssm_c_im, ssm_d, w_glu, b_glu, w_out, ln1_g, ln1_b, w_up, w_down, ln2_g, ln2_b):
    assert w_in.shape[0] == DEPTH
    l = 0
    lam_init = _lambda_init(l)
    bp, seq, _ = x_prompt.shape
    bs, dec_seq, _ = x_sample.shape
    past = cache_k.shape[2]

    lb_re, lb_im, bre, bim, cre, cim, d, g128, lam = _prep(
        l, ssm_a_re, ssm_a_im, ssm_log_dt, ssm_b_re, ssm_b_im, ssm_c_re, ssm_c_im, ssm_d, subln_g,
        lambda_q1, lambda_k1, lambda_q2, lambda_k2, lam_init)
    w_in_bf, w_glu_bf, w_out_bf, w_up_bf, w_down_bf = (w[l].astype(BF16) for w in (w_in, w_glu, w_out, w_up, w_down))
    row = lambda a: a.reshape(a.shape[0], 1, a.shape[1])
    b_glu3 = row(b_glu)
    ln = [row(a) for a in (ln1_g, ln1_b, ln2_g, ln2_b)]

    def layer(x, positions, cache, h0_re, h0_im, nb, tl, tt):
        bsz, n, _ = x.shape
        q, k, v, u_t = _inproj(x, w_in_bf, _rope_tables(positions), nb, tl)
        if cache is None:
            attn = _attn_prompt(lam, q, k, v, g128, 512, lam_init)
        else:
            attn = _attn_sample(lam, q, cache[0], cache[1], k, v, g128, lam_init)
        ssm, h_re, h_im = _ssm(l, u_t, h0_re, h0_im, lb_re, lb_im, bre, bim, cre, cim, d, w_glu_bf, b_glu3, tt)
        y = _post(l, x, attn, ssm, w_out_bf, ln[0], ln[1], w_up_bf, w_down_bf, ln[2], ln[3], nb, tl)
        shape_kv = (1, bsz, n, N_HEADS, HEAD_DIM)
        return y, k.reshape(shape_kv), v.reshape(shape_kv), h_re[None], h_im[None]

    zeros = jnp.zeros((bp, N_SSM_GROUPS, SSM_STATE), F32)
    yp, kp, vp, rp, ip = layer(x_prompt, np.arange(seq), None, zeros, zeros, nb=1, tl=512, tt=64)
    cache = tuple(jnp.transpose(c[l], (0, 2, 3, 1)).reshape(bs, ATTN_WIDTH, past) for c in (cache_k, cache_v))
    ys, ks, vs, rs, is_ = layer(x_sample, past + np.arange(dec_seq), cache, state_ssm_re[l], state_ssm_im[l],
                                nb=SUBLANES, tl=dec_seq, tt=dec_seq)
    return (yp, ys, kp, vp, rp, ip, ks, vs, rs, is_)
```

```python
import functools
import math

import numpy as np
import jax
import jax.numpy as jnp
from jax import lax
from jax.experimental import pallas as pl
from jax.experimental.pallas import tpu as pltpu

D_MODEL = 1024
DEPTH = 1
CHUNK = 64
ATTN_WIDTH = 512
SSM_WIDTH = 512
N_HEADS = 8
HEAD_DIM = 64
HALF_DIM = 32
ROT_DIM = 8
ROPE_THETA = 500000.0
SSM_GROUP = 16
N_SSM_GROUPS = 32
SSM_STATE = 64
D_FF = 4 * D_MODEL
LN_EPS = 1e-5
SUBLN_EPS = 1e-5
DEEPNORM_ALPHA = (2 * DEPTH) ** 0.25
QK_SCALE = HALF_DIM ** -0.5
LOG2_E = math.log2(math.e)

SUBLANES = 8
LANES = 128
SSM_LANES = N_SSM_GROUPS * SSM_STATE
GROUPS_PER_SLAB = LANES // SSM_GROUP
N_SLABS = SSM_WIDTH // LANES
SLAB_STATES = GROUPS_PER_SLAB * SSM_STATE
HEADS_PER_SLAB = LANES // HEAD_DIM
MAPS_PER_SLAB = LANES // HALF_DIM
VMEM_LIMIT_BYTES = 56 * 1024 * 1024

F32 = jnp.float32
BF16 = jnp.bfloat16


def _lambda_init(layer_idx):
    return 0.8 - 0.6 * math.exp(-0.3 * layer_idx)


def _params(semantics):
    return pltpu.CompilerParams(dimension_semantics=semantics, vmem_limit_bytes=VMEM_LIMIT_BYTES)


def _lane_concat_rows(x):
    return jnp.concatenate([x[r:r + 1, :] for r in range(x.shape[0])], axis=1)


def _block_diag_slabs(blocks):
    slabs = []
    for s in range(N_SLABS):
        rows = []
        for gi in range(GROUPS_PER_SLAB):
            pieces = []
            if gi > 0:
                pieces.append(jnp.zeros((SSM_GROUP, gi * SSM_STATE), F32))
            pieces.append(blocks[s * GROUPS_PER_SLAB + gi])
            if gi < GROUPS_PER_SLAB - 1:
                pieces.append(jnp.zeros((SSM_GROUP, (GROUPS_PER_SLAB - 1 - gi) * SSM_STATE), F32))
            rows.append(jnp.concatenate(pieces, axis=1))
        slabs.append(jnp.concatenate(rows, axis=0))
    return slabs


def _prep_kernel(are_ref, aim_ref, logdt_ref, bre_ref, bim_ref, cre_ref, cim_ref, d_ref, g_ref,
                 lq1_ref, lk1_ref, lq2_ref, lk2_ref,
                 lbre_ref, lbim_ref, bbre_ref, bbim_ref, ccre_ref, ccim_ref, dd_ref, gg_ref, lam_ref,
                 *, lam_init):
    g = N_SSM_GROUPS
    eye = lax.broadcasted_iota(jnp.int32, (g, g), 0) == lax.broadcasted_iota(jnp.int32, (g, g), 1)
    logdt = jnp.sum(jnp.where(eye, jnp.broadcast_to(logdt_ref[...], (g, g)), 0.0), axis=-1, keepdims=True)
    dt = jnp.exp(logdt)
    ar = are_ref[...]
    ai = aim_ref[...]
    mag = jnp.exp(ar * dt)
    lb_re = mag * jnp.cos(ai * dt)
    lb_im = mag * jnp.sin(ai * dt)
    nr = lb_re - 1.0
    ni = lb_im
    den = ar * ar + ai * ai
    f_re = (nr * ar + ni * ai) / den
    f_im = (ni * ar - nr * ai) / den
    lbre_ref[...] = _lane_concat_rows(lb_re)
    lbim_ref[...] = _lane_concat_rows(lb_im)
    br = bre_ref[...]
    bi = bim_ref[...]
    bb_re = f_re[:, None, :] * br - f_im[:, None, :] * bi
    bb_im = f_re[:, None, :] * bi + f_im[:, None, :] * br
    for out_ref, blocks in ((bbre_ref, bb_re), (bbim_ref, bb_im), (ccre_ref, cre_ref[...]), (ccim_ref, cim_ref[...])):
        for s, slab in enumerate(_block_diag_slabs(blocks)):
            out_ref[s] = slab.astype(BF16)
    dd_ref[...] = _lane_concat_rows(d_ref[...])
    gg_ref[...] = jnp.concatenate([g_ref[...]] * HEADS_PER_SLAB, axis=1)
    s1 = jnp.sum(lq1_ref[...] * lk1_ref[...], axis=-1, keepdims=True)
    s2 = jnp.sum(lq2_ref[...] * lk2_ref[...], axis=-1, keepdims=True)
    lam_ref[...] = jnp.exp(s1) - jnp.exp(s2) + lam_init


def _prep(l, a_re, a_im, log_dt, b_re, b_im, c_re, c_im, d, subln_g, lq1, lk1, lq2, lk2, lam_init):
    g, p, c = N_SSM_GROUPS, SSM_STATE, SSM_GROUP
    layer = lambda *shape: pl.BlockSpec((None,) + shape, lambda: (l,) + (0,) * len(shape))
    row = lambda a: a.reshape(a.shape[0], 1, a.shape[1])
    slab = jax.ShapeDtypeStruct((N_SLABS, LANES, SLAB_STATES), BF16)
    out_shape = (jax.ShapeDtypeStruct((1, SSM_LANES), F32), jax.ShapeDtypeStruct((1, SSM_LANES), F32),
                 slab, slab, slab, slab,
                 jax.ShapeDtypeStruct((1, SSM_WIDTH), F32), jax.ShapeDtypeStruct((1, LANES), F32),
                 jax.ShapeDtypeStruct((1, 1), F32))
    return pl.pallas_call(
        functools.partial(_prep_kernel, lam_init=lam_init),
        in_specs=[layer(g, p), layer(g, p), layer(1, g), layer(g, c, p), layer(g, c, p), layer(g, c, p),
                  layer(g, c, p), layer(g, c), layer(1, HEAD_DIM)] + [layer(1, HALF_DIM)] * 4,
        out_shape=out_shape, name="prep",
    )(a_re, a_im, row(log_dt), jnp.swapaxes(b_re, 2, 3), jnp.swapaxes(b_im, 2, 3), c_re, c_im, d,
      row(subln_g), row(lq1), row(lk1), row(lq2), row(lk2))


def _rope_tables(positions):
    inv = ROPE_THETA ** (-np.arange(0, ROT_DIM, 2, dtype=np.float64) / ROT_DIM)
    ang = np.asarray(positions, np.float64)[:, None] * inv[None, :]
    r = np.arange(LANES) % HALF_DIM
    half = ROT_DIM // 2
    idx = r % half
    cos = np.where(r[None, :] < ROT_DIM, np.cos(ang)[:, idx], 1.0)
    sin = np.sin(ang)[:, idx]
    s_up = np.where(r[None, :] < half, -sin, 0.0)
    s_dn = np.where((r[None, :] >= half) & (r[None, :] < ROT_DIM), sin, 0.0)
    return (jnp.asarray(cos, F32), jnp.asarray(s_up, F32), jnp.asarray(s_dn, F32))


def _inproj_kernel(x_ref, w_ref, cos_ref, sup_ref, sdn_ref, q_ref, k_ref, v_ref, u_ref, *, nb, tl):
    half = ROT_DIM // 2
    x = x_ref[...].reshape(nb * tl, D_MODEL).astype(BF16)
    cos = jnp.concatenate([cos_ref[...]] * nb, axis=0)
    sup = jnp.concatenate([sup_ref[...]] * nb, axis=0)
    sdn = jnp.concatenate([sdn_ref[...]] * nb, axis=0)

    def rope(t):
        slabs = []
        for j in range(ATTN_WIDTH // LANES):
            s = t[:, j * LANES:(j + 1) * LANES]
            slabs.append(s * cos + pltpu.roll(s, LANES - half, 1) * sup + pltpu.roll(s, half, 1) * sdn)
        return jnp.concatenate(slabs, axis=1)

    q = jnp.dot(x, w_ref[:, 0:ATTN_WIDTH], preferred_element_type=F32)
    q_ref[...] = (rope(q) * (QK_SCALE * LOG2_E)).astype(BF16).reshape(nb, tl, ATTN_WIDTH)
    k = jnp.dot(x, w_ref[:, ATTN_WIDTH:2 * ATTN_WIDTH], preferred_element_type=F32)
    k_ref[...] = rope(k).reshape(nb, tl, ATTN_WIDTH)
    v = jnp.dot(x, w_ref[:, 2 * ATTN_WIDTH:3 * ATTN_WIDTH], preferred_element_type=F32)
    v_ref[...] = v.reshape(nb, tl, ATTN_WIDTH)
    u = jnp.dot(x, w_ref[:, 3 * ATTN_WIDTH:], preferred_element_type=F32)
    first_slot = (pl.program_id(1) * nb) % SUBLANES
    for b in range(nb):
        for s in range(N_SLABS):
            u_ref[s, pl.ds(first_slot + b, tl, stride=SUBLANES), :] = u[b * tl:(b + 1) * tl,
                                                                        s * LANES:(s + 1) * LANES]


def _inproj(x, w_bf, tables, nb, tl):
    bsz, seq, _ = x.shape
    nblk = bsz // SUBLANES
    per_blk = SUBLANES // nb
    grid = (seq // tl, bsz // nb)
    tab_spec = pl.BlockSpec((tl, LANES), lambda i, b: (i, 0))
    act_spec = lambda width: pl.BlockSpec((nb, tl, width), lambda i, b: (b, i, 0))
    out_shape = (jax.ShapeDtypeStruct((bsz, seq, ATTN_WIDTH), BF16),
                 jax.ShapeDtypeStruct((bsz, seq, ATTN_WIDTH), F32),
                 jax.ShapeDtypeStruct((bsz, seq, ATTN_WIDTH), F32),
                 jax.ShapeDtypeStruct((nblk, N_SLABS, seq * SUBLANES, LANES), F32))
    return pl.pallas_call(
        functools.partial(_inproj_kernel, nb=nb, tl=tl),
        grid=grid,
        in_specs=[act_spec(D_MODEL),
                  pl.BlockSpec((D_MODEL, 4 * ATTN_WIDTH), lambda i, b: (0, 0)),
                  tab_spec, tab_spec, tab_spec],
        out_specs=(act_spec(ATTN_WIDTH), act_spec(ATTN_WIDTH), act_spec(ATTN_WIDTH),
                   pl.BlockSpec((None, N_SLABS, tl * SUBLANES, LANES), lambda i, b: (b // per_blk, 0, i, 0))),
        out_shape=out_shape,
        compiler_params=_params(("parallel", "arbitrary")),
        name="inproj",
    )(x, w_bf, *tables)


def _expand_maps(q):
    lane = lax.broadcasted_iota(jnp.int32, q.shape, 1)
    zero = jnp.zeros_like(q)
    return jnp.concatenate(
        [jnp.where((lane >= m * HALF_DIM) & (lane < (m + 1) * HALF_DIM), q, zero) for m in range(MAPS_PER_SLAB)],
        axis=0)


def _diff_combine(o, lam, g, n, lam_init):
    lane = lax.broadcasted_iota(jnp.int32, (n, LANES), 1)
    first = lane < HEAD_DIM
    d = jnp.where(first, o[0:n] - lam * o[n:2 * n], o[2 * n:3 * n] - lam * o[3 * n:4 * n])
    sq = d * d
    ss_a = jnp.sum(jnp.where(first, sq, 0.0), axis=-1, keepdims=True)
    ss_b = jnp.sum(jnp.where(first, 0.0, sq), axis=-1, keepdims=True)
    ms = jnp.where(first, ss_a, ss_b) * (1.0 / HEAD_DIM)
    return d * lax.rsqrt(ms + SUBLN_EPS) * g * (1.0 - lam_init)


def _attn_prompt_kernel(lam_ref, q_ref, k_ref, v_ref, g_ref, o_ref, kb, vb, qx_a, qx_b, s_a, s_b, m_ref, acc_ref,
                        *, blk, lam_init):
    seq = q_ref.shape[0]
    hb = blk // 2
    half = MAPS_PER_SLAB * hb
    rows = 2 * half
    dims = (((1,), (1,)), ((), ()))

    kb[...] = k_ref[...].astype(BF16)
    vb[:, 0:LANES] = v_ref[...].astype(BF16)
    vb[:, LANES:2 * LANES] = jnp.ones(v_ref.shape, BF16)

    steps = [(i, j) for i in range(seq // blk) for j in range(i + 1)]
    qx_bufs = (qx_a, qx_b)
    s_bufs = (s_a, s_b)

    def expand_queries(i):
        qx = qx_bufs[i % 2]
        qx[0:half, :] = _expand_maps(q_ref[i * blk:i * blk + hb, :])
        qx[half:rows, :] = _expand_maps(q_ref[i * blk + hb:(i + 1) * blk, :])

    def scores(n):
        i, j = steps[n]
        s_bufs[n % 2][...] = lax.dot_general(qx_bufs[i % 2][...], kb[j * blk:(j + 1) * blk, :], dims,
                                             preferred_element_type=F32)

    def update(j, src, h, width, row_offset):
        rs = slice(h * half, (h + 1) * half)
        s = src[rs, 0:width]
        if row_offset is not None:
            r = lax.broadcasted_iota(jnp.int32, (half, width), 0)
            c = lax.broadcasted_iota(jnp.int32, (half, width), 1)
            s = jnp.where((row_offset + r % hb) // CHUNK >= c // CHUNK, s, -jnp.inf)
        m_prev = m_ref[rs, :]
        m_new = jnp.maximum(m_prev, jnp.max(s, axis=-1, keepdims=True))
        alpha = jnp.exp2(m_prev - m_new)
        p = jnp.exp2(s - jnp.concatenate([m_new] * (width // LANES), axis=1)).astype(BF16)
        pv = jnp.dot(p, vb[j * blk:j * blk + width, :], preferred_element_type=F32)
        acc_ref[rs, :] = jnp.concatenate([alpha, alpha], axis=1) * acc_ref[rs, :] + pv
        m_ref[rs, :] = m_new

    expand_queries(0)
    scores(0)
    for n, (i, j) in enumerate(steps):
        if n + 1 < len(steps):
            if steps[n + 1][1] == 0:
                expand_queries(steps[n + 1][0])
            scores(n + 1)
        if j == 0:
            m_ref[...] = jnp.full((rows, LANES), -jnp.inf, F32)
            acc_ref[...] = jnp.zeros((rows, 2 * LANES), F32)
        src = s_bufs[n % 2]
        if j < i:
            update(j, src, 0, blk, None)
            update(j, src, 1, blk, None)
        else:
            update(j, src, 0, hb, 0)
            update(j, src, 1, blk, hb)
            for h in range(2):
                acc = acc_ref[h * half:(h + 1) * half, :]
                o = acc[:, 0:LANES] / acc[:, LANES:2 * LANES]
                o_ref[i * blk + h * hb:i * blk + (h + 1) * hb, :] = _diff_combine(
                    o, lam_ref[0, 0], g_ref[...], hb, lam_init).astype(BF16)


def _attn_prompt(lam, q, k, v, g128, blk, lam_init):
    bsz, seq, _ = q.shape
    rows = MAPS_PER_SLAB * blk
    slab_spec = pl.BlockSpec((None, seq, LANES), lambda b, j: (b, 0, j))
    return pl.pallas_call(
        functools.partial(_attn_prompt_kernel, blk=blk, lam_init=lam_init),
        grid=(bsz, ATTN_WIDTH // LANES),
        in_specs=[pl.BlockSpec(memory_space=pltpu.SMEM), slab_spec, slab_spec, slab_spec,
                  pl.BlockSpec((1, LANES), lambda b, j: (0, 0))],
        out_specs=slab_spec,
        out_shape=jax.ShapeDtypeStruct((bsz, seq, ATTN_WIDTH), BF16),
        scratch_shapes=[pltpu.VMEM((seq, LANES), BF16), pltpu.VMEM((seq, 2 * LANES), BF16),
                        pltpu.VMEM((rows, LANES), BF16), pltpu.VMEM((rows, LANES), BF16),
                        pltpu.VMEM((rows, blk), F32), pltpu.VMEM((rows, blk), F32),
                        pltpu.VMEM((rows, LANES), F32), pltpu.VMEM((rows, 2 * LANES), F32)],
        compiler_params=_params(("parallel", "parallel")),
        name="attn_prompt",
    )(lam, q, k, v, g128)


def _attn_sample_kernel(lam_ref, q_ref, kc_ref, vc_ref, kn_ref, vn_ref, g_ref, o_ref, *, n, lam_init):
    dims = (((1,), (1,)), ((), ()))
    for j in range(ATTN_WIDTH // LANES):
        cols = slice(j * LANES, (j + 1) * LANES)
        qx = _expand_maps(q_ref[:, cols])
        s_c = jnp.dot(qx, kc_ref[cols, :].astype(BF16), preferred_element_type=F32)
        s_n = lax.dot_general(qx, kn_ref[:, cols].astype(BF16), dims, preferred_element_type=F32)
        m = jnp.maximum(jnp.max(s_c, axis=-1, keepdims=True), jnp.max(s_n, axis=-1, keepdims=True))
        p_c = jnp.exp2(s_c - m)
        p_n = jnp.exp2(s_n - m)
        l = jnp.sum(p_c, axis=-1, keepdims=True) + jnp.sum(p_n, axis=-1, keepdims=True)
        acc = (lax.dot_general(p_c.astype(BF16), vc_ref[cols, :].astype(BF16), dims, preferred_element_type=F32)
               + jnp.dot(p_n.astype(BF16), vn_ref[:, cols].astype(BF16), preferred_element_type=F32))
        o_ref[:, cols] = _diff_combine(acc / l, lam_ref[0, 0], g_ref[...], n, lam_init).astype(BF16)


def _attn_sample(lam, q, cache_kt, cache_vt, k_new, v_new, g128, lam_init):
    bsz, n, _ = q.shape
    past = cache_kt.shape[2]
    assert past % CHUNK == 0 and n <= CHUNK
    cache_spec = pl.BlockSpec((None, ATTN_WIDTH, past), lambda b: (b, 0, 0))
    new_spec = pl.BlockSpec((None, n, ATTN_WIDTH), lambda b: (b, 0, 0))
    return pl.pallas_call(
        functools.partial(_attn_sample_kernel, n=n, lam_init=lam_init),
        grid=(bsz,),
        in_specs=[pl.BlockSpec(memory_space=pltpu.SMEM), new_spec, cache_spec, cache_spec, new_spec, new_spec,
                  pl.BlockSpec((1, LANES), lambda b: (0, 0))],
        out_specs=new_spec,
        out_shape=jax.ShapeDtypeStruct((bsz, n, ATTN_WIDTH), BF16),
        compiler_params=_params(("parallel",)),
        name="attn_sample",
    )(lam, q, cache_kt, cache_vt, k_new, v_new, g128)


def _ssm_kernel(u_ref, h0re_ref, h0im_ref, are_ref, aim_ref, bre_ref, bim_ref, cre_ref, cim_ref, d_ref,
                wglu_ref, bglu_ref, out_ref, hre_out, him_out, bure, buim, hsre, hsim, hre, him, res,
                *, tt):
    i = pl.program_id(1)
    dims = (((1,), (1,)), ((), ()))

    @pl.when(i == 0)
    def _():
        for b in range(SUBLANES):
            hre[b:b + 1, :] = _lane_concat_rows(h0re_ref[b])
            him[b:b + 1, :] = _lane_concat_rows(h0im_ref[b])

    ys = []
    for s in range(N_SLABS):
        us = u_ref[s].astype(BF16)
        cols = slice(s * SLAB_STATES, (s + 1) * SLAB_STATES)
        bure[:, cols] = jnp.dot(us, bre_ref[s], preferred_element_type=F32)
        buim[:, cols] = jnp.dot(us, bim_ref[s], preferred_element_type=F32)
        ar = jnp.broadcast_to(are_ref[:, cols], (SUBLANES, SLAB_STATES))
        ai = jnp.broadcast_to(aim_ref[:, cols], (SUBLANES, SLAB_STATES))
        hr = hre[:, cols]
        hi = him[:, cols]
        for t in range(tt):
            r = slice(t * SUBLANES, (t + 1) * SUBLANES)
            hr, hi = ar * hr - ai * hi + bure[r, cols], ar * hi + ai * hr + buim[r, cols]
            hsre[r, cols] = hr
            hsim[r, cols] = hi
        hre[:, cols] = hr
        him[:, cols] = hi
        ys.append(lax.dot_general(hsre[:, cols].astype(BF16), cre_ref[s], dims, preferred_element_type=F32)
                  - lax.dot_general(hsim[:, cols].astype(BF16), cim_ref[s], dims, preferred_element_type=F32)
                  + d_ref[:, s * LANES:(s + 1) * LANES] * u_ref[s])
    y = jnp.concatenate(ys, axis=1)
    z = jnp.dot(jax.nn.gelu(y).astype(BF16), wglu_ref[...], preferred_element_type=F32) + bglu_ref[...]
    gated = z[:, :SSM_WIDTH] * jax.nn.sigmoid(z[:, SSM_WIDTH:])
    for s in range(N_SLABS):
        res[s] = gated[:, s * LANES:(s + 1) * LANES]
    for b in range(SUBLANES):
        for s in range(N_SLABS):
            out_ref[b, :, s * LANES:(s + 1) * LANES] = res[s, pl.ds(b, tt, stride=SUBLANES), :].astype(BF16)

    @pl.when(i == pl.num_programs(1) - 1)
    def _():
        for b in range(SUBLANES):
            for g in range(N_SSM_GROUPS):
                lanes = slice(g * SSM_STATE, (g + 1) * SSM_STATE)
                hre_out[b, g:g + 1, :] = hre[b:b + 1, lanes]
                him_out[b, g:g + 1, :] = him[b:b + 1, lanes]


def _ssm(l, u_t, h0_re, h0_im, lb_re, lb_im, bre, bim, cre, cim, d, w_glu, b_glu, tt):
    nblk, _, rows_total, _ = u_t.shape
    seq = rows_total // SUBLANES
    rows = SUBLANES * tt
    grid = (nblk, seq // tt)
    const = lambda shape: pl.BlockSpec(shape, lambda b, i: (0,) * len(shape))
    layer = lambda *shape: pl.BlockSpec((None,) + shape, lambda b, i: (l,) + (0,) * len(shape))
    slab_spec = const((N_SLABS, LANES, SLAB_STATES))
    state_spec = pl.BlockSpec((SUBLANES, N_SSM_GROUPS, SSM_STATE), lambda b, i: (b, 0, 0))
    state_shape = jax.ShapeDtypeStruct((nblk * SUBLANES, N_SSM_GROUPS, SSM_STATE), F32)
    return pl.pallas_call(
        functools.partial(_ssm_kernel, tt=tt),
        grid=grid,
        in_specs=[pl.BlockSpec((None, N_SLABS, rows, LANES), lambda b, i: (b, 0, i, 0)),
                  state_spec, state_spec,
                  const((1, SSM_LANES)), const((1, SSM_LANES)),
                  slab_spec, slab_spec, slab_spec, slab_spec,
                  const((1, SSM_WIDTH)), const((SSM_WIDTH, 2 * SSM_WIDTH)), layer(1, 2 * SSM_WIDTH)],
        out_specs=(pl.BlockSpec((SUBLANES, tt, SSM_WIDTH), lambda b, i: (b, i, 0)), state_spec, state_spec),
        out_shape=(jax.ShapeDtypeStruct((nblk * SUBLANES, seq, SSM_WIDTH), BF16), state_shape, state_shape),
        scratch_shapes=[pltpu.VMEM((rows, SSM_LANES), F32) for _ in range(4)]
                       + [pltpu.VMEM((SUBLANES, SSM_LANES), F32) for _ in range(2)]
                       + [pltpu.VMEM((N_SLABS, rows, LANES), F32)],
        compiler_params=_params(("parallel", "arbitrary")),
        name="ssm",
    )(u_t, h0_re, h0_im, lb_re, lb_im, bre, bim, cre, cim, d, w_glu, b_glu)


def _layer_norm(x, g, b):
    mu = jnp.mean(x, axis=-1, keepdims=True)
    xc = x - mu
    var = jnp.mean(xc * xc, axis=-1, keepdims=True)
    return xc * lax.rsqrt(var + LN_EPS) * g + b


def _post_kernel(x_ref, a_ref, s_ref, wout_ref, g1_ref, b1_ref, wup_ref, wdown_ref, g2_ref, b2_ref, o_ref,
                 *, nb, tl, ff_chunk):
    chains = (0, 1)
    n = nb * tl // len(chains)
    if nb == 1:
        views = [(slice(None), slice(c * n, (c + 1) * n)) for c in chains]
    else:
        views = [(slice(c * nb // len(chains), (c + 1) * nb // len(chains)), slice(None)) for c in chains]
    xs = [x_ref[v].reshape(n, D_MODEL) for v in views]
    mix = [jnp.dot(jnp.concatenate([a_ref[v].reshape(n, ATTN_WIDTH), s_ref[v].reshape(n, SSM_WIDTH)], axis=1),
                   wout_ref[...], preferred_element_type=F32) for v in views]
    x1 = [None, None]
    x1b = [None, None]
    h = [None, None]
    ff = [jnp.zeros((n, D_MODEL), F32) for _ in chains]
    n_chunks = D_FF // ff_chunk

    def up(c, k):
        return jnp.dot(x1b[c], wup_ref[:, k * ff_chunk:(k + 1) * ff_chunk], preferred_element_type=F32)

    def down(c, k):
        act = jnp.square(jnp.maximum(h[c], 0.0)).astype(BF16)
        return ff[c] + jnp.dot(act, wdown_ref[k * ff_chunk:(k + 1) * ff_chunk, :], preferred_element_type=F32)

    x1[0] = _layer_norm(DEEPNORM_ALPHA * xs[0] + mix[0], g1_ref[...], b1_ref[...])
    x1b[0] = x1[0].astype(BF16)
    h[0] = up(0, 0)
    x1[1] = _layer_norm(DEEPNORM_ALPHA * xs[1] + mix[1], g1_ref[...], b1_ref[...])
    x1b[1] = x1[1].astype(BF16)
    for k in range(n_chunks):
        h[1] = up(1, k)
        ff[0] = down(0, k)
        if k + 1 < n_chunks:
            h[0] = up(0, k + 1)
        ff[1] = down(1, k)
    for c in chains:
        x2 = _layer_norm(DEEPNORM_ALPHA * x1[c] + ff[c], g2_ref[...], b2_ref[...])
        o_ref[views[c]] = x2.reshape(x_ref[views[c]].shape)


def _post(l, x, attn, ssm, w_out, g1, b1, w_up, w_down, g2, b2, nb, tl):
    bsz, seq, _ = x.shape
    grid = (bsz // nb, seq // tl)
    act_spec = lambda width: pl.BlockSpec((nb, tl, width), lambda b, i: (b, i, 0))
    const = lambda shape: pl.BlockSpec(shape, lambda b, i: (0,) * len(shape), pipeline_mode=pl.Buffered(1))
    layer = lambda *shape: pl.BlockSpec((None,) + shape, lambda b, i: (l,) + (0,) * len(shape))
    return pl.pallas_call(
        functools.partial(_post_kernel, nb=nb, tl=tl, ff_chunk=1024),
        grid=grid,
        in_specs=[act_spec(D_MODEL), act_spec(ATTN_WIDTH), act_spec(SSM_WIDTH),
                  const((D_MODEL, D_MODEL)), layer(1, D_MODEL), layer(1, D_MODEL),
                  const((D_MODEL, D_FF)), const((D_FF, D_MODEL)), layer(1, D_MODEL), layer(1, D_MODEL)],
        out_specs=act_spec(D_MODEL),
        out_shape=jax.ShapeDtypeStruct((bsz, seq, D_MODEL), F32),
        compiler_params=_params(("parallel", "parallel")),
        name="post",
    )(x, attn, ssm, w_out, g1, b1, w_up, w_down, g2, b2)


def kernel(x_prompt, x_sample, cache_k, cache_v, state_ssm_re, state_ssm_im, w_in, lambda_q1, lambda_k1,
           lambda_q2, lambda_k2, subln_g, ssm_a_re, ssm_a_im, ssm_log_dt, ssm_b_re, ssm_b_im, ssm_c_re,
           ssm_c_im, ssm_d, w_glu, b_glu, w_out, ln1_g, ln1_b, w_up, w_down, ln2_g, ln2_b):
    assert w_in.shape[0] == DEPTH
    l = 0
    lam_init = _lambda_init(l)
    bp, seq, _ = x_prompt.shape
    bs, dec_seq, _ = x_sample.shape
    past = cache_k.shape[2]

    lb_re, lb_im, bre, bim, cre, cim, d, g128, lam = _prep(
        l, ssm_a_re, ssm_a_im, ssm_log_dt, ssm_b_re, ssm_b_im, ssm_c_re, ssm_c_im, ssm_d, subln_g,
        lambda_q1, lambda_k1, lambda_q2, lambda_k2, lam_init)
    w_in_bf, w_glu_bf, w_out_bf, w_up_bf, w_down_bf = (w[l].astype(BF16) for w in (w_in, w_glu, w_out, w_up, w_down))
    row = lambda a: a.reshape(a.shape[0], 1, a.shape[1])
    b_glu3 = row(b_glu)
    ln = [row(a) for a in (ln1_g, ln1_b, ln2_g, ln2_b)]

    def layer(x, positions, cache, h0_re, h0_im, nb, tl, tt):
        bsz, n, _ = x.shape
        q, k, v, u_t = _inproj(x, w_in_bf, _rope_tables(positions), nb, tl)
        if cache is None:
            attn = _attn_prompt(lam, q, k, v, g128, 512, lam_init)
        else:
            attn = _attn_sample(lam, q, cache[0], cache[1], k, v, g128, lam_init)
        ssm, h_re, h_im = _ssm(l, u_t, h0_re, h0_im, lb_re, lb_im, bre, bim, cre, cim, d, w_glu_bf, b_glu3, tt)
        y = _post(l, x, attn, ssm, w_out_bf, ln[0], ln[1], w_up_bf, w_down_bf, ln[2], ln[3], nb, tl)
        shape_kv = (1, bsz, n, N_HEADS, HEAD_DIM)
        return y, k.reshape(shape_kv), v.reshape(shape_kv), h_re[None], h_im[None]

    cache = tuple(jnp.transpose(c[l], (0, 2, 3, 1)).reshape(bs, ATTN_WIDTH, past) for c in (cache_k, cache_v))
    ys, ks, vs, rs, is_ = layer(x_sample, past + np.arange(dec_seq), cache, state_ssm_re[l], state_ssm_im[l],
                                nb=SUBLANES, tl=dec_seq, tt=dec_seq)
    zeros = jnp.zeros((bp, N_SSM_GROUPS, SSM_STATE), F32)
    yp, kp, vp, rp, ip = layer(x_prompt, np.arange(seq), None, zeros, zeros, nb=1, tl=512, tt=64)
    return (yp, ys, kp, vp, rp, ip, ks, vs, rs, is_)
```

```python
import functools
import math

import numpy as np
import jax
import jax.numpy as jnp
from jax import lax
from jax.experimental import pallas as pl
from jax.experimental.pallas import tpu as pltpu

D_MODEL = 1024
DEPTH = 1
CHUNK = 64
ATTN_WIDTH = 512
SSM_WIDTH = 512
N_HEADS = 8
HEAD_DIM = 64
HALF_DIM = 32
ROT_DIM = 8
ROPE_THETA = 500000.0
SSM_GROUP = 16
N_SSM_GROUPS = 32
SSM_STATE = 64
D_FF = 4 * D_MODEL
LN_EPS = 1e-5
SUBLN_EPS = 1e-5
DEEPNORM_ALPHA = (2 * DEPTH) ** 0.25
QK_SCALE = HALF_DIM ** -0.5
LOG2_E = math.log2(math.e)

SUBLANES = 8
LANES = 128
SSM_LANES = N_SSM_GROUPS * SSM_STATE
GROUPS_PER_SLAB = LANES // SSM_GROUP
N_SLABS = SSM_WIDTH // LANES
SLAB_STATES = GROUPS_PER_SLAB * SSM_STATE
HEADS_PER_SLAB = LANES // HEAD_DIM
MAPS_PER_SLAB = LANES // HALF_DIM
VMEM_LIMIT_BYTES = 56 * 1024 * 1024

F32 = jnp.float32
BF16 = jnp.bfloat16


def _lambda_init(layer_idx):
    return 0.8 - 0.6 * math.exp(-0.3 * layer_idx)


def _params(semantics):
    return pltpu.CompilerParams(dimension_semantics=semantics, vmem_limit_bytes=VMEM_LIMIT_BYTES)


def _lane_concat_rows(x):
    return jnp.concatenate([x[r:r + 1, :] for r in range(x.shape[0])], axis=1)


def _block_diag_slabs(blocks):
    slabs = []
    for s in range(N_SLABS):
        rows = []
        for gi in range(GROUPS_PER_SLAB):
            pieces = []
            if gi > 0:
                pieces.append(jnp.zeros((SSM_GROUP, gi * SSM_STATE), F32))
            pieces.append(blocks[s * GROUPS_PER_SLAB + gi])
            if gi < GROUPS_PER_SLAB - 1:
                pieces.append(jnp.zeros((SSM_GROUP, (GROUPS_PER_SLAB - 1 - gi) * SSM_STATE), F32))
            rows.append(jnp.concatenate(pieces, axis=1))
        slabs.append(jnp.concatenate(rows, axis=0))
    return slabs


def _prep_kernel(are_ref, aim_ref, logdt_ref, bre_ref, bim_ref, cre_ref, cim_ref, d_ref, g_ref,
                 lq1_ref, lk1_ref, lq2_ref, lk2_ref,
                 lbre_ref, lbim_ref, bbre_ref, bbim_ref, ccre_ref, ccim_ref, dd_ref, gg_ref, lam_ref,
                 *, lam_init):
    g = N_SSM_GROUPS
    eye = lax.broadcasted_iota(jnp.int32, (g, g), 0) == lax.broadcasted_iota(jnp.int32, (g, g), 1)
    logdt = jnp.sum(jnp.where(eye, jnp.broadcast_to(logdt_ref[...], (g, g)), 0.0), axis=-1, keepdims=True)
    dt = jnp.exp(logdt)
    ar = are_ref[...]
    ai = aim_ref[...]
    mag = jnp.exp(ar * dt)
    lb_re = mag * jnp.cos(ai * dt)
    lb_im = mag * jnp.sin(ai * dt)
    nr = lb_re - 1.0
    ni = lb_im
    den = ar * ar + ai * ai
    f_re = (nr * ar + ni * ai) / den
    f_im = (ni * ar - nr * ai) / den
    lbre_ref[...] = _lane_concat_rows(lb_re)
    lbim_ref[...] = _lane_concat_rows(lb_im)
    br = bre_ref[...]
    bi = bim_ref[...]
    bb_re = f_re[:, None, :] * br - f_im[:, None, :] * bi
    bb_im = f_re[:, None, :] * bi + f_im[:, None, :] * br
    for out_ref, blocks in ((bbre_ref, bb_re), (bbim_ref, bb_im), (ccre_ref, cre_ref[...]), (ccim_ref, cim_ref[...])):
        for s, slab in enumerate(_block_diag_slabs(blocks)):
            out_ref[s] = slab.astype(BF16)
    dd_ref[...] = _lane_concat_rows(d_ref[...])
    gg_ref[...] = jnp.concatenate([g_ref[...]] * HEADS_PER_SLAB, axis=1)
    s1 = jnp.sum(lq1_ref[...] * lk1_ref[...], axis=-1, keepdims=True)
    s2 = jnp.sum(lq2_ref[...] * lk2_ref[...], axis=-1, keepdims=True)
    lam_ref[...] = jnp.exp(s1) - jnp.exp(s2) + lam_init


def _prep(l, a_re, a_im, log_dt, b_re, b_im, c_re, c_im, d, subln_g, lq1, lk1, lq2, lk2, lam_init):
    g, p, c = N_SSM_GROUPS, SSM_STATE, SSM_GROUP
    layer = lambda *shape: pl.BlockSpec((None,) + shape, lambda: (l,) + (0,) * len(shape))
    row = lambda a: a.reshape(a.shape[0], 1, a.shape[1])
    slab = jax.ShapeDtypeStruct((N_SLABS, LANES, SLAB_STATES), BF16)
    out_shape = (jax.ShapeDtypeStruct((1, SSM_LANES), F32), jax.ShapeDtypeStruct((1, SSM_LANES), F32),
                 slab, slab, slab, slab,
                 jax.ShapeDtypeStruct((1, SSM_WIDTH), F32), jax.ShapeDtypeStruct((1, LANES), F32),
                 jax.ShapeDtypeStruct((1, 1), F32))
    return pl.pallas_call(
        functools.partial(_prep_kernel, lam_init=lam_init),
        in_specs=[layer(g, p), layer(g, p), layer(1, g), layer(g, c, p), layer(g, c, p), layer(g, c, p),
                  layer(g, c, p), layer(g, c), layer(1, HEAD_DIM)] + [layer(1, HALF_DIM)] * 4,
        out_shape=out_shape, name="prep",
    )(a_re, a_im, row(log_dt), jnp.swapaxes(b_re, 2, 3), jnp.swapaxes(b_im, 2, 3), c_re, c_im, d,
      row(subln_g), row(lq1), row(lk1), row(lq2), row(lk2))


def _rope_tables(positions):
    inv = ROPE_THETA ** (-np.arange(0, ROT_DIM, 2, dtype=np.float64) / ROT_DIM)
    ang = np.asarray(positions, np.float64)[:, None] * inv[None, :]
    r = np.arange(LANES) % HALF_DIM
    half = ROT_DIM // 2
    idx = r % half
    cos = np.where(r[None, :] < ROT_DIM, np.cos(ang)[:, idx], 1.0)
    sin = np.sin(ang)[:, idx]
    s_up = np.where(r[None, :] < half, -sin, 0.0)
    s_dn = np.where((r[None, :] >= half) & (r[None, :] < ROT_DIM), sin, 0.0)
    return (jnp.asarray(cos, F32), jnp.asarray(s_up, F32), jnp.asarray(s_dn, F32))


def _inproj_kernel(after_ref, x_ref, w_ref, cos_ref, sup_ref, sdn_ref, q_ref, k_ref, v_ref, u_ref, *, nb, tl):
    del after_ref
    half = ROT_DIM // 2
    x = x_ref[...].reshape(nb * tl, D_MODEL).astype(BF16)
    cos = jnp.concatenate([cos_ref[...]] * nb, axis=0)
    sup = jnp.concatenate([sup_ref[...]] * nb, axis=0)
    sdn = jnp.concatenate([sdn_ref[...]] * nb, axis=0)

    def rope(t):
        slabs = []
        for j in range(ATTN_WIDTH // LANES):
            s = t[:, j * LANES:(j + 1) * LANES]
            slabs.append(s * cos + pltpu.roll(s, LANES - half, 1) * sup + pltpu.roll(s, half, 1) * sdn)
        return jnp.concatenate(slabs, axis=1)

    q = jnp.dot(x, w_ref[:, 0:ATTN_WIDTH], preferred_element_type=F32)
    q_ref[...] = (rope(q) * (QK_SCALE * LOG2_E)).astype(BF16).reshape(nb, tl, ATTN_WIDTH)
    k = jnp.dot(x, w_ref[:, ATTN_WIDTH:2 * ATTN_WIDTH], preferred_element_type=F32)
    k_ref[...] = rope(k).reshape(nb, tl, ATTN_WIDTH)
    v = jnp.dot(x, w_ref[:, 2 * ATTN_WIDTH:3 * ATTN_WIDTH], preferred_element_type=F32)
    v_ref[...] = v.reshape(nb, tl, ATTN_WIDTH)
    u = jnp.dot(x, w_ref[:, 3 * ATTN_WIDTH:], preferred_element_type=F32)
    first_slot = (pl.program_id(1) * nb) % SUBLANES
    for b in range(nb):
        for s in range(N_SLABS):
            u_ref[s, pl.ds(first_slot + b, tl, stride=SUBLANES), :] = u[b * tl:(b + 1) * tl,
                                                                        s * LANES:(s + 1) * LANES]


def _inproj(after, x, w_bf, tables, nb, tl):
    bsz, seq, _ = x.shape
    nblk = bsz // SUBLANES
    per_blk = SUBLANES // nb
    grid = (seq // tl, bsz // nb)
    tab_spec = pl.BlockSpec((tl, LANES), lambda i, b: (i, 0))
    act_spec = lambda width: pl.BlockSpec((nb, tl, width), lambda i, b: (b, i, 0))
    out_shape = (jax.ShapeDtypeStruct((bsz, seq, ATTN_WIDTH), BF16),
                 jax.ShapeDtypeStruct((bsz, seq, ATTN_WIDTH), F32),
                 jax.ShapeDtypeStruct((bsz, seq, ATTN_WIDTH), F32),
                 jax.ShapeDtypeStruct((nblk, N_SLABS, seq * SUBLANES, LANES), F32))
    return pl.pallas_call(
        functools.partial(_inproj_kernel, nb=nb, tl=tl),
        grid=grid,
        in_specs=[pl.BlockSpec(memory_space=pltpu.SMEM), act_spec(D_MODEL),
                  pl.BlockSpec((D_MODEL, 4 * ATTN_WIDTH), lambda i, b: (0, 0)),
                  tab_spec, tab_spec, tab_spec],
        out_specs=(act_spec(ATTN_WIDTH), act_spec(ATTN_WIDTH), act_spec(ATTN_WIDTH),
                   pl.BlockSpec((None, N_SLABS, tl * SUBLANES, LANES), lambda i, b: (b // per_blk, 0, i, 0))),
        out_shape=out_shape,
        compiler_params=_params(("parallel", "arbitrary")),
        name="inproj",
    )(after, x, w_bf, *tables)


def _expand_maps(q):
    lane = lax.broadcasted_iota(jnp.int32, q.shape, 1)
    zero = jnp.zeros_like(q)
    return jnp.concatenate(
        [jnp.where((lane >= m * HALF_DIM) & (lane < (m + 1) * HALF_DIM), q, zero) for m in range(MAPS_PER_SLAB)],
        axis=0)


def _diff_combine(o, lam, g, n, lam_init):
    lane = lax.broadcasted_iota(jnp.int32, (n, LANES), 1)
    first = lane < HEAD_DIM
    d = jnp.where(first, o[0:n] - lam * o[n:2 * n], o[2 * n:3 * n] - lam * o[3 * n:4 * n])
    sq = d * d
    ss_a = jnp.sum(jnp.where(first, sq, 0.0), axis=-1, keepdims=True)
    ss_b = jnp.sum(jnp.where(first, 0.0, sq), axis=-1, keepdims=True)
    ms = jnp.where(first, ss_a, ss_b) * (1.0 / HEAD_DIM)
    return d * lax.rsqrt(ms + SUBLN_EPS) * g * (1.0 - lam_init)


def _attn_prompt_kernel(lam_ref, q_ref, k_ref, v_ref, g_ref, o_ref, kb, vb, qx_a, qx_b, s_a, s_b, m_ref, acc_ref,
                        *, blk, lam_init):
    seq = q_ref.shape[0]
    hb = blk // 2
    half = MAPS_PER_SLAB * hb
    rows = 2 * half
    dims = (((1,), (1,)), ((), ()))

    kb[...] = k_ref[...].astype(BF16)
    vb[:, 0:LANES] = v_ref[...].astype(BF16)
    vb[:, LANES:2 * LANES] = jnp.ones(v_ref.shape, BF16)

    steps = [(i, j) for i in range(seq // blk) for j in range(i + 1)]
    qx_bufs = (qx_a, qx_b)
    s_bufs = (s_a, s_b)

    def expand_queries(i):
        qx = qx_bufs[i % 2]
        qx[0:half, :] = _expand_maps(q_ref[i * blk:i * blk + hb, :])
        qx[half:rows, :] = _expand_maps(q_ref[i * blk + hb:(i + 1) * blk, :])

    def scores(n):
        i, j = steps[n]
        s_bufs[n % 2][...] = lax.dot_general(qx_bufs[i % 2][...], kb[j * blk:(j + 1) * blk, :], dims,
                                             preferred_element_type=F32)

    def update(j, src, h, width, row_offset):
        rs = slice(h * half, (h + 1) * half)
        s = src[rs, 0:width]
        if row_offset is not None:
            r = lax.broadcasted_iota(jnp.int32, (half, width), 0)
            c = lax.broadcasted_iota(jnp.int32, (half, width), 1)
            s = jnp.where((row_offset + r % hb) // CHUNK >= c // CHUNK, s, -jnp.inf)
        m_prev = m_ref[rs, :]
        m_new = jnp.maximum(m_prev, jnp.max(s, axis=-1, keepdims=True))
        alpha = jnp.exp2(m_prev - m_new)
        p = jnp.exp2(s - jnp.concatenate([m_new] * (width // LANES), axis=1)).astype(BF16)
        pv = jnp.dot(p, vb[j * blk:j * blk + width, :], preferred_element_type=F32)
        acc_ref[rs, :] = jnp.concatenate([alpha, alpha], axis=1) * acc_ref[rs, :] + pv
        m_ref[rs, :] = m_new

    expand_queries(0)
    scores(0)
    for n, (i, j) in enumerate(steps):
        if n + 1 < len(steps):
            if steps[n + 1][1] == 0:
                expand_queries(steps[n + 1][0])
            scores(n + 1)
        if j == 0:
            m_ref[...] = jnp.full((rows, LANES), -jnp.inf, F32)
            acc_ref[...] = jnp.zeros((rows, 2 * LANES), F32)
        src = s_bufs[n % 2]
        if j < i:
            update(j, src, 0, blk, None)
            update(j, src, 1, blk, None)
        else:
            update(j, src, 0, hb, 0)
            update(j, src, 1, blk, hb)
            for h in range(2):
                acc = acc_ref[h * half:(h + 1) * half, :]
                o = acc[:, 0:LANES] / acc[:, LANES:2 * LANES]
                o_ref[i * blk + h * hb:i * blk + (h + 1) * hb, :] = _diff_combine(
                    o, lam_ref[0, 0], g_ref[...], hb, lam_init).astype(BF16)


def _attn_prompt(lam, q, k, v, g128, blk, lam_init):
    bsz, seq, _ = q.shape
    rows = MAPS_PER_SLAB * blk
    slab_spec = pl.BlockSpec((None, seq, LANES), lambda b, j: (b, 0, j))
    return pl.pallas_call(
        functools.partial(_attn_prompt_kernel, blk=blk, lam_init=lam_init),
        grid=(bsz, ATTN_WIDTH // LANES),
        in_specs=[pl.BlockSpec(memory_space=pltpu.SMEM), slab_spec, slab_spec, slab_spec,
                  pl.BlockSpec((1, LANES), lambda b, j: (0, 0))],
        out_specs=slab_spec,
        out_shape=jax.ShapeDtypeStruct((bsz, seq, ATTN_WIDTH), BF16),
        scratch_shapes=[pltpu.VMEM((seq, LANES), BF16), pltpu.VMEM((seq, 2 * LANES), BF16),
                        pltpu.VMEM((rows, LANES), BF16), pltpu.VMEM((rows, LANES), BF16),
                        pltpu.VMEM((rows, blk), F32), pltpu.VMEM((rows, blk), F32),
                        pltpu.VMEM((rows, LANES), F32), pltpu.VMEM((rows, 2 * LANES), F32)],
        compiler_params=_params(("parallel", "parallel")),
        name="attn_prompt",
    )(lam, q, k, v, g128)


def _attn_sample_kernel(lam_ref, q_ref, kc_ref, vc_ref, kn_ref, vn_ref, g_ref, o_ref, *, n, lam_init):
    dims = (((1,), (1,)), ((), ()))
    for j in range(ATTN_WIDTH // LANES):
        cols = slice(j * LANES, (j + 1) * LANES)
        qx = _expand_maps(q_ref[:, cols])
        s_c = jnp.dot(qx, kc_ref[cols, :].astype(BF16), preferred_element_type=F32)
        s_n = lax.dot_general(qx, kn_ref[:, cols].astype(BF16), dims, preferred_element_type=F32)
        m = jnp.maximum(jnp.max(s_c, axis=-1, keepdims=True), jnp.max(s_n, axis=-1, keepdims=True))
        p_c = jnp.exp2(s_c - m)
        p_n = jnp.exp2(s_n - m)
        l = jnp.sum(p_c, axis=-1, keepdims=True) + jnp.sum(p_n, axis=-1, keepdims=True)
        acc = (lax.dot_general(p_c.astype(BF16), vc_ref[cols, :].astype(BF16), dims, preferred_element_type=F32)
               + jnp.dot(p_n.astype(BF16), vn_ref[:, cols].astype(BF16), preferred_element_type=F32))
        o_ref[:, cols] = _diff_combine(acc / l, lam_ref[0, 0], g_ref[...], n, lam_init).astype(BF16)


def _attn_sample(lam, q, cache_kt, cache_vt, k_new, v_new, g128, lam_init):
    bsz, n, _ = q.shape
    past = cache_kt.shape[2]
    assert past % CHUNK == 0 and n <= CHUNK
    cache_spec = pl.BlockSpec((None, ATTN_WIDTH, past), lambda b: (b, 0, 0))
    new_spec = pl.BlockSpec((None, n, ATTN_WIDTH), lambda b: (b, 0, 0))
    return pl.pallas_call(
        functools.partial(_attn_sample_kernel, n=n, lam_init=lam_init),
        grid=(bsz,),
        in_specs=[pl.BlockSpec(memory_space=pltpu.SMEM), new_spec, cache_spec, cache_spec, new_spec, new_spec,
                  pl.BlockSpec((1, LANES), lambda b: (0, 0))],
        out_specs=new_spec,
        out_shape=jax.ShapeDtypeStruct((bsz, n, ATTN_WIDTH), BF16),
        compiler_params=_params(("parallel",)),
        name="attn_sample",
    )(lam, q, cache_kt, cache_vt, k_new, v_new, g128)


def _ssm_kernel(u_ref, h0re_ref, h0im_ref, are_ref, aim_ref, bre_ref, bim_ref, cre_ref, cim_ref, d_ref,
                wglu_ref, bglu_ref, out_ref, hre_out, him_out, bure, buim, hsre, hsim, hre, him, res,
                *, tt):
    i = pl.program_id(1)
    dims = (((1,), (1,)), ((), ()))

    @pl.when(i == 0)
    def _():
        for b in range(SUBLANES):
            hre[b:b + 1, :] = _lane_concat_rows(h0re_ref[b])
            him[b:b + 1, :] = _lane_concat_rows(h0im_ref[b])

    ys = []
    for s in range(N_SLABS):
        us = u_ref[s].astype(BF16)
        cols = slice(s * SLAB_STATES, (s + 1) * SLAB_STATES)
        bure[:, cols] = jnp.dot(us, bre_ref[s], preferred_element_type=F32)
        buim[:, cols] = jnp.dot(us, bim_ref[s], preferred_element_type=F32)
        ar = jnp.broadcast_to(are_ref[:, cols], (SUBLANES, SLAB_STATES))
        ai = jnp.broadcast_to(aim_ref[:, cols], (SUBLANES, SLAB_STATES))
        hr = hre[:, cols]
        hi = him[:, cols]
        for t in range(tt):
            r = slice(t * SUBLANES, (t + 1) * SUBLANES)
            hr, hi = ar * hr - ai * hi + bure[r, cols], ar * hi + ai * hr + buim[r, cols]
            hsre[r, cols] = hr
            hsim[r, cols] = hi
        hre[:, cols] = hr
        him[:, cols] = hi
        ys.append(lax.dot_general(hsre[:, cols].astype(BF16), cre_ref[s], dims, preferred_element_type=F32)
                  - lax.dot_general(hsim[:, cols].astype(BF16), cim_ref[s], dims, preferred_element_type=F32)
                  + d_ref[:, s * LANES:(s + 1) * LANES] * u_ref[s])
    y = jnp.concatenate(ys, axis=1)
    z = jnp.dot(jax.nn.gelu(y).astype(BF16), wglu_ref[...], preferred_element_type=F32) + bglu_ref[...]
    gated = z[:, :SSM_WIDTH] * jax.nn.sigmoid(z[:, SSM_WIDTH:])
    for s in range(N_SLABS):
        res[s] = gated[:, s * LANES:(s + 1) * LANES]
    for b in range(SUBLANES):
        for s in range(N_SLABS):
            out_ref[b, :, s * LANES:(s + 1) * LANES] = res[s, pl.ds(b, tt, stride=SUBLANES), :].astype(BF16)

    @pl.when(i == pl.num_programs(1) - 1)
    def _():
        for b in range(SUBLANES):
            for g in range(N_SSM_GROUPS):
                lanes = slice(g * SSM_STATE, (g + 1) * SSM_STATE)
                hre_out[b, g:g + 1, :] = hre[b:b + 1, lanes]
                him_out[b, g:g + 1, :] = him[b:b + 1, lanes]


def _ssm(l, u_t, h0_re, h0_im, lb_re, lb_im, bre, bim, cre, cim, d, w_glu, b_glu, tt):
    nblk, _, rows_total, _ = u_t.shape
    seq = rows_total // SUBLANES
    rows = SUBLANES * tt
    grid = (nblk, seq // tt)
    const = lambda shape: pl.BlockSpec(shape, lambda b, i: (0,) * len(shape))
    layer = lambda *shape: pl.BlockSpec((None,) + shape, lambda b, i: (l,) + (0,) * len(shape))
    slab_spec = const((N_SLABS, LANES, SLAB_STATES))
    state_spec = pl.BlockSpec((SUBLANES, N_SSM_GROUPS, SSM_STATE), lambda b, i: (b, 0, 0))
    state_shape = jax.ShapeDtypeStruct((nblk * SUBLANES, N_SSM_GROUPS, SSM_STATE), F32)
    return pl.pallas_call(
        functools.partial(_ssm_kernel, tt=tt),
        grid=grid,
        in_specs=[pl.BlockSpec((None, N_SLABS, rows, LANES), lambda b, i: (b, 0, i, 0)),
                  state_spec, state_spec,
                  const((1, SSM_LANES)), const((1, SSM_LANES)),
                  slab_spec, slab_spec, slab_spec, slab_spec,
                  const((1, SSM_WIDTH)), const((SSM_WIDTH, 2 * SSM_WIDTH)), layer(1, 2 * SSM_WIDTH)],
        out_specs=(pl.BlockSpec((SUBLANES, tt, SSM_WIDTH), lambda b, i: (b, i, 0)), state_spec, state_spec),
        out_shape=(jax.ShapeDtypeStruct((nblk * SUBLANES, seq, SSM_WIDTH), BF16), state_shape, state_shape),
        scratch_shapes=[pltpu.VMEM((rows, SSM_LANES), F32) for _ in range(4)]
                       + [pltpu.VMEM((SUBLANES, SSM_LANES), F32) for _ in range(2)]
                       + [pltpu.VMEM((N_SLABS, rows, LANES), F32)],
        compiler_params=_params(("parallel", "arbitrary")),
        name="ssm",
    )(u_t, h0_re, h0_im, lb_re, lb_im, bre, bim, cre, cim, d, w_glu, b_glu)


def _layer_norm(x, g, b):
    mu = jnp.mean(x, axis=-1, keepdims=True)
    xc = x - mu
    var = jnp.mean(xc * xc, axis=-1, keepdims=True)
    return xc * lax.rsqrt(var + LN_EPS) * g + b


def _post_kernel(x_ref, a_ref, s_ref, wout_ref, g1_ref, b1_ref, wup_ref, wdown_ref, g2_ref, b2_ref, o_ref,
                 *, nb, tl, ff_chunk):
    chains = (0, 1)
    n = nb * tl // len(chains)
    if nb == 1:
        views = [(slice(None), slice(c * n, (c + 1) * n)) for c in chains]
    else:
        views = [(slice(c * nb // len(chains), (c + 1) * nb // len(chains)), slice(None)) for c in chains]
    xs = [x_ref[v].reshape(n, D_MODEL) for v in views]
    mix = [jnp.dot(jnp.concatenate([a_ref[v].reshape(n, ATTN_WIDTH), s_ref[v].reshape(n, SSM_WIDTH)], axis=1),
                   wout_ref[...], preferred_element_type=F32) for v in views]
    x1 = [None, None]
    x1b = [None, None]
    h = [None, None]
    ff = [jnp.zeros((n, D_MODEL), F32) for _ in chains]
    n_chunks = D_FF // ff_chunk

    def up(c, k):
        return jnp.dot(x1b[c], wup_ref[:, k * ff_chunk:(k + 1) * ff_chunk], preferred_element_type=F32)

    def down(c, k):
        act = jnp.square(jnp.maximum(h[c], 0.0)).astype(BF16)
        return ff[c] + jnp.dot(act, wdown_ref[k * ff_chunk:(k + 1) * ff_chunk, :], preferred_element_type=F32)

    x1[0] = _layer_norm(DEEPNORM_ALPHA * xs[0] + mix[0], g1_ref[...], b1_ref[...])
    x1b[0] = x1[0].astype(BF16)
    h[0] = up(0, 0)
    x1[1] = _layer_norm(DEEPNORM_ALPHA * xs[1] + mix[1], g1_ref[...], b1_ref[...])
    x1b[1] = x1[1].astype(BF16)
    for k in range(n_chunks):
        h[1] = up(1, k)
        ff[0] = down(0, k)
        if k + 1 < n_chunks:
            h[0] = up(0, k + 1)
        ff[1] = down(1, k)
    for c in chains:
        x2 = _layer_norm(DEEPNORM_ALPHA * x1[c] + ff[c], g2_ref[...], b2_ref[...])
        o_ref[views[c]] = x2.reshape(x_ref[views[c]].shape)


def _post(l, x, attn, ssm, w_out, g1, b1, w_up, w_down, g2, b2, nb, tl):
    bsz, seq, _ = x.shape
    grid = (bsz // nb, seq // tl)
    act_spec = lambda width: pl.BlockSpec((nb, tl, width), lambda b, i: (b, i, 0))
    const = lambda shape: pl.BlockSpec(shape, lambda b, i: (0,) * len(shape), pipeline_mode=pl.Buffered(1))
    layer = lambda *shape: pl.BlockSpec((None,) + shape, lambda b, i: (l,) + (0,) * len(shape))
    return pl.pallas_call(
        functools.partial(_post_kernel, nb=nb, tl=tl, ff_chunk=1024),
        grid=grid,
        in_specs=[act_spec(D_MODEL), act_spec(ATTN_WIDTH), act_spec(SSM_WIDTH),
                  const((D_MODEL, D_MODEL)), layer(1, D_MODEL), layer(1, D_MODEL),
                  const((D_MODEL, D_FF)), const((D_FF, D_MODEL)), layer(1, D_MODEL), layer(1, D_MODEL)],
        out_specs=act_spec(D_MODEL),
        out_shape=jax.ShapeDtypeStruct((bsz, seq, D_MODEL), F32),
        compiler_params=_params(("parallel", "parallel")),
        name="post",
    )(x, attn, ssm, w_out, g1, b1, w_up, w_down, g2, b2)


def kernel(x_prompt, x_sample, cache_k, cache_v, state_ssm_re, state_ssm_im, w_in, lambda_q1, lambda_k1,
           lambda_q2, lambda_k2, subln_g, ssm_a_re, ssm_a_im, ssm_log_dt, ssm_b_re, ssm_b_im, ssm_c_re,
           ssm_c_im, ssm_d, w_glu, b_glu, w_out, ln1_g, ln1_b, w_up, w_down, ln2_g, ln2_b):
    assert w_in.shape[0] == DEPTH
    l = 0
    lam_init = _lambda_init(l)
    bp, seq, _ = x_prompt.shape
    bs, dec_seq, _ = x_sample.shape
    past = cache_k.shape[2]

    lb_re, lb_im, bre, bim, cre, cim, d, g128, lam = _prep(
        l, ssm_a_re, ssm_a_im, ssm_log_dt, ssm_b_re, ssm_b_im, ssm_c_re, ssm_c_im, ssm_d, subln_g,
        lambda_q1, lambda_k1, lambda_q2, lambda_k2, lam_init)
    w_in_bf, w_glu_bf, w_out_bf, w_up_bf, w_down_bf = (w[l].astype(BF16) for w in (w_in, w_glu, w_out, w_up, w_down))
    row = lambda a: a.reshape(a.shape[0], 1, a.shape[1])
    b_glu3 = row(b_glu)
    ln = [row(a) for a in (ln1_g, ln1_b, ln2_g, ln2_b)]

    def layer(after, x, positions, cache, h0_re, h0_im, nb, tl, tt):
        bsz, n, _ = x.shape
        q, k, v, u_t = _inproj(after, x, w_in_bf, _rope_tables(positions), nb, tl)
        if cache is None:
            attn = _attn_prompt(lam, q, k, v, g128, 512, lam_init)
        else:
            attn = _attn_sample(lam, q, cache[0], cache[1], k, v, g128, lam_init)
        ssm, h_re, h_im = _ssm(l, u_t, h0_re, h0_im, lb_re, lb_im, bre, bim, cre, cim, d, w_glu_bf, b_glu3, tt)
        y = _post(l, x, attn, ssm, w_out_bf, ln[0], ln[1], w_up_bf, w_down_bf, ln[2], ln[3], nb, tl)
        shape_kv = (1, bsz, n, N_HEADS, HEAD_DIM)
        return y, k.reshape(shape_kv), v.reshape(shape_kv), h_re[None], h_im[None]

    cache = tuple(jnp.transpose(c[l], (0, 2, 3, 1)).reshape(bs, ATTN_WIDTH, past) for c in (cache_k, cache_v))
    ys, ks, vs, rs, is_ = layer(lam, x_sample, past + np.arange(dec_seq), cache, state_ssm_re[l], state_ssm_im[l],
                                nb=SUBLANES, tl=dec_seq, tt=dec_seq)
    zeros = jnp.zeros((bp, N_SSM_GROUPS, SSM_STATE), F32)
    yp, kp, vp, rp, ip = layer(ys[0, 0:1, 0:1], x_prompt, np.arange(seq), None, zeros, zeros, nb=1, tl=512, tt=64)
    return (yp, ys, kp, vp, rp, ip, ks, vs, rs, is_)
```

```python
import functools
import math

import numpy as np
import jax
import jax.numpy as jnp
from jax import lax
from jax.experimental import pallas as pl
from jax.experimental.pallas import tpu as pltpu

D_MODEL = 1024
DEPTH = 1
CHUNK = 64
ATTN_WIDTH = 512
SSM_WIDTH = 512
N_HEADS = 8
HEAD_DIM = 64
HALF_DIM = 32
ROT_DIM = 8
ROPE_THETA = 500000.0
SSM_GROUP = 16
N_SSM_GROUPS = 32
SSM_STATE = 64
D_FF = 4 * D_MODEL
LN_EPS = 1e-5
SUBLN_EPS = 1e-5
DEEPNORM_ALPHA = (2 * DEPTH) ** 0.25
QK_SCALE = HALF_DIM ** -0.5
LOG2_E = math.log2(math.e)

SUBLANES = 8
LANES = 128
SSM_LANES = N_SSM_GROUPS * SSM_STATE
GROUPS_PER_SLAB = LANES // SSM_GROUP
N_SLABS = SSM_WIDTH // LANES
SLAB_STATES = GROUPS_PER_SLAB * SSM_STATE
HEADS_PER_SLAB = LANES // HEAD_DIM
MAPS_PER_SLAB = LANES // HALF_DIM
VMEM_LIMIT_BYTES = 56 * 1024 * 1024

F32 = jnp.float32
BF16 = jnp.bfloat16


def _lambda_init(layer_idx):
    return 0.8 - 0.6 * math.exp(-0.3 * layer_idx)


def _params(semantics):
    return pltpu.CompilerParams(dimension_semantics=semantics, vmem_limit_bytes=VMEM_LIMIT_BYTES)


def _lane_concat_rows(x):
    return jnp.concatenate([x[r:r + 1, :] for r in range(x.shape[0])], axis=1)


def _block_diag_slabs(blocks):
    slabs = []
    for s in range(N_SLABS):
        rows = []
        for gi in range(GROUPS_PER_SLAB):
            pieces = []
            if gi > 0:
                pieces.append(jnp.zeros((SSM_GROUP, gi * SSM_STATE), F32))
            pieces.append(blocks[s * GROUPS_PER_SLAB + gi])
            if gi < GROUPS_PER_SLAB - 1:
                pieces.append(jnp.zeros((SSM_GROUP, (GROUPS_PER_SLAB - 1 - gi) * SSM_STATE), F32))
            rows.append(jnp.concatenate(pieces, axis=1))
        slabs.append(jnp.concatenate(rows, axis=0))
    return slabs


def _prep_kernel(are_ref, aim_ref, logdt_ref, bre_ref, bim_ref, cre_ref, cim_ref, d_ref, g_ref,
                 lq1_ref, lk1_ref, lq2_ref, lk2_ref,
                 lbre_ref, lbim_ref, bbre_ref, bbim_ref, ccre_ref, ccim_ref, dd_ref, gg_ref, lam_ref,
                 *, lam_init):
    g = N_SSM_GROUPS
    eye = lax.broadcasted_iota(jnp.int32, (g, g), 0) == lax.broadcasted_iota(jnp.int32, (g, g), 1)
    logdt = jnp.sum(jnp.where(eye, jnp.broadcast_to(logdt_ref[...], (g, g)), 0.0), axis=-1, keepdims=True)
    dt = jnp.exp(logdt)
    ar = are_ref[...]
    ai = aim_ref[...]
    mag = jnp.exp(ar * dt)
    lb_re = mag * jnp.cos(ai * dt)
    lb_im = mag * jnp.sin(ai * dt)
    nr = lb_re - 1.0
    ni = lb_im
    den = ar * ar + ai * ai
    f_re = (nr * ar + ni * ai) / den
    f_im = (ni * ar - nr * ai) / den
    lbre_ref[...] = _lane_concat_rows(lb_re)
    lbim_ref[...] = _lane_concat_rows(lb_im)
    br = bre_ref[...]
    bi = bim_ref[...]
    bb_re = f_re[:, None, :] * br - f_im[:, None, :] * bi
    bb_im = f_re[:, None, :] * bi + f_im[:, None, :] * br
    for out_ref, blocks in ((bbre_ref, bb_re), (bbim_ref, bb_im), (ccre_ref, cre_ref[...]), (ccim_ref, cim_ref[...])):
        for s, slab in enumerate(_block_diag_slabs(blocks)):
            out_ref[s] = slab.astype(BF16)
    dd_ref[...] = _lane_concat_rows(d_ref[...])
    gg_ref[...] = jnp.concatenate([g_ref[...]] * HEADS_PER_SLAB, axis=1)
    s1 = jnp.sum(lq1_ref[...] * lk1_ref[...], axis=-1, keepdims=True)
    s2 = jnp.sum(lq2_ref[...] * lk2_ref[...], axis=-1, keepdims=True)
    lam_ref[...] = jnp.exp(s1) - jnp.exp(s2) + lam_init


def _prep(l, a_re, a_im, log_dt, b_re, b_im, c_re, c_im, d, subln_g, lq1, lk1, lq2, lk2, lam_init):
    g, p, c = N_SSM_GROUPS, SSM_STATE, SSM_GROUP
    layer = lambda *shape: pl.BlockSpec((None,) + shape, lambda: (l,) + (0,) * len(shape))
    row = lambda a: a.reshape(a.shape[0], 1, a.shape[1])
    slab = jax.ShapeDtypeStruct((N_SLABS, LANES, SLAB_STATES), BF16)
    out_shape = (jax.ShapeDtypeStruct((1, SSM_LANES), F32), jax.ShapeDtypeStruct((1, SSM_LANES), F32),
                 slab, slab, slab, slab,
                 jax.ShapeDtypeStruct((1, SSM_WIDTH), F32), jax.ShapeDtypeStruct((1, LANES), F32),
                 jax.ShapeDtypeStruct((1, 1), F32))
    return pl.pallas_call(
        functools.partial(_prep_kernel, lam_init=lam_init),
        in_specs=[layer(g, p), layer(g, p), layer(1, g), layer(g, c, p), layer(g, c, p), layer(g, c, p),
                  layer(g, c, p), layer(g, c), layer(1, HEAD_DIM)] + [layer(1, HALF_DIM)] * 4,
        out_shape=out_shape, name="prep",
    )(a_re, a_im, row(log_dt), jnp.swapaxes(b_re, 2, 3), jnp.swapaxes(b_im, 2, 3), c_re, c_im, d,
      row(subln_g), row(lq1), row(lk1), row(lq2), row(lk2))


def _rope_tables(positions):
    inv = ROPE_THETA ** (-np.arange(0, ROT_DIM, 2, dtype=np.float64) / ROT_DIM)
    ang = np.asarray(positions, np.float64)[:, None] * inv[None, :]
    r = np.arange(LANES) % HALF_DIM
    half = ROT_DIM // 2
    idx = r % half
    cos = np.where(r[None, :] < ROT_DIM, np.cos(ang)[:, idx], 1.0)
    sin = np.sin(ang)[:, idx]
    s_up = np.where(r[None, :] < half, -sin, 0.0)
    s_dn = np.where((r[None, :] >= half) & (r[None, :] < ROT_DIM), sin, 0.0)
    return (jnp.asarray(cos, F32), jnp.asarray(s_up, F32), jnp.asarray(s_dn, F32))


def _inproj_kernel(after_ref, x_ref, w_ref, cos_ref, sup_ref, sdn_ref, q_ref, k_ref, v_ref, u_ref, *, nb, tl):
    del after_ref
    half = ROT_DIM // 2
    x = x_ref[...].reshape(nb * tl, D_MODEL).astype(BF16)
    cos = jnp.concatenate([cos_ref[...]] * nb, axis=0)
    sup = jnp.concatenate([sup_ref[...]] * nb, axis=0)
    sdn = jnp.concatenate([sdn_ref[...]] * nb, axis=0)

    def rope(t):
        slabs = []
        for j in range(ATTN_WIDTH // LANES):
            s = t[:, j * LANES:(j + 1) * LANES]
            slabs.append(s * cos + pltpu.roll(s, LANES - half, 1) * sup + pltpu.roll(s, half, 1) * sdn)
        return jnp.concatenate(slabs, axis=1)

    q = jnp.dot(x, w_ref[:, 0:ATTN_WIDTH], preferred_element_type=F32)
    q_ref[...] = (rope(q) * (QK_SCALE * LOG2_E)).astype(BF16).reshape(nb, tl, ATTN_WIDTH)
    k = jnp.dot(x, w_ref[:, ATTN_WIDTH:2 * ATTN_WIDTH], preferred_element_type=F32)
    k_ref[...] = rope(k).reshape(nb, tl, ATTN_WIDTH)
    v = jnp.dot(x, w_ref[:, 2 * ATTN_WIDTH:3 * ATTN_WIDTH], preferred_element_type=F32)
    v_ref[...] = v.reshape(nb, tl, ATTN_WIDTH)
    u = jnp.dot(x, w_ref[:, 3 * ATTN_WIDTH:], preferred_element_type=F32)
    first_slot = (pl.program_id(1) * nb) % SUBLANES
    for b in range(nb):
        for s in range(N_SLABS):
            u_ref[s, pl.ds(first_slot + b, tl, stride=SUBLANES), :] = u[b * tl:(b + 1) * tl,
                                                                        s * LANES:(s + 1) * LANES]


def _inproj(after, x, w_bf, tables, nb, tl):
    bsz, seq, _ = x.shape
    nblk = bsz // SUBLANES
    per_blk = SUBLANES // nb
    grid = (seq // tl, bsz // nb)
    tab_spec = pl.BlockSpec((tl, LANES), lambda i, b: (i, 0))
    act_spec = lambda width: pl.BlockSpec((nb, tl, width), lambda i, b: (b, i, 0))
    out_shape = (jax.ShapeDtypeStruct((bsz, seq, ATTN_WIDTH), BF16),
                 jax.ShapeDtypeStruct((bsz, seq, ATTN_WIDTH), F32),
                 jax.ShapeDtypeStruct((bsz, seq, ATTN_WIDTH), F32),
                 jax.ShapeDtypeStruct((nblk, N_SLABS, seq * SUBLANES, LANES), F32))
    return pl.pallas_call(
        functools.partial(_inproj_kernel, nb=nb, tl=tl),
        grid=grid,
        in_specs=[pl.BlockSpec(memory_space=pltpu.SMEM), act_spec(D_MODEL),
                  pl.BlockSpec((D_MODEL, 4 * ATTN_WIDTH), lambda i, b: (0, 0)),
                  tab_spec, tab_spec, tab_spec],
        out_specs=(act_spec(ATTN_WIDTH), act_spec(ATTN_WIDTH), act_spec(ATTN_WIDTH),
                   pl.BlockSpec((None, N_SLABS, tl * SUBLANES, LANES), lambda i, b: (b // per_blk, 0, i, 0))),
        out_shape=out_shape,
        compiler_params=_params(("parallel", "arbitrary")),
        name="inproj",
    )(after, x, w_bf, *tables)


def _expand_maps(q):
    lane = lax.broadcasted_iota(jnp.int32, q.shape, 1)
    zero = jnp.zeros_like(q)
    return jnp.concatenate(
        [jnp.where((lane >= m * HALF_DIM) & (lane < (m + 1) * HALF_DIM), q, zero) for m in range(MAPS_PER_SLAB)],
        axis=0)


def _diff_combine(o, lam, g, n, lam_init):
    lane = lax.broadcasted_iota(jnp.int32, (n, LANES), 1)
    first = lane < HEAD_DIM
    d = jnp.where(first, o[0:n] - lam * o[n:2 * n], o[2 * n:3 * n] - lam * o[3 * n:4 * n])
    sq = d * d
    ss_a = jnp.sum(jnp.where(first, sq, 0.0), axis=-1, keepdims=True)
    ss_b = jnp.sum(jnp.where(first, 0.0, sq), axis=-1, keepdims=True)
    ms = jnp.where(first, ss_a, ss_b) * (1.0 / HEAD_DIM)
    return d * lax.rsqrt(ms + SUBLN_EPS) * g * (1.0 - lam_init)


def _attn_prompt_kernel(lam_ref, q_ref, k_ref, v_ref, g_ref, o_ref, kb, vb, qx_a, qx_b, s_a, s_b, m_ref, acc_ref,
                        *, blk, lam_init):
    seq = q_ref.shape[0]
    hb = blk // 2
    half = MAPS_PER_SLAB * hb
    rows = 2 * half
    dims = (((1,), (1,)), ((), ()))

    kb[...] = k_ref[...].astype(BF16)
    vb[:, 0:LANES] = v_ref[...].astype(BF16)
    vb[:, LANES:2 * LANES] = jnp.ones(v_ref.shape, BF16)

    steps = [(i, j) for i in range(seq // blk) for j in range(i + 1)]
    qx_bufs = (qx_a, qx_b)
    s_bufs = (s_a, s_b)

    def expand_queries(i):
        qx = qx_bufs[i % 2]
        qx[0:half, :] = _expand_maps(q_ref[i * blk:i * blk + hb, :])
        qx[half:rows, :] = _expand_maps(q_ref[i * blk + hb:(i + 1) * blk, :])

    def scores(n):
        i, j = steps[n]
        s_bufs[n % 2][...] = lax.dot_general(qx_bufs[i % 2][...], kb[j * blk:(j + 1) * blk, :], dims,
                                             preferred_element_type=F32)

    def update(j, src, h, width, row_offset):
        rs = slice(h * half, (h + 1) * half)
        s = src[rs, 0:width]
        if row_offset is not None:
            r = lax.broadcasted_iota(jnp.int32, (half, width), 0)
            c = lax.broadcasted_iota(jnp.int32, (half, width), 1)
            s = jnp.where((row_offset + r % hb) // CHUNK >= c // CHUNK, s, -jnp.inf)
        m_prev = m_ref[rs, :]
        m_new = jnp.maximum(m_prev, jnp.max(s, axis=-1, keepdims=True))
        alpha = jnp.exp2(m_prev - m_new)
        p = jnp.exp2(s - jnp.concatenate([m_new] * (width // LANES), axis=1)).astype(BF16)
        pv = jnp.dot(p, vb[j * blk:j * blk + width, :], preferred_element_type=F32)
        acc_ref[rs, :] = jnp.concatenate([alpha, alpha], axis=1) * acc_ref[rs, :] + pv
        m_ref[rs, :] = m_new

    expand_queries(0)
    scores(0)
    for n, (i, j) in enumerate(steps):
        if n + 1 < len(steps):
            if steps[n + 1][1] == 0:
                expand_queries(steps[n + 1][0])
            scores(n + 1)
        if j == 0:
            m_ref[...] = jnp.full((rows, LANES), -jnp.inf, F32)
            acc_ref[...] = jnp.zeros((rows, 2 * LANES), F32)
        src = s_bufs[n % 2]
        if j < i:
            update(j, src, 0, blk, None)
            update(j, src, 1, blk, None)
        else:
            update(j, src, 0, hb, 0)
            update(j, src, 1, blk, hb)
            for h in range(2):
                acc = acc_ref[h * half:(h + 1) * half, :]
                o = acc[:, 0:LANES] / acc[:, LANES:2 * LANES]
                o_ref[i * blk + h * hb:i * blk + (h + 1) * hb, :] = _diff_combine(
                    o, lam_ref[0, 0], g_ref[...], hb, lam_init).astype(BF16)


def _attn_prompt(lam, q, k, v, g128, blk, lam_init):
    bsz, seq, _ = q.shape
    rows = MAPS_PER_SLAB * blk
    slab_spec = pl.BlockSpec((None, seq, LANES), lambda b, j: (b, 0, j))
    return pl.pallas_call(
        functools.partial(_attn_prompt_kernel, blk=blk, lam_init=lam_init),
        grid=(bsz, ATTN_WIDTH // LANES),
        in_specs=[pl.BlockSpec(memory_space=pltpu.SMEM), slab_spec, slab_spec, slab_spec,
                  pl.BlockSpec((1, LANES), lambda b, j: (0, 0))],
        out_specs=slab_spec,
        out_shape=jax.ShapeDtypeStruct((bsz, seq, ATTN_WIDTH), BF16),
        scratch_shapes=[pltpu.VMEM((seq, LANES), BF16), pltpu.VMEM((seq, 2 * LANES), BF16),
                        pltpu.VMEM((rows, LANES), BF16), pltpu.VMEM((rows, LANES), BF16),
                        pltpu.VMEM((rows, blk), F32), pltpu.VMEM((rows, blk), F32),
                        pltpu.VMEM((rows, LANES), F32), pltpu.VMEM((rows, 2 * LANES), F32)],
        compiler_params=_params(("parallel", "parallel")),
        name="attn_prompt",
    )(lam, q, k, v, g128)


def _attn_sample_kernel(lam_ref, q_ref, kc_ref, vc_ref, kn_ref, vn_ref, g_ref, o_ref, *, n, nseq, lam_init):
    dims = (((1,), (1,)), ((), ()))
    chains = [(b, slice(j * LANES, (j + 1) * LANES)) for b in range(nseq) for j in range(ATTN_WIDTH // LANES)]
    scores = []
    for b, cols in chains:
        qx = _expand_maps(q_ref[b, :, cols])
        scores.append((jnp.dot(qx, kc_ref[b, cols, :].astype(BF16), preferred_element_type=F32),
                       lax.dot_general(qx, kn_ref[b, :, cols].astype(BF16), dims, preferred_element_type=F32)))
    for (b, cols), (s_c, s_n) in zip(chains, scores):
        m = jnp.maximum(jnp.max(s_c, axis=-1, keepdims=True), jnp.max(s_n, axis=-1, keepdims=True))
        p_c = jnp.exp2(s_c - m)
        p_n = jnp.exp2(s_n - m)
        l = jnp.sum(p_c, axis=-1, keepdims=True) + jnp.sum(p_n, axis=-1, keepdims=True)
        acc = (lax.dot_general(p_c.astype(BF16), vc_ref[b, cols, :].astype(BF16), dims, preferred_element_type=F32)
               + jnp.dot(p_n.astype(BF16), vn_ref[b, :, cols].astype(BF16), preferred_element_type=F32))
        o_ref[b, :, cols] = _diff_combine(acc / l, lam_ref[0, 0], g_ref[...], n, lam_init).astype(BF16)


def _attn_sample(lam, q, cache_kt, cache_vt, k_new, v_new, g128, lam_init, nseq=2):
    bsz, n, _ = q.shape
    past = cache_kt.shape[2]
    assert past % CHUNK == 0 and n <= CHUNK
    cache_spec = pl.BlockSpec((nseq, ATTN_WIDTH, past), lambda b: (b, 0, 0))
    new_spec = pl.BlockSpec((nseq, n, ATTN_WIDTH), lambda b: (b, 0, 0))
    return pl.pallas_call(
        functools.partial(_attn_sample_kernel, n=n, nseq=nseq, lam_init=lam_init),
        grid=(bsz // nseq,),
        in_specs=[pl.BlockSpec(memory_space=pltpu.SMEM), new_spec, cache_spec, cache_spec, new_spec, new_spec,
                  pl.BlockSpec((1, LANES), lambda b: (0, 0))],
        out_specs=new_spec,
        out_shape=jax.ShapeDtypeStruct((bsz, n, ATTN_WIDTH), BF16),
        compiler_params=_params(("parallel",)),
        name="attn_sample",
    )(lam, q, cache_kt, cache_vt, k_new, v_new, g128)


def _ssm_kernel(u_ref, h0re_ref, h0im_ref, are_ref, aim_ref, bre_ref, bim_ref, cre_ref, cim_ref, d_ref,
                wglu_ref, bglu_ref, out_ref, hre_out, him_out, bure, buim, hsre, hsim, hre, him, res,
                *, tt):
    i = pl.program_id(1)
    dims = (((1,), (1,)), ((), ()))

    @pl.when(i == 0)
    def _():
        for b in range(SUBLANES):
            hre[b:b + 1, :] = _lane_concat_rows(h0re_ref[b])
            him[b:b + 1, :] = _lane_concat_rows(h0im_ref[b])

    ys = []
    for s in range(N_SLABS):
        us = u_ref[s].astype(BF16)
        cols = slice(s * SLAB_STATES, (s + 1) * SLAB_STATES)
        bure[:, cols] = jnp.dot(us, bre_ref[s], preferred_element_type=F32)
        buim[:, cols] = jnp.dot(us, bim_ref[s], preferred_element_type=F32)
        ar = jnp.broadcast_to(are_ref[:, cols], (SUBLANES, SLAB_STATES))
        ai = jnp.broadcast_to(aim_ref[:, cols], (SUBLANES, SLAB_STATES))
        hr = hre[:, cols]
        hi = him[:, cols]
        for t in range(tt):
            r = slice(t * SUBLANES, (t + 1) * SUBLANES)
            hr, hi = ar * hr - ai * hi + bure[r, cols], ar * hi + ai * hr + buim[r, cols]
            hsre[r, cols] = hr
            hsim[r, cols] = hi
        hre[:, cols] = hr
        him[:, cols] = hi
        ys.append(lax.dot_general(hsre[:, cols].astype(BF16), cre_ref[s], dims, preferred_element_type=F32)
                  - lax.dot_general(hsim[:, cols].astype(BF16), cim_ref[s], dims, preferred_element_type=F32)
                  + d_ref[:, s * LANES:(s + 1) * LANES] * u_ref[s])
    y = jnp.concatenate(ys, axis=1)
    z = jnp.dot(jax.nn.gelu(y).astype(BF16), wglu_ref[...], preferred_element_type=F32) + bglu_ref[...]
    gated = z[:, :SSM_WIDTH] * jax.nn.sigmoid(z[:, SSM_WIDTH:])
    for s in range(N_SLABS):
        res[s] = gated[:, s * LANES:(s + 1) * LANES]
    for b in range(SUBLANES):
        for s in range(N_SLABS):
            out_ref[b, :, s * LANES:(s + 1) * LANES] = res[s, pl.ds(b, tt, stride=SUBLANES), :].astype(BF16)

    @pl.when(i == pl.num_programs(1) - 1)
    def _():
        for b in range(SUBLANES):
            for g in range(N_SSM_GROUPS):
                lanes = slice(g * SSM_STATE, (g + 1) * SSM_STATE)
                hre_out[b, g:g + 1, :] = hre[b:b + 1, lanes]
                him_out[b, g:g + 1, :] = him[b:b + 1, lanes]


def _ssm(l, u_t, h0_re, h0_im, lb_re, lb_im, bre, bim, cre, cim, d, w_glu, b_glu, tt):
    nblk, _, rows_total, _ = u_t.shape
    seq = rows_total // SUBLANES
    rows = SUBLANES * tt
    grid = (nblk, seq // tt)
    const = lambda shape: pl.BlockSpec(shape, lambda b, i: (0,) * len(shape))
    layer = lambda *shape: pl.BlockSpec((None,) + shape, lambda b, i: (l,) + (0,) * len(shape))
    slab_spec = const((N_SLABS, LANES, SLAB_STATES))
    state_spec = pl.BlockSpec((SUBLANES, N_SSM_GROUPS, SSM_STATE), lambda b, i: (b, 0, 0))
    state_shape = jax.ShapeDtypeStruct((nblk * SUBLANES, N_SSM_GROUPS, SSM_STATE), F32)
    return pl.pallas_call(
        functools.partial(_ssm_kernel, tt=tt),
        grid=grid,
        in_specs=[pl.BlockSpec((None, N_SLABS, rows, LANES), lambda b, i: (b, 0, i, 0)),
                  state_spec, state_spec,
                  const((1, SSM_LANES)), const((1, SSM_LANES)),
                  slab_spec, slab_spec, slab_spec, slab_spec,
                  const((1, SSM_WIDTH)), const((SSM_WIDTH, 2 * SSM_WIDTH)), layer(1, 2 * SSM_WIDTH)],
        out_specs=(pl.BlockSpec((SUBLANES, tt, SSM_WIDTH), lambda b, i: (b, i, 0)), state_spec, state_spec),
        out_shape=(jax.ShapeDtypeStruct((nblk * SUBLANES, seq, SSM_WIDTH), BF16), state_shape, state_shape),
        scratch_shapes=[pltpu.VMEM((rows, SSM_LANES), F32) for _ in range(4)]
                       + [pltpu.VMEM((SUBLANES, SSM_LANES), F32) for _ in range(2)]
                       + [pltpu.VMEM((N_SLABS, rows, LANES), F32)],
        compiler_params=_params(("parallel", "arbitrary")),
        name="ssm",
    )(u_t, h0_re, h0_im, lb_re, lb_im, bre, bim, cre, cim, d, w_glu, b_glu)


def _layer_norm(x, g, b):
    mu = jnp.mean(x, axis=-1, keepdims=True)
    xc = x - mu
    var = jnp.mean(xc * xc, axis=-1, keepdims=True)
    return xc * lax.rsqrt(var + LN_EPS) * g + b


def _post_kernel(x_ref, a_ref, s_ref, wout_ref, g1_ref, b1_ref, wup_ref, wdown_ref, g2_ref, b2_ref, o_ref,
                 *, nb, tl, ff_chunk):
    chains = (0, 1)
    n = nb * tl // len(chains)
    if nb == 1:
        views = [(slice(None), slice(c * n, (c + 1) * n)) for c in chains]
    else:
        views = [(slice(c * nb // len(chains), (c + 1) * nb // len(chains)), slice(None)) for c in chains]
    xs = [x_ref[v].reshape(n, D_MODEL) for v in views]
    mix = [jnp.dot(jnp.concatenate([a_ref[v].reshape(n, ATTN_WIDTH), s_ref[v].reshape(n, SSM_WIDTH)], axis=1),
                   wout_ref[...], preferred_element_type=F32) for v in views]
    x1 = [None, None]
    x1b = [None, None]
    h = [None, None]
    ff = [jnp.zeros((n, D_MODEL), F32) for _ in chains]
    n_chunks = D_FF // ff_chunk

    def up(c, k):
        return jnp.dot(x1b[c], wup_ref[:, k * ff_chunk:(k + 1) * ff_chunk], preferred_element_type=F32)

    def down(c, k):
        act = jnp.square(jnp.maximum(h[c], 0.0)).astype(BF16)
        return ff[c] + jnp.dot(act, wdown_ref[k * ff_chunk:(k + 1) * ff_chunk, :], preferred_element_type=F32)

    x1[0] = _layer_norm(DEEPNORM_ALPHA * xs[0] + mix[0], g1_ref[...], b1_ref[...])
    x1b[0] = x1[0].astype(BF16)
    h[0] = up(0, 0)
    x1[1] = _layer_norm(DEEPNORM_ALPHA * xs[1] + mix[1], g1_ref[...], b1_ref[...])
    x1b[1] = x1[1].astype(BF16)
    for k in range(n_chunks):
        h[1] = up(1, k)
        ff[0] = down(0, k)
        if k + 1 < n_chunks:
            h[0] = up(0, k + 1)
        ff[1] = down(1, k)
    for c in chains:
        x2 = _layer_norm(DEEPNORM_ALPHA * x1[c] + ff[c], g2_ref[...], b2_ref[...])
        o_ref[views[c]] = x2.reshape(x_ref[views[c]].shape)


def _post(l, x, attn, ssm, w_out, g1, b1, w_up, w_down, g2, b2, nb, tl):
    bsz, seq, _ = x.shape
    grid = (bsz // nb, seq // tl)
    act_spec = lambda width: pl.BlockSpec((nb, tl, width), lambda b, i: (b, i, 0))
    const = lambda shape: pl.BlockSpec(shape, lambda b, i: (0,) * len(shape), pipeline_mode=pl.Buffered(1))
    layer = lambda *shape: pl.BlockSpec((None,) + shape, lambda b, i: (l,) + (0,) * len(shape))
    return pl.pallas_call(
        functools.partial(_post_kernel, nb=nb, tl=tl, ff_chunk=1024),
        grid=grid,
        in_specs=[act_spec(D_MODEL), act_spec(ATTN_WIDTH), act_spec(SSM_WIDTH),
                  const((D_MODEL, D_MODEL)), layer(1, D_MODEL), layer(1, D_MODEL),
                  const((D_MODEL, D_FF)), const((D_FF, D_MODEL)), layer(1, D_MODEL), layer(1, D_MODEL)],
        out_specs=act_spec(D_MODEL),
        out_shape=jax.ShapeDtypeStruct((bsz, seq, D_MODEL), F32),
        compiler_params=_params(("parallel", "parallel")),
        name="post",
    )(x, attn, ssm, w_out, g1, b1, w_up, w_down, g2, b2)


def kernel(x_prompt, x_sample, cache_k, cache_v, state_ssm_re, state_ssm_im, w_in, lambda_q1, lambda_k1,
           lambda_q2, lambda_k2, subln_g, ssm_a_re, ssm_a_im, ssm_log_dt, ssm_b_re, ssm_b_im, ssm_c_re,
           ssm_c_im, ssm_d, w_glu, b_glu, w_out, ln1_g, ln1_b, w_up, w_down, ln2_g, ln2_b):
    assert w_in.shape[0] == DEPTH
    l = 0
    lam_init = _lambda_init(l)
    bp, seq, _ = x_prompt.shape
    bs, dec_seq, _ = x_sample.shape
    past = cache_k.shape[2]

    lb_re, lb_im, bre, bim, cre, cim, d, g128, lam = _prep(
        l, ssm_a_re, ssm_a_im, ssm_log_dt, ssm_b_re, ssm_b_im, ssm_c_re, ssm_c_im, ssm_d, subln_g,
        lambda_q1, lambda_k1, lambda_q2, lambda_k2, lam_init)
    w_in_bf, w_glu_bf, w_out_bf, w_up_bf, w_down_bf = (w[l].astype(BF16) for w in (w_in, w_glu, w_out, w_up, w_down))
    row = lambda a: a.reshape(a.shape[0], 1, a.shape[1])
    b_glu3 = row(b_glu)
    ln = [row(a) for a in (ln1_g, ln1_b, ln2_g, ln2_b)]

    def layer(after, x, positions, cache, h0_re, h0_im, nb, tl, tt):
        bsz, n, _ = x.shape
        q, k, v, u_t = _inproj(after, x, w_in_bf, _rope_tables(positions), nb, tl)
        if cache is None:
            attn = _attn_prompt(lam, q, k, v, g128, 512, lam_init)
        else:
            attn = _attn_sample(lam, q, cache[0], cache[1], k, v, g128, lam_init)
        ssm, h_re, h_im = _ssm(l, u_t, h0_re, h0_im, lb_re, lb_im, bre, bim, cre, cim, d, w_glu_bf, b_glu3, tt)
        y = _post(l, x, attn, ssm, w_out_bf, ln[0], ln[1], w_up_bf, w_down_bf, ln[2], ln[3], nb, tl)
        shape_kv = (1, bsz, n, N_HEADS, HEAD_DIM)
        return y, k.reshape(shape_kv), v.reshape(shape_kv), h_re[None], h_im[None]

    cache = tuple(jnp.transpose(c[l], (0, 2, 3, 1)).reshape(bs, ATTN_WIDTH, past) for c in (cache_k, cache_v))
    ys, ks, vs, rs, is_ = layer(lam, x_sample, past + np.arange(dec_seq), cache, state_ssm_re[l], state_ssm_im[l],
                                nb=SUBLANES, tl=dec_seq, tt=dec_seq)
    zeros = jnp.zeros((bp, N_SSM_GROUPS, SSM_STATE), F32)
    yp, kp, vp, rp, ip = layer(ys[0, 0:1, 0:1], x_prompt, np.arange(seq), None, zeros, zeros, nb=1, tl=512, tt=64)
    return (yp, ys, kp, vp, rp, ip, ks, vs, rs, is_)
```

```python
import functools
import math

import numpy as np
import jax
import jax.numpy as jnp
from jax import lax
from jax.experimental import pallas as pl
from jax.experimental.pallas import tpu as pltpu

D_MODEL = 1024
DEPTH = 1
CHUNK = 64
ATTN_WIDTH = 512
SSM_WIDTH = 512
N_HEADS = 8
HEAD_DIM = 64
HALF_DIM = 32
ROT_DIM = 8
ROPE_THETA = 500000.0
SSM_GROUP = 16
N_SSM_GROUPS = 32
SSM_STATE = 64
D_FF = 4 * D_MODEL
LN_EPS = 1e-5
SUBLN_EPS = 1e-5
DEEPNORM_ALPHA = (2 * DEPTH) ** 0.25
QK_SCALE = HALF_DIM ** -0.5
LOG2_E = math.log2(math.e)

SUBLANES = 8
LANES = 128
SSM_LANES = N_SSM_GROUPS * SSM_STATE
GROUPS_PER_SLAB = LANES // SSM_GROUP
N_SLABS = SSM_WIDTH // LANES
SLAB_STATES = GROUPS_PER_SLAB * SSM_STATE
HEADS_PER_SLAB = LANES // HEAD_DIM
MAPS_PER_SLAB = LANES // HALF_DIM
VMEM_LIMIT_BYTES = 56 * 1024 * 1024

F32 = jnp.float32
BF16 = jnp.bfloat16


def _lambda_init(layer_idx):
    return 0.8 - 0.6 * math.exp(-0.3 * layer_idx)


def _params(semantics):
    return pltpu.CompilerParams(dimension_semantics=semantics, vmem_limit_bytes=VMEM_LIMIT_BYTES)


def _lane_concat_rows(x):
    return jnp.concatenate([x[r:r + 1, :] for r in range(x.shape[0])], axis=1)


def _block_diag_slabs(blocks):
    slabs = []
    for s in range(N_SLABS):
        rows = []
        for gi in range(GROUPS_PER_SLAB):
            pieces = []
            if gi > 0:
                pieces.append(jnp.zeros((SSM_GROUP, gi * SSM_STATE), F32))
            pieces.append(blocks[s * GROUPS_PER_SLAB + gi])
            if gi < GROUPS_PER_SLAB - 1:
                pieces.append(jnp.zeros((SSM_GROUP, (GROUPS_PER_SLAB - 1 - gi) * SSM_STATE), F32))
            rows.append(jnp.concatenate(pieces, axis=1))
        slabs.append(jnp.concatenate(rows, axis=0))
    return slabs


def _prep_kernel(are_ref, aim_ref, logdt_ref, bre_ref, bim_ref, cre_ref, cim_ref, d_ref, g_ref,
                 lq1_ref, lk1_ref, lq2_ref, lk2_ref,
                 lbre_ref, lbim_ref, bbre_ref, bbim_ref, ccre_ref, ccim_ref, dd_ref, gg_ref, lam_ref,
                 *, lam_init):
    g = N_SSM_GROUPS
    eye = lax.broadcasted_iota(jnp.int32, (g, g), 0) == lax.broadcasted_iota(jnp.int32, (g, g), 1)
    logdt = jnp.sum(jnp.where(eye, jnp.broadcast_to(logdt_ref[...], (g, g)), 0.0), axis=-1, keepdims=True)
    dt = jnp.exp(logdt)
    ar = are_ref[...]
    ai = aim_ref[...]
    mag = jnp.exp(ar * dt)
    lb_re = mag * jnp.cos(ai * dt)
    lb_im = mag * jnp.sin(ai * dt)
    nr = lb_re - 1.0
    ni = lb_im
    den = ar * ar + ai * ai
    f_re = (nr * ar + ni * ai) / den
    f_im = (ni * ar - nr * ai) / den
    lbre_ref[...] = _lane_concat_rows(lb_re)
    lbim_ref[...] = _lane_concat_rows(lb_im)
    br = bre_ref[...]
    bi = bim_ref[...]
    bb_re = f_re[:, None, :] * br - f_im[:, None, :] * bi
    bb_im = f_re[:, None, :] * bi + f_im[:, None, :] * br
    for out_ref, blocks in ((bbre_ref, bb_re), (bbim_ref, bb_im), (ccre_ref, cre_ref[...]), (ccim_ref, cim_ref[...])):
        for s, slab in enumerate(_block_diag_slabs(blocks)):
            out_ref[s] = slab.astype(BF16)
    dd_ref[...] = _lane_concat_rows(d_ref[...])
    gg_ref[...] = jnp.concatenate([g_ref[...]] * HEADS_PER_SLAB, axis=1)
    s1 = jnp.sum(lq1_ref[...] * lk1_ref[...], axis=-1, keepdims=True)
    s2 = jnp.sum(lq2_ref[...] * lk2_ref[...], axis=-1, keepdims=True)
    lam_ref[...] = jnp.exp(s1) - jnp.exp(s2) + lam_init


def _prep(l, a_re, a_im, log_dt, b_re, b_im, c_re, c_im, d, subln_g, lq1, lk1, lq2, lk2, lam_init):
    g, p, c = N_SSM_GROUPS, SSM_STATE, SSM_GROUP
    layer = lambda *shape: pl.BlockSpec((None,) + shape, lambda: (l,) + (0,) * len(shape))
    row = lambda a: a.reshape(a.shape[0], 1, a.shape[1])
    slab = jax.ShapeDtypeStruct((N_SLABS, LANES, SLAB_STATES), BF16)
    out_shape = (jax.ShapeDtypeStruct((1, SSM_LANES), F32), jax.ShapeDtypeStruct((1, SSM_LANES), F32),
                 slab, slab, slab, slab,
                 jax.ShapeDtypeStruct((1, SSM_WIDTH), F32), jax.ShapeDtypeStruct((1, LANES), F32),
                 jax.ShapeDtypeStruct((1, 1), F32))
    return pl.pallas_call(
        functools.partial(_prep_kernel, lam_init=lam_init),
        in_specs=[layer(g, p), layer(g, p), layer(1, g), layer(g, c, p), layer(g, c, p), layer(g, c, p),
                  layer(g, c, p), layer(g, c), layer(1, HEAD_DIM)] + [layer(1, HALF_DIM)] * 4,
        out_shape=out_shape, name="prep",
    )(a_re, a_im, row(log_dt), jnp.swapaxes(b_re, 2, 3), jnp.swapaxes(b_im, 2, 3), c_re, c_im, d,
      row(subln_g), row(lq1), row(lk1), row(lq2), row(lk2))


def _rope_tables(positions):
    inv = ROPE_THETA ** (-np.arange(0, ROT_DIM, 2, dtype=np.float64) / ROT_DIM)
    ang = np.asarray(positions, np.float64)[:, None] * inv[None, :]
    r = np.arange(LANES) % HALF_DIM
    half = ROT_DIM // 2
    idx = r % half
    cos = np.where(r[None, :] < ROT_DIM, np.cos(ang)[:, idx], 1.0)
    sin = np.sin(ang)[:, idx]
    s_up = np.where(r[None, :] < half, -sin, 0.0)
    s_dn = np.where((r[None, :] >= half) & (r[None, :] < ROT_DIM), sin, 0.0)
    return (jnp.asarray(cos, F32), jnp.asarray(s_up, F32), jnp.asarray(s_dn, F32))


def _inproj_kernel(after_ref, x_ref, w_ref, cos_ref, sup_ref, sdn_ref, q_ref, k_ref, v_ref, u_ref, *, nb, tl):
    del after_ref
    half = ROT_DIM // 2
    x = x_ref[...].reshape(nb * tl, D_MODEL).astype(BF16)
    cos = jnp.concatenate([cos_ref[...]] * nb, axis=0)
    sup = jnp.concatenate([sup_ref[...]] * nb, axis=0)
    sdn = jnp.concatenate([sdn_ref[...]] * nb, axis=0)

    def rope(t):
        slabs = []
        for j in range(ATTN_WIDTH // LANES):
            s = t[:, j * LANES:(j + 1) * LANES]
            slabs.append(s * cos + pltpu.roll(s, LANES - half, 1) * sup + pltpu.roll(s, half, 1) * sdn)
        return jnp.concatenate(slabs, axis=1)

    q = jnp.dot(x, w_ref[:, 0:ATTN_WIDTH], preferred_element_type=F32)
    q_ref[...] = (rope(q) * (QK_SCALE * LOG2_E)).astype(BF16).reshape(nb, tl, ATTN_WIDTH)
    k = jnp.dot(x, w_ref[:, ATTN_WIDTH:2 * ATTN_WIDTH], preferred_element_type=F32)
    k_ref[...] = rope(k).reshape(nb, tl, ATTN_WIDTH)
    v = jnp.dot(x, w_ref[:, 2 * ATTN_WIDTH:3 * ATTN_WIDTH], preferred_element_type=F32)
    v_ref[...] = v.reshape(nb, tl, ATTN_WIDTH)
    u = jnp.dot(x, w_ref[:, 3 * ATTN_WIDTH:], preferred_element_type=F32)
    first_slot = (pl.program_id(1) * nb) % SUBLANES
    for b in range(nb):
        for s in range(N_SLABS):
            u_ref[s, pl.ds(first_slot + b, tl, stride=SUBLANES), :] = u[b * tl:(b + 1) * tl,
                                                                        s * LANES:(s + 1) * LANES]


def _inproj(after, x, w_bf, tables, nb, tl):
    bsz, seq, _ = x.shape
    nblk = bsz // SUBLANES
    per_blk = SUBLANES // nb
    grid = (seq // tl, bsz // nb)
    tab_spec = pl.BlockSpec((tl, LANES), lambda i, b: (i, 0))
    act_spec = lambda width: pl.BlockSpec((nb, tl, width), lambda i, b: (b, i, 0))
    out_shape = (jax.ShapeDtypeStruct((bsz, seq, ATTN_WIDTH), BF16),
                 jax.ShapeDtypeStruct((bsz, seq, ATTN_WIDTH), F32),
                 jax.ShapeDtypeStruct((bsz, seq, ATTN_WIDTH), F32),
                 jax.ShapeDtypeStruct((nblk, N_SLABS, seq * SUBLANES, LANES), F32))
    return pl.pallas_call(
        functools.partial(_inproj_kernel, nb=nb, tl=tl),
        grid=grid,
        in_specs=[pl.BlockSpec(memory_space=pltpu.SMEM), act_spec(D_MODEL),
                  pl.BlockSpec((D_MODEL, 4 * ATTN_WIDTH), lambda i, b: (0, 0)),
                  tab_spec, tab_spec, tab_spec],
        out_specs=(act_spec(ATTN_WIDTH), act_spec(ATTN_WIDTH), act_spec(ATTN_WIDTH),
                   pl.BlockSpec((None, N_SLABS, tl * SUBLANES, LANES), lambda i, b: (b // per_blk, 0, i, 0))),
        out_shape=out_shape,
        compiler_params=_params(("parallel", "arbitrary")),
        name="inproj",
    )(after, x, w_bf, *tables)


def _expand_maps(q):
    lane = lax.broadcasted_iota(jnp.int32, q.shape, 1)
    zero = jnp.zeros_like(q)
    return jnp.concatenate(
        [jnp.where((lane >= m * HALF_DIM) & (lane < (m + 1) * HALF_DIM), q, zero) for m in range(MAPS_PER_SLAB)],
        axis=0)


def _diff_combine(o, lam, g, n, lam_init):
    lane = lax.broadcasted_iota(jnp.int32, (n, LANES), 1)
    first = lane < HEAD_DIM
    d = jnp.where(first, o[0:n] - lam * o[n:2 * n], o[2 * n:3 * n] - lam * o[3 * n:4 * n])
    sq = d * d
    ss_a = jnp.sum(jnp.where(first, sq, 0.0), axis=-1, keepdims=True)
    ss_b = jnp.sum(jnp.where(first, 0.0, sq), axis=-1, keepdims=True)
    ms = jnp.where(first, ss_a, ss_b) * (1.0 / HEAD_DIM)
    return d * lax.rsqrt(ms + SUBLN_EPS) * g * (1.0 - lam_init)


def _attn_prompt_kernel(lam_ref, q_ref, k_ref, v_ref, g_ref, o_ref, kb, vb, qx_a, qx_b, s_a, s_b, m_ref, acc_ref,
                        *, blk, lam_init):
    seq = q_ref.shape[0]
    hb = blk // 2
    half = MAPS_PER_SLAB * hb
    rows = 2 * half
    dims = (((1,), (1,)), ((), ()))

    kb[...] = k_ref[...].astype(BF16)
    vb[:, 0:LANES] = v_ref[...].astype(BF16)
    vb[:, LANES:2 * LANES] = jnp.ones(v_ref.shape, BF16)

    steps = [(i, j) for i in range(seq // blk) for j in range(i + 1)]
    qx_bufs = (qx_a, qx_b)
    s_bufs = (s_a, s_b)

    def expand_queries(i):
        qx = qx_bufs[i % 2]
        qx[0:half, :] = _expand_maps(q_ref[i * blk:i * blk + hb, :])
        qx[half:rows, :] = _expand_maps(q_ref[i * blk + hb:(i + 1) * blk, :])

    def scores(n):
        i, j = steps[n]
        s_bufs[n % 2][...] = lax.dot_general(qx_bufs[i % 2][...], kb[j * blk:(j + 1) * blk, :], dims,
                                             preferred_element_type=F32)

    def update(j, src, h, width, row_offset):
        rs = slice(h * half, (h + 1) * half)
        s = src[rs, 0:width]
        if row_offset is not None:
            r = lax.broadcasted_iota(jnp.int32, (half, width), 0)
            c = lax.broadcasted_iota(jnp.int32, (half, width), 1)
            s = jnp.where((row_offset + r % hb) // CHUNK >= c // CHUNK, s, -jnp.inf)
        m_prev = m_ref[rs, :]
        m_new = jnp.maximum(m_prev, jnp.max(s, axis=-1, keepdims=True))
        alpha = jnp.exp2(m_prev - m_new)
        p = jnp.exp2(s - jnp.concatenate([m_new] * (width // LANES), axis=1)).astype(BF16)
        pv = jnp.dot(p, vb[j * blk:j * blk + width, :], preferred_element_type=F32)
        acc_ref[rs, :] = jnp.concatenate([alpha, alpha], axis=1) * acc_ref[rs, :] + pv
        m_ref[rs, :] = m_new

    expand_queries(0)
    scores(0)
    for n, (i, j) in enumerate(steps):
        if n + 1 < len(steps):
            if steps[n + 1][1] == 0:
                expand_queries(steps[n + 1][0])
            scores(n + 1)
        if j == 0:
            m_ref[...] = jnp.full((rows, LANES), -jnp.inf, F32)
            acc_ref[...] = jnp.zeros((rows, 2 * LANES), F32)
        src = s_bufs[n % 2]
        if j < i:
            update(j, src, 0, blk, None)
            update(j, src, 1, blk, None)
        else:
            update(j, src, 0, hb, 0)
            update(j, src, 1, blk, hb)
            for h in range(2):
                acc = acc_ref[h * half:(h + 1) * half, :]
                o = acc[:, 0:LANES] / acc[:, LANES:2 * LANES]
                o_ref[i * blk + h * hb:i * blk + (h + 1) * hb, :] = _diff_combine(
                    o, lam_ref[0, 0], g_ref[...], hb, lam_init).astype(BF16)


def _attn_prompt(lam, q, k, v, g128, blk, lam_init):
    bsz, seq, _ = q.shape
    rows = MAPS_PER_SLAB * blk
    slab_spec = pl.BlockSpec((None, seq, LANES), lambda b, j: (b, 0, j))
    return pl.pallas_call(
        functools.partial(_attn_prompt_kernel, blk=blk, lam_init=lam_init),
        grid=(bsz, ATTN_WIDTH // LANES),
        in_specs=[pl.BlockSpec(memory_space=pltpu.SMEM), slab_spec, slab_spec, slab_spec,
                  pl.BlockSpec((1, LANES), lambda b, j: (0, 0))],
        out_specs=slab_spec,
        out_shape=jax.ShapeDtypeStruct((bsz, seq, ATTN_WIDTH), BF16),
        scratch_shapes=[pltpu.VMEM((seq, LANES), BF16), pltpu.VMEM((seq, 2 * LANES), BF16),
                        pltpu.VMEM((rows, LANES), BF16), pltpu.VMEM((rows, LANES), BF16),
                        pltpu.VMEM((rows, blk), F32), pltpu.VMEM((rows, blk), F32),
                        pltpu.VMEM((rows, LANES), F32), pltpu.VMEM((rows, 2 * LANES), F32)],
        compiler_params=_params(("parallel", "parallel")),
        name="attn_prompt",
    )(lam, q, k, v, g128)


def _attn_sample_kernel(lam_ref, q_ref, kc_ref, vc_ref, kn_ref, vn_ref, g_ref, o_ref, *, n, nseq, lam_init):
    dims = (((1,), (1,)), ((), ()))
    chains = [(b, slice(j * LANES, (j + 1) * LANES)) for b in range(nseq) for j in range(ATTN_WIDTH // LANES)]
    scores = []
    for b, cols in chains:
        qx = _expand_maps(q_ref[b, :, cols])
        scores.append((jnp.dot(qx, kc_ref[b, cols, :].astype(BF16), preferred_element_type=F32),
                       lax.dot_general(qx, kn_ref[b, :, cols].astype(BF16), dims, preferred_element_type=F32)))
    for (b, cols), (s_c, s_n) in zip(chains, scores):
        m = jnp.maximum(jnp.max(s_c, axis=-1, keepdims=True), jnp.max(s_n, axis=-1, keepdims=True))
        p_c = jnp.exp2(s_c - m)
        p_n = jnp.exp2(s_n - m)
        l = jnp.sum(p_c, axis=-1, keepdims=True) + jnp.sum(p_n, axis=-1, keepdims=True)
        acc = (lax.dot_general(p_c.astype(BF16), vc_ref[b, cols, :].astype(BF16), dims, preferred_element_type=F32)
               + jnp.dot(p_n.astype(BF16), vn_ref[b, :, cols].astype(BF16), preferred_element_type=F32))
        o_ref[b, :, cols] = _diff_combine(acc / l, lam_ref[0, 0], g_ref[...], n, lam_init).astype(BF16)


def _attn_sample(lam, q, cache_kt, cache_vt, k_new, v_new, g128, lam_init, nseq=2):
    bsz, n, _ = q.shape
    past = cache_kt.shape[2]
    assert past % CHUNK == 0 and n <= CHUNK
    cache_spec = pl.BlockSpec((nseq, ATTN_WIDTH, past), lambda b: (b, 0, 0))
    new_spec = pl.BlockSpec((nseq, n, ATTN_WIDTH), lambda b: (b, 0, 0))
    return pl.pallas_call(
        functools.partial(_attn_sample_kernel, n=n, nseq=nseq, lam_init=lam_init),
        grid=(bsz // nseq,),
        in_specs=[pl.BlockSpec(memory_space=pltpu.SMEM), new_spec, cache_spec, cache_spec, new_spec, new_spec,
                  pl.BlockSpec((1, LANES), lambda b: (0, 0))],
        out_specs=new_spec,
        out_shape=jax.ShapeDtypeStruct((bsz, n, ATTN_WIDTH), BF16),
        compiler_params=_params(("parallel",)),
        name="attn_sample",
    )(lam, q, cache_kt, cache_vt, k_new, v_new, g128)


def _ssm_kernel(u_ref, h0re_ref, h0im_ref, are_ref, aim_ref, bre_ref, bim_ref, cre_ref, cim_ref, d_ref,
                wglu_ref, bglu_ref, out_ref, hre_out, him_out, bure, buim, hsre, hsim, hre, him, res,
                *, tt):
    i = pl.program_id(1)
    dims = (((1,), (1,)), ((), ()))

    @pl.when(i == 0)
    def _():
        for b in range(SUBLANES):
            hre[b:b + 1, :] = _lane_concat_rows(h0re_ref[b])
            him[b:b + 1, :] = _lane_concat_rows(h0im_ref[b])

    ys = []
    for s in range(N_SLABS):
        us = u_ref[s].astype(BF16)
        cols = slice(s * SLAB_STATES, (s + 1) * SLAB_STATES)
        bure[:, cols] = jnp.dot(us, bre_ref[s], preferred_element_type=F32)
        buim[:, cols] = jnp.dot(us, bim_ref[s], preferred_element_type=F32)
        ar = jnp.broadcast_to(are_ref[:, cols], (SUBLANES, SLAB_STATES))
        ai = jnp.broadcast_to(aim_ref[:, cols], (SUBLANES, SLAB_STATES))
        hr = hre[:, cols]
        hi = him[:, cols]
        for t in range(tt):
            r = slice(t * SUBLANES, (t + 1) * SUBLANES)
            hr, hi = ar * hr - ai * hi + bure[r, cols], ar * hi + ai * hr + buim[r, cols]
            hsre[r, cols] = hr
            hsim[r, cols] = hi
        hre[:, cols] = hr
        him[:, cols] = hi
        ys.append(lax.dot_general(hsre[:, cols].astype(BF16), cre_ref[s], dims, preferred_element_type=F32)
                  - lax.dot_general(hsim[:, cols].astype(BF16), cim_ref[s], dims, preferred_element_type=F32)
                  + d_ref[:, s * LANES:(s + 1) * LANES] * u_ref[s])
    y = jnp.concatenate(ys, axis=1)
    z = jnp.dot(jax.nn.gelu(y).astype(BF16), wglu_ref[...], preferred_element_type=F32) + bglu_ref[...]
    gated = z[:, :SSM_WIDTH] * jax.nn.sigmoid(z[:, SSM_WIDTH:])
    for s in range(N_SLABS):
        res[s] = gated[:, s * LANES:(s + 1) * LANES]
    for b in range(SUBLANES):
        for s in range(N_SLABS):
            out_ref[b, :, s * LANES:(s + 1) * LANES] = res[s, pl.ds(b, tt, stride=SUBLANES), :].astype(BF16)

    @pl.when(i == pl.num_programs(1) - 1)
    def _():
        for b in range(SUBLANES):
            for g in range(N_SSM_GROUPS):
                lanes = slice(g * SSM_STATE, (g + 1) * SSM_STATE)
                hre_out[b, g:g + 1, :] = hre[b:b + 1, lanes]
                him_out[b, g:g + 1, :] = him[b:b + 1, lanes]


def _ssm(l, u_t, h0_re, h0_im, lb_re, lb_im, bre, bim, cre, cim, d, w_glu, b_glu, tt):
    nblk, _, rows_total, _ = u_t.shape
    seq = rows_total // SUBLANES
    rows = SUBLANES * tt
    grid = (nblk, seq // tt)
    const = lambda shape: pl.BlockSpec(shape, lambda b, i: (0,) * len(shape))
    layer = lambda *shape: pl.BlockSpec((None,) + shape, lambda b, i: (l,) + (0,) * len(shape))
    slab_spec = const((N_SLABS, LANES, SLAB_STATES))
    state_spec = pl.BlockSpec((SUBLANES, N_SSM_GROUPS, SSM_STATE), lambda b, i: (b, 0, 0))
    state_shape = jax.ShapeDtypeStruct((nblk * SUBLANES, N_SSM_GROUPS, SSM_STATE), F32)
    return pl.pallas_call(
        functools.partial(_ssm_kernel, tt=tt),
        grid=grid,
        in_specs=[pl.BlockSpec((None, N_SLABS, rows, LANES), lambda b, i: (b, 0, i, 0)),
                  state_spec, state_spec,
                  const((1, SSM_LANES)), const((1, SSM_LANES)),
                  slab_spec, slab_spec, slab_spec, slab_spec,
                  const((1, SSM_WIDTH)), const((SSM_WIDTH, 2 * SSM_WIDTH)), layer(1, 2 * SSM_WIDTH)],
        out_specs=(pl.BlockSpec((SUBLANES, tt, SSM_WIDTH), lambda b, i: (b, i, 0)), state_spec, state_spec),
        out_shape=(jax.ShapeDtypeStruct((nblk * SUBLANES, seq, SSM_WIDTH), BF16), state_shape, state_shape),
        scratch_shapes=[pltpu.VMEM((rows, SSM_LANES), F32) for _ in range(4)]
                       + [pltpu.VMEM((SUBLANES, SSM_LANES), F32) for _ in range(2)]
                       + [pltpu.VMEM((N_SLABS, rows, LANES), F32)],
        compiler_params=_params(("parallel", "arbitrary")),
        name="ssm",
    )(u_t, h0_re, h0_im, lb_re, lb_im, bre, bim, cre, cim, d, w_glu, b_glu)


def _layer_norm(x, g, b):
    mu = jnp.mean(x, axis=-1, keepdims=True)
    xc = x - mu
    var = jnp.mean(xc * xc, axis=-1, keepdims=True)
    return xc * lax.rsqrt(var + LN_EPS) * g + b


def _post_kernel(xp_ref, ap_ref, sp_ref, xs_ref, as_ref, ss_ref, wout_ref, g1_ref, b1_ref, wup_ref, wdown_ref,
                 g2_ref, b2_ref, op_ref, os_ref, *, tl, n_prompt, ff_chunk):
    is_prompt = pl.program_id(0) < n_prompt
    tile = functools.partial(_post_tile, wout_ref, g1_ref, b1_ref, wup_ref, wdown_ref, g2_ref, b2_ref,
                             tl=tl, ff_chunk=ff_chunk)

    @pl.when(is_prompt)
    def _():
        tile(xp_ref, ap_ref, sp_ref, op_ref)

    @pl.when(jnp.logical_not(is_prompt))
    def _():
        tile(xs_ref, as_ref, ss_ref, os_ref)


def _post_tile(wout_ref, g1_ref, b1_ref, wup_ref, wdown_ref, g2_ref, b2_ref, x_ref, a_ref, s_ref, o_ref,
               *, tl, ff_chunk):
    chains = (0, 1)
    n = tl // len(chains)
    views = [slice(c * n, (c + 1) * n) for c in chains]
    xs = [x_ref[v, :] for v in views]
    mix = [jnp.dot(jnp.concatenate([a_ref[v, :], s_ref[v, :]], axis=1), wout_ref[...],
                   preferred_element_type=F32) for v in views]
    x1 = [None, None]
    x1b = [None, None]
    h = [None, None]
    ff = [jnp.zeros((n, D_MODEL), F32) for _ in chains]
    n_chunks = D_FF // ff_chunk

    def up(c, k):
        return jnp.dot(x1b[c], wup_ref[:, k * ff_chunk:(k + 1) * ff_chunk], preferred_element_type=F32)

    def down(c, k):
        act = jnp.square(jnp.maximum(h[c], 0.0)).astype(BF16)
        return ff[c] + jnp.dot(act, wdown_ref[k * ff_chunk:(k + 1) * ff_chunk, :], preferred_element_type=F32)

    x1[0] = _layer_norm(DEEPNORM_ALPHA * xs[0] + mix[0], g1_ref[...], b1_ref[...])
    x1b[0] = x1[0].astype(BF16)
    h[0] = up(0, 0)
    x1[1] = _layer_norm(DEEPNORM_ALPHA * xs[1] + mix[1], g1_ref[...], b1_ref[...])
    x1b[1] = x1[1].astype(BF16)
    for k in range(n_chunks):
        h[1] = up(1, k)
        ff[0] = down(0, k)
        if k + 1 < n_chunks:
            h[0] = up(0, k + 1)
        ff[1] = down(1, k)
    for c in chains:
        o_ref[views[c], :] = _layer_norm(DEEPNORM_ALPHA * x1[c] + ff[c], g2_ref[...], b2_ref[...])


def _post(l, prompt, sample, w_out, g1, b1, w_up, w_down, g2, b2, tl):
    bp, seq, _ = prompt[0].shape
    bs, dec_seq, _ = sample[0].shape
    per_seq = seq // tl
    n_prompt = bp * per_seq
    n_sample = bs * dec_seq // tl
    sample = [a.reshape(n_sample, tl, a.shape[-1]) for a in sample]

    def p_spec(width):
        def index(s):
            t = jnp.minimum(s, n_prompt - 1)
            return (t // per_seq, t % per_seq, 0)
        return pl.BlockSpec((None, tl, width), index)

    s_spec = lambda width: pl.BlockSpec((None, tl, width), lambda s: (jnp.maximum(s - n_prompt, 0), 0, 0))
    const = lambda shape: pl.BlockSpec(shape, lambda s: (0,) * len(shape), pipeline_mode=pl.Buffered(1))
    layer = lambda *shape: pl.BlockSpec((None,) + shape, lambda s: (l,) + (0,) * len(shape))
    widths = (D_MODEL, ATTN_WIDTH, SSM_WIDTH)
    yp, ys = pl.pallas_call(
        functools.partial(_post_kernel, tl=tl, n_prompt=n_prompt, ff_chunk=1024),
        grid=(n_prompt + n_sample,),
        in_specs=[p_spec(w) for w in widths] + [s_spec(w) for w in widths]
                 + [const((D_MODEL, D_MODEL)), layer(1, D_MODEL), layer(1, D_MODEL),
                    const((D_MODEL, D_FF)), const((D_FF, D_MODEL)), layer(1, D_MODEL), layer(1, D_MODEL)],
        out_specs=(p_spec(D_MODEL), s_spec(D_MODEL)),
        out_shape=(jax.ShapeDtypeStruct((bp, seq, D_MODEL), F32),
                   jax.ShapeDtypeStruct((n_sample, tl, D_MODEL), F32)),
        compiler_params=_params(("arbitrary",)),
        name="post",
    )(*prompt, *sample, w_out, g1, b1, w_up, w_down, g2, b2)
    return yp, ys.reshape(bs, dec_seq, D_MODEL)


def kernel(x_prompt, x_sample, cache_k, cache_v, state_ssm_re, state_ssm_im, w_in, lambda_q1, lambda_k1,
           lambda_q2, lambda_k2, subln_g, ssm_a_re, ssm_a_im, ssm_log_dt, ssm_b_re, ssm_b_im, ssm_c_re,
           ssm_c_im, ssm_d, w_glu, b_glu, w_out, ln1_g, ln1_b, w_up, w_down, ln2_g, ln2_b):
    assert w_in.shape[0] == DEPTH
    l = 0
    lam_init = _lambda_init(l)
    bp, seq, _ = x_prompt.shape
    bs, dec_seq, _ = x_sample.shape
    past = cache_k.shape[2]

    lb_re, lb_im, bre, bim, cre, cim, d, g128, lam = _prep(
        l, ssm_a_re, ssm_a_im, ssm_log_dt, ssm_b_re, ssm_b_im, ssm_c_re, ssm_c_im, ssm_d, subln_g,
        lambda_q1, lambda_k1, lambda_q2, lambda_k2, lam_init)
    w_in_bf, w_glu_bf, w_out_bf, w_up_bf, w_down_bf = (w[l].astype(BF16) for w in (w_in, w_glu, w_out, w_up, w_down))
    row = lambda a: a.reshape(a.shape[0], 1, a.shape[1])
    b_glu3 = row(b_glu)
    ln = [row(a) for a in (ln1_g, ln1_b, ln2_g, ln2_b)]

    def mixers(after, x, positions, cache, h0_re, h0_im, nb, tl, tt):
        bsz, n, _ = x.shape
        q, k, v, u_t = _inproj(after, x, w_in_bf, _rope_tables(positions), nb, tl)
        if cache is None:
            attn = _attn_prompt(lam, q, k, v, g128, 512, lam_init)
        else:
            attn = _attn_sample(lam, q, cache[0], cache[1], k, v, g128, lam_init)
        ssm, h_re, h_im = _ssm(l, u_t, h0_re, h0_im, lb_re, lb_im, bre, bim, cre, cim, d, w_glu_bf, b_glu3, tt)
        shape_kv = (1, bsz, n, N_HEADS, HEAD_DIM)
        return attn, ssm, k.reshape(shape_kv), v.reshape(shape_kv), h_re[None], h_im[None]

    cache = tuple(jnp.transpose(c[l], (0, 2, 3, 1)).reshape(bs, ATTN_WIDTH, past) for c in (cache_k, cache_v))
    attn_s, ssm_s, ks, vs, rs, is_ = mixers(lam, x_sample, past + np.arange(dec_seq), cache, state_ssm_re[l],
                                            state_ssm_im[l], nb=SUBLANES, tl=dec_seq, tt=dec_seq)
    zeros = jnp.zeros((bp, N_SSM_GROUPS, SSM_STATE), F32)
    after = ssm_s[0, 0:1, 0:1].astype(F32)
    attn_p, ssm_p, kp, vp, rp, ip = mixers(after, x_prompt, np.arange(seq), None, zeros, zeros, nb=1, tl=512, tt=64)
    yp, ys = _post(l, (x_prompt, attn_p, ssm_p), (x_sample, attn_s, ssm_s), w_out_bf, ln[0], ln[1],
                   w_up_bf, w_down_bf, ln[2], ln[3], tl=512)
    return (yp, ys, kp, vp, rp, ip, ks, vs, rs, is_)
```

```python
import functools
import math

import numpy as np
import jax
import jax.numpy as jnp
from jax import lax
from jax.experimental import pallas as pl
from jax.experimental.pallas import tpu as pltpu

D_MODEL = 1024
DEPTH = 1
CHUNK = 64
ATTN_WIDTH = 512
SSM_WIDTH = 512
N_HEADS = 8
HEAD_DIM = 64
HALF_DIM = 32
ROT_DIM = 8
ROPE_THETA = 500000.0
SSM_GROUP = 16
N_SSM_GROUPS = 32
SSM_STATE = 64
D_FF = 4 * D_MODEL
LN_EPS = 1e-5
SUBLN_EPS = 1e-5
DEEPNORM_ALPHA = (2 * DEPTH) ** 0.25
QK_SCALE = HALF_DIM ** -0.5
LOG2_E = math.log2(math.e)

SUBLANES = 8
LANES = 128
SSM_LANES = N_SSM_GROUPS * SSM_STATE
GROUPS_PER_SLAB = LANES // SSM_GROUP
N_SLABS = SSM_WIDTH // LANES
SLAB_STATES = GROUPS_PER_SLAB * SSM_STATE
HEADS_PER_SLAB = LANES // HEAD_DIM
MAPS_PER_SLAB = LANES // HALF_DIM
VMEM_LIMIT_BYTES = 56 * 1024 * 1024

ROW_TILE = 512
ATTN_BLOCK = 512
SSM_TIME_TILE = 64
SAMPLE_SEQS_PER_STEP = 2
FF_CHUNK = 1024

F32 = jnp.float32
BF16 = jnp.bfloat16


def _lambda_init(layer_idx):
    return 0.8 - 0.6 * math.exp(-0.3 * layer_idx)


def _params(semantics):
    return pltpu.CompilerParams(dimension_semantics=semantics, vmem_limit_bytes=VMEM_LIMIT_BYTES)


def _lane_concat_rows(x):
    return jnp.concatenate([x[r:r + 1, :] for r in range(x.shape[0])], axis=1)


def _block_diag_slabs(blocks):
    slabs = []
    for s in range(N_SLABS):
        rows = []
        for gi in range(GROUPS_PER_SLAB):
            pieces = []
            if gi > 0:
                pieces.append(jnp.zeros((SSM_GROUP, gi * SSM_STATE), F32))
            pieces.append(blocks[s * GROUPS_PER_SLAB + gi])
            if gi < GROUPS_PER_SLAB - 1:
                pieces.append(jnp.zeros((SSM_GROUP, (GROUPS_PER_SLAB - 1 - gi) * SSM_STATE), F32))
            rows.append(jnp.concatenate(pieces, axis=1))
        slabs.append(jnp.concatenate(rows, axis=0))
    return slabs


def _prep_kernel(are_ref, aim_ref, logdt_ref, bre_ref, bim_ref, cre_ref, cim_ref, d_ref, g_ref,
                 lq1_ref, lk1_ref, lq2_ref, lk2_ref,
                 lbre_ref, lbim_ref, bbre_ref, bbim_ref, ccre_ref, ccim_ref, dd_ref, gg_ref, lam_ref,
                 *, lam_init):
    g = N_SSM_GROUPS
    eye = lax.broadcasted_iota(jnp.int32, (g, g), 0) == lax.broadcasted_iota(jnp.int32, (g, g), 1)
    logdt = jnp.sum(jnp.where(eye, jnp.broadcast_to(logdt_ref[...], (g, g)), 0.0), axis=-1, keepdims=True)
    dt = jnp.exp(logdt)
    ar = are_ref[...]
    ai = aim_ref[...]
    mag = jnp.exp(ar * dt)
    lb_re = mag * jnp.cos(ai * dt)
    lb_im = mag * jnp.sin(ai * dt)
    nr = lb_re - 1.0
    ni = lb_im
    den = ar * ar + ai * ai
    f_re = (nr * ar + ni * ai) / den
    f_im = (ni * ar - nr * ai) / den
    lbre_ref[...] = _lane_concat_rows(lb_re)
    lbim_ref[...] = _lane_concat_rows(lb_im)
    br = bre_ref[...]
    bi = bim_ref[...]
    bb_re = f_re[:, None, :] * br - f_im[:, None, :] * bi
    bb_im = f_re[:, None, :] * bi + f_im[:, None, :] * br
    for out_ref, blocks in ((bbre_ref, bb_re), (bbim_ref, bb_im), (ccre_ref, cre_ref[...]), (ccim_ref, cim_ref[...])):
        for s, slab in enumerate(_block_diag_slabs(blocks)):
            out_ref[s] = slab.astype(BF16)
    dd_ref[...] = _lane_concat_rows(d_ref[...])
    gg_ref[...] = jnp.concatenate([g_ref[...]] * HEADS_PER_SLAB, axis=1)
    s1 = jnp.sum(lq1_ref[...] * lk1_ref[...], axis=-1, keepdims=True)
    s2 = jnp.sum(lq2_ref[...] * lk2_ref[...], axis=-1, keepdims=True)
    lam_ref[...] = jnp.exp(s1) - jnp.exp(s2) + lam_init


def _prep(l, a_re, a_im, log_dt, b_re, b_im, c_re, c_im, d, subln_g, lq1, lk1, lq2, lk2, lam_init):
    g, p, c = N_SSM_GROUPS, SSM_STATE, SSM_GROUP
    layer = lambda *shape: pl.BlockSpec((None,) + shape, lambda: (l,) + (0,) * len(shape))
    row = lambda a: a.reshape(a.shape[0], 1, a.shape[1])
    slab = jax.ShapeDtypeStruct((N_SLABS, LANES, SLAB_STATES), BF16)
    out_shape = (jax.ShapeDtypeStruct((1, SSM_LANES), F32), jax.ShapeDtypeStruct((1, SSM_LANES), F32),
                 slab, slab, slab, slab,
                 jax.ShapeDtypeStruct((1, SSM_WIDTH), F32), jax.ShapeDtypeStruct((1, LANES), F32),
                 jax.ShapeDtypeStruct((1, 1), F32))
    return pl.pallas_call(
        functools.partial(_prep_kernel, lam_init=lam_init),
        in_specs=[layer(g, p), layer(g, p), layer(1, g), layer(g, c, p), layer(g, c, p), layer(g, c, p),
                  layer(g, c, p), layer(g, c), layer(1, HEAD_DIM)] + [layer(1, HALF_DIM)] * 4,
        out_shape=out_shape, name="prep",
    )(a_re, a_im, row(log_dt), jnp.swapaxes(b_re, 2, 3), jnp.swapaxes(b_im, 2, 3), c_re, c_im, d,
      row(subln_g), row(lq1), row(lk1), row(lq2), row(lk2))


def _rope_tables(positions):
    inv = ROPE_THETA ** (-np.arange(0, ROT_DIM, 2, dtype=np.float64) / ROT_DIM)
    ang = np.asarray(positions, np.float64)[:, None] * inv[None, :]
    r = np.arange(LANES) % HALF_DIM
    half = ROT_DIM // 2
    idx = r % half
    cos = np.where(r[None, :] < ROT_DIM, np.cos(ang)[:, idx], 1.0)
    sin = np.sin(ang)[:, idx]
    s_up = np.where(r[None, :] < half, -sin, 0.0)
    s_dn = np.where((r[None, :] >= half) & (r[None, :] < ROT_DIM), sin, 0.0)
    return (jnp.asarray(cos, F32), jnp.asarray(s_up, F32), jnp.asarray(s_dn, F32))


def _inproj_kernel(after_ref, x_ref, w_ref, cos_ref, sup_ref, sdn_ref, q_ref, k_ref, v_ref, u_ref, *, nb, tl):
    del after_ref
    half = ROT_DIM // 2
    x = x_ref[...].reshape(nb * tl, D_MODEL).astype(BF16)
    cos = jnp.concatenate([cos_ref[...]] * nb, axis=0)
    sup = jnp.concatenate([sup_ref[...]] * nb, axis=0)
    sdn = jnp.concatenate([sdn_ref[...]] * nb, axis=0)

    def rope(t):
        slabs = []
        for j in range(ATTN_WIDTH // LANES):
            s = t[:, j * LANES:(j + 1) * LANES]
            slabs.append(s * cos + pltpu.roll(s, LANES - half, 1) * sup + pltpu.roll(s, half, 1) * sdn)
        return jnp.concatenate(slabs, axis=1)

    q = jnp.dot(x, w_ref[:, 0:ATTN_WIDTH], preferred_element_type=F32)
    q_ref[...] = (rope(q) * (QK_SCALE * LOG2_E)).astype(BF16).reshape(nb, tl, ATTN_WIDTH)
    k = jnp.dot(x, w_ref[:, ATTN_WIDTH:2 * ATTN_WIDTH], preferred_element_type=F32)
    k_ref[...] = rope(k).reshape(nb, tl, ATTN_WIDTH)
    v = jnp.dot(x, w_ref[:, 2 * ATTN_WIDTH:3 * ATTN_WIDTH], preferred_element_type=F32)
    v_ref[...] = v.reshape(nb, tl, ATTN_WIDTH)
    u = jnp.dot(x, w_ref[:, 3 * ATTN_WIDTH:], preferred_element_type=F32)
    first_slot = (pl.program_id(1) * nb) % SUBLANES
    for b in range(nb):
        for s in range(N_SLABS):
            u_ref[s, pl.ds(first_slot + b, tl, stride=SUBLANES), :] = u[b * tl:(b + 1) * tl,
                                                                        s * LANES:(s + 1) * LANES]


def _inproj(after, x, w_bf, tables, nb, tl):
    bsz, seq, _ = x.shape
    nblk = bsz // SUBLANES
    per_blk = SUBLANES // nb
    grid = (seq // tl, bsz // nb)
    tab_spec = pl.BlockSpec((tl, LANES), lambda i, b: (i, 0))
    act_spec = lambda width: pl.BlockSpec((nb, tl, width), lambda i, b: (b, i, 0))
    out_shape = (jax.ShapeDtypeStruct((bsz, seq, ATTN_WIDTH), BF16),
                 jax.ShapeDtypeStruct((bsz, seq, ATTN_WIDTH), F32),
                 jax.ShapeDtypeStruct((bsz, seq, ATTN_WIDTH), F32),
                 jax.ShapeDtypeStruct((nblk, N_SLABS, seq * SUBLANES, LANES), F32))
    return pl.pallas_call(
        functools.partial(_inproj_kernel, nb=nb, tl=tl),
        grid=grid,
        in_specs=[pl.BlockSpec(memory_space=pltpu.SMEM), act_spec(D_MODEL),
                  pl.BlockSpec((D_MODEL, 4 * ATTN_WIDTH), lambda i, b: (0, 0)),
                  tab_spec, tab_spec, tab_spec],
        out_specs=(act_spec(ATTN_WIDTH), act_spec(ATTN_WIDTH), act_spec(ATTN_WIDTH),
                   pl.BlockSpec((None, N_SLABS, tl * SUBLANES, LANES), lambda i, b: (b // per_blk, 0, i, 0))),
        out_shape=out_shape,
        compiler_params=_params(("parallel", "arbitrary")),
        name="inproj",
    )(after, x, w_bf, *tables)


def _expand_maps(q):
    lane = lax.broadcasted_iota(jnp.int32, q.shape, 1)
    zero = jnp.zeros_like(q)
    return jnp.concatenate(
        [jnp.where((lane >= m * HALF_DIM) & (lane < (m + 1) * HALF_DIM), q, zero) for m in range(MAPS_PER_SLAB)],
        axis=0)


def _diff_combine(o, lam, g, n, lam_init):
    lane = lax.broadcasted_iota(jnp.int32, (n, LANES), 1)
    first = lane < HEAD_DIM
    d = jnp.where(first, o[0:n] - lam * o[n:2 * n], o[2 * n:3 * n] - lam * o[3 * n:4 * n])
    sq = d * d
    ss_a = jnp.sum(jnp.where(first, sq, 0.0), axis=-1, keepdims=True)
    ss_b = jnp.sum(jnp.where(first, 0.0, sq), axis=-1, keepdims=True)
    ms = jnp.where(first, ss_a, ss_b) * (1.0 / HEAD_DIM)
    return d * lax.rsqrt(ms + SUBLN_EPS) * g * (1.0 - lam_init)


def _attn_prompt_kernel(lam_ref, q_ref, k_ref, v_ref, g_ref, o_ref, kb, vb, qx_a, qx_b, s_a, s_b, m_ref, acc_ref,
                        *, blk, lam_init):
    seq = q_ref.shape[0]
    hb = blk // 2
    half = MAPS_PER_SLAB * hb
    rows = 2 * half
    dims = (((1,), (1,)), ((), ()))

    kb[...] = k_ref[...].astype(BF16)
    vb[:, 0:LANES] = v_ref[...].astype(BF16)
    vb[:, LANES:2 * LANES] = jnp.ones(v_ref.shape, BF16)

    steps = [(i, j) for i in range(seq // blk) for j in range(i + 1)]
    qx_bufs = (qx_a, qx_b)
    s_bufs = (s_a, s_b)

    def expand_queries(i):
        qx = qx_bufs[i % 2]
        qx[0:half, :] = _expand_maps(q_ref[i * blk:i * blk + hb, :])
        qx[half:rows, :] = _expand_maps(q_ref[i * blk + hb:(i + 1) * blk, :])

    def scores(n):
        i, j = steps[n]
        s_bufs[n % 2][...] = lax.dot_general(qx_bufs[i % 2][...], kb[j * blk:(j + 1) * blk, :], dims,
                                             preferred_element_type=F32)

    def update(j, src, h, width, row_offset):
        rs = slice(h * half, (h + 1) * half)
        s = src[rs, 0:width]
        if row_offset is not None:
            r = lax.broadcasted_iota(jnp.int32, (half, width), 0)
            c = lax.broadcasted_iota(jnp.int32, (half, width), 1)
            s = jnp.where((row_offset + r % hb) // CHUNK >= c // CHUNK, s, -jnp.inf)
        m_prev = m_ref[rs, :]
        m_new = jnp.maximum(m_prev, jnp.max(s, axis=-1, keepdims=True))
        alpha = jnp.exp2(m_prev - m_new)
        p = jnp.exp2(s - jnp.concatenate([m_new] * (width // LANES), axis=1)).astype(BF16)
        pv = jnp.dot(p, vb[j * blk:j * blk + width, :], preferred_element_type=F32)
        acc_ref[rs, :] = jnp.concatenate([alpha, alpha], axis=1) * acc_ref[rs, :] + pv
        m_ref[rs, :] = m_new

    expand_queries(0)
    scores(0)
    for n, (i, j) in enumerate(steps):
        if n + 1 < len(steps):
            if steps[n + 1][1] == 0:
                expand_queries(steps[n + 1][0])
            scores(n + 1)
        if j == 0:
            m_ref[...] = jnp.full((rows, LANES), -jnp.inf, F32)
            acc_ref[...] = jnp.zeros((rows, 2 * LANES), F32)
        src = s_bufs[n % 2]
        if j < i:
            update(j, src, 0, blk, None)
            update(j, src, 1, blk, None)
        else:
            update(j, src, 0, hb, 0)
            update(j, src, 1, blk, hb)
            for h in range(2):
                acc = acc_ref[h * half:(h + 1) * half, :]
                o = acc[:, 0:LANES] / acc[:, LANES:2 * LANES]
                o_ref[i * blk + h * hb:i * blk + (h + 1) * hb, :] = _diff_combine(
                    o, lam_ref[0, 0], g_ref[...], hb, lam_init).astype(BF16)


def _attn_prompt(lam, q, k, v, g128, blk, lam_init):
    bsz, seq, _ = q.shape
    rows = MAPS_PER_SLAB * blk
    slab_spec = pl.BlockSpec((None, seq, LANES), lambda b, j: (b, 0, j))
    return pl.pallas_call(
        functools.partial(_attn_prompt_kernel, blk=blk, lam_init=lam_init),
        grid=(bsz, ATTN_WIDTH // LANES),
        in_specs=[pl.BlockSpec(memory_space=pltpu.SMEM), slab_spec, slab_spec, slab_spec,
                  pl.BlockSpec((1, LANES), lambda b, j: (0, 0))],
        out_specs=slab_spec,
        out_shape=jax.ShapeDtypeStruct((bsz, seq, ATTN_WIDTH), BF16),
        scratch_shapes=[pltpu.VMEM((seq, LANES), BF16), pltpu.VMEM((seq, 2 * LANES), BF16),
                        pltpu.VMEM((rows, LANES), BF16), pltpu.VMEM((rows, LANES), BF16),
                        pltpu.VMEM((rows, blk), F32), pltpu.VMEM((rows, blk), F32),
                        pltpu.VMEM((rows, LANES), F32), pltpu.VMEM((rows, 2 * LANES), F32)],
        compiler_params=_params(("parallel", "parallel")),
        name="attn_prompt",
    )(lam, q, k, v, g128)


def _attn_sample_kernel(lam_ref, q_ref, kc_ref, vc_ref, kn_ref, vn_ref, g_ref, o_ref, *, n, nseq, lam_init):
    dims = (((1,), (1,)), ((), ()))
    chains = [(b, slice(j * LANES, (j + 1) * LANES)) for b in range(nseq) for j in range(ATTN_WIDTH // LANES)]
    scores = []
    for b, cols in chains:
        qx = _expand_maps(q_ref[b, :, cols])
        scores.append((jnp.dot(qx, kc_ref[b, cols, :].astype(BF16), preferred_element_type=F32),
                       lax.dot_general(qx, kn_ref[b, :, cols].astype(BF16), dims, preferred_element_type=F32)))
    for (b, cols), (s_c, s_n) in zip(chains, scores):
        m = jnp.maximum(jnp.max(s_c, axis=-1, keepdims=True), jnp.max(s_n, axis=-1, keepdims=True))
        p_c = jnp.exp2(s_c - m)
        p_n = jnp.exp2(s_n - m)
        l = jnp.sum(p_c, axis=-1, keepdims=True) + jnp.sum(p_n, axis=-1, keepdims=True)
        acc = (lax.dot_general(p_c.astype(BF16), vc_ref[b, cols, :].astype(BF16), dims, preferred_element_type=F32)
               + jnp.dot(p_n.astype(BF16), vn_ref[b, :, cols].astype(BF16), preferred_element_type=F32))
        o_ref[b, :, cols] = _diff_combine(acc / l, lam_ref[0, 0], g_ref[...], n, lam_init).astype(BF16)


def _attn_sample(lam, q, cache_kt, cache_vt, k_new, v_new, g128, lam_init, nseq=SAMPLE_SEQS_PER_STEP):
    bsz, n, _ = q.shape
    past = cache_kt.shape[2]
    assert past % CHUNK == 0 and n <= CHUNK
    cache_spec = pl.BlockSpec((nseq, ATTN_WIDTH, past), lambda b: (b, 0, 0))
    new_spec = pl.BlockSpec((nseq, n, ATTN_WIDTH), lambda b: (b, 0, 0))
    return pl.pallas_call(
        functools.partial(_attn_sample_kernel, n=n, nseq=nseq, lam_init=lam_init),
        grid=(bsz // nseq,),
        in_specs=[pl.BlockSpec(memory_space=pltpu.SMEM), new_spec, cache_spec, cache_spec, new_spec, new_spec,
                  pl.BlockSpec((1, LANES), lambda b: (0, 0))],
        out_specs=new_spec,
        out_shape=jax.ShapeDtypeStruct((bsz, n, ATTN_WIDTH), BF16),
        compiler_params=_params(("parallel",)),
        name="attn_sample",
    )(lam, q, cache_kt, cache_vt, k_new, v_new, g128)


def _ssm_kernel(u_ref, h0re_ref, h0im_ref, are_ref, aim_ref, bre_ref, bim_ref, cre_ref, cim_ref, d_ref,
                wglu_ref, bglu_ref, out_ref, hre_out, him_out, bure, buim, hsre, hsim, hre, him, res,
                *, tt):
    i = pl.program_id(1)
    dims = (((1,), (1,)), ((), ()))

    @pl.when(i == 0)
    def _():
        for b in range(SUBLANES):
            hre[b:b + 1, :] = _lane_concat_rows(h0re_ref[b])
            him[b:b + 1, :] = _lane_concat_rows(h0im_ref[b])

    ys = []
    for s in range(N_SLABS):
        us = u_ref[s].astype(BF16)
        cols = slice(s * SLAB_STATES, (s + 1) * SLAB_STATES)
        bure[:, cols] = jnp.dot(us, bre_ref[s], preferred_element_type=F32)
        buim[:, cols] = jnp.dot(us, bim_ref[s], preferred_element_type=F32)
        ar = jnp.broadcast_to(are_ref[:, cols], (SUBLANES, SLAB_STATES))
        ai = jnp.broadcast_to(aim_ref[:, cols], (SUBLANES, SLAB_STATES))
        hr = hre[:, cols]
        hi = him[:, cols]
        for t in range(tt):
            r = slice(t * SUBLANES, (t + 1) * SUBLANES)
            hr, hi = ar * hr - ai * hi + bure[r, cols], ar * hi + ai * hr + buim[r, cols]
            hsre[r, cols] = hr
            hsim[r, cols] = hi
        hre[:, cols] = hr
        him[:, cols] = hi
        ys.append(lax.dot_general(hsre[:, cols].astype(BF16), cre_ref[s], dims, preferred_element_type=F32)
                  - lax.dot_general(hsim[:, cols].astype(BF16), cim_ref[s], dims, preferred_element_type=F32)
                  + d_ref[:, s * LANES:(s + 1) * LANES] * u_ref[s])
    y = jnp.concatenate(ys, axis=1)
    z = jnp.dot(jax.nn.gelu(y).astype(BF16), wglu_ref[...], preferred_element_type=F32) + bglu_ref[...]
    gated = z[:, :SSM_WIDTH] * jax.nn.sigmoid(z[:, SSM_WIDTH:])
    for s in range(N_SLABS):
        res[s] = gated[:, s * LANES:(s + 1) * LANES]
    for b in range(SUBLANES):
        for s in range(N_SLABS):
            out_ref[b, :, s * LANES:(s + 1) * LANES] = res[s, pl.ds(b, tt, stride=SUBLANES), :].astype(BF16)

    @pl.when(i == pl.num_programs(1) - 1)
    def _():
        for b in range(SUBLANES):
            for g in range(N_SSM_GROUPS):
                lanes = slice(g * SSM_STATE, (g + 1) * SSM_STATE)
                hre_out[b, g:g + 1, :] = hre[b:b + 1, lanes]
                him_out[b, g:g + 1, :] = him[b:b + 1, lanes]


def _ssm(l, u_t, h0_re, h0_im, lb_re, lb_im, bre, bim, cre, cim, d, w_glu, b_glu, tt):
    nblk, _, rows_total, _ = u_t.shape
    seq = rows_total // SUBLANES
    rows = SUBLANES * tt
    grid = (nblk, seq // tt)
    const = lambda shape: pl.BlockSpec(shape, lambda b, i: (0,) * len(shape))
    layer = lambda *shape: pl.BlockSpec((None,) + shape, lambda b, i: (l,) + (0,) * len(shape))
    slab_spec = const((N_SLABS, LANES, SLAB_STATES))
    state_spec = pl.BlockSpec((SUBLANES, N_SSM_GROUPS, SSM_STATE), lambda b, i: (b, 0, 0))
    state_shape = jax.ShapeDtypeStruct((nblk * SUBLANES, N_SSM_GROUPS, SSM_STATE), F32)
    return pl.pallas_call(
        functools.partial(_ssm_kernel, tt=tt),
        grid=grid,
        in_specs=[pl.BlockSpec((None, N_SLABS, rows, LANES), lambda b, i: (b, 0, i, 0)),
                  state_spec, state_spec,
                  const((1, SSM_LANES)), const((1, SSM_LANES)),
                  slab_spec, slab_spec, slab_spec, slab_spec,
                  const((1, SSM_WIDTH)), const((SSM_WIDTH, 2 * SSM_WIDTH)), layer(1, 2 * SSM_WIDTH)],
        out_specs=(pl.BlockSpec((SUBLANES, tt, SSM_WIDTH), lambda b, i: (b, i, 0)), state_spec, state_spec),
        out_shape=(jax.ShapeDtypeStruct((nblk * SUBLANES, seq, SSM_WIDTH), BF16), state_shape, state_shape),
        scratch_shapes=[pltpu.VMEM((rows, SSM_LANES), F32) for _ in range(4)]
                       + [pltpu.VMEM((SUBLANES, SSM_LANES), F32) for _ in range(2)]
                       + [pltpu.VMEM((N_SLABS, rows, LANES), F32)],
        compiler_params=_params(("parallel", "arbitrary")),
        name="ssm",
    )(u_t, h0_re, h0_im, lb_re, lb_im, bre, bim, cre, cim, d, w_glu, b_glu)


def _layer_norm(x, g, b):
    mu = jnp.mean(x, axis=-1, keepdims=True)
    xc = x - mu
    var = jnp.mean(xc * xc, axis=-1, keepdims=True)
    return xc * lax.rsqrt(var + LN_EPS) * g + b


def _post_kernel(xp_ref, ap_ref, sp_ref, xs_ref, as_ref, ss_ref, wout_ref, g1_ref, b1_ref, wup_ref, wdown_ref,
                 g2_ref, b2_ref, op_ref, os_ref, *, tl, n_prompt, ff_chunk):
    is_prompt = pl.program_id(0) < n_prompt
    tile = functools.partial(_post_tile, wout_ref, g1_ref, b1_ref, wup_ref, wdown_ref, g2_ref, b2_ref,
                             tl=tl, ff_chunk=ff_chunk)

    @pl.when(is_prompt)
    def _():
        tile(xp_ref, ap_ref, sp_ref, op_ref)

    @pl.when(jnp.logical_not(is_prompt))
    def _():
        tile(xs_ref, as_ref, ss_ref, os_ref)


def _post_tile(wout_ref, g1_ref, b1_ref, wup_ref, wdown_ref, g2_ref, b2_ref, x_ref, a_ref, s_ref, o_ref,
               *, tl, ff_chunk):
    chains = (0, 1)
    n = tl // len(chains)
    views = [slice(c * n, (c + 1) * n) for c in chains]
    xs = [x_ref[v, :] for v in views]
    mix = [jnp.dot(jnp.concatenate([a_ref[v, :], s_ref[v, :]], axis=1), wout_ref[...],
                   preferred_element_type=F32) for v in views]
    x1 = [None, None]
    x1b = [None, None]
    h = [None, None]
    ff = [jnp.zeros((n, D_MODEL), F32) for _ in chains]
    n_chunks = D_FF // ff_chunk

    def up(c, k):
        return jnp.dot(x1b[c], wup_ref[:, k * ff_chunk:(k + 1) * ff_chunk], preferred_element_type=F32)

    def down(c, k):
        act = jnp.square(jnp.maximum(h[c], 0.0)).astype(BF16)
        return ff[c] + jnp.dot(act, wdown_ref[k * ff_chunk:(k + 1) * ff_chunk, :], preferred_element_type=F32)

    x1[0] = _layer_norm(DEEPNORM_ALPHA * xs[0] + mix[0], g1_ref[...], b1_ref[...])
    x1b[0] = x1[0].astype(BF16)
    h[0] = up(0, 0)
    x1[1] = _layer_norm(DEEPNORM_ALPHA * xs[1] + mix[1], g1_ref[...], b1_ref[...])
    x1b[1] = x1[1].astype(BF16)
    for k in range(n_chunks):
        h[1] = up(1, k)
        ff[0] = down(0, k)
        if k + 1 < n_chunks:
            h[0] = up(0, k + 1)
        ff[1] = down(1, k)
    for c in chains:
        o_ref[views[c], :] = _layer_norm(DEEPNORM_ALPHA * x1[c] + ff[c], g2_ref[...], b2_ref[...])


def _post(l, prompt, sample, w_out, g1, b1, w_up, w_down, g2, b2, tl):
    bp, seq, _ = prompt[0].shape
    bs, dec_seq, _ = sample[0].shape
    per_seq = seq // tl
    n_prompt = bp * per_seq
    n_sample = bs * dec_seq // tl
    sample = [a.reshape(n_sample, tl, a.shape[-1]) for a in sample]

    def p_spec(width):
        def index(s):
            t = jnp.minimum(s, n_prompt - 1)
            return (t // per_seq, t % per_seq, 0)
        return pl.BlockSpec((None, tl, width), index)

    s_spec = lambda width: pl.BlockSpec((None, tl, width), lambda s: (jnp.maximum(s - n_prompt, 0), 0, 0))
    const = lambda shape: pl.BlockSpec(shape, lambda s: (0,) * len(shape), pipeline_mode=pl.Buffered(1))
    layer = lambda *shape: pl.BlockSpec((None,) + shape, lambda s: (l,) + (0,) * len(shape))
    widths = (D_MODEL, ATTN_WIDTH, SSM_WIDTH)
    yp, ys = pl.pallas_call(
        functools.partial(_post_kernel, tl=tl, n_prompt=n_prompt, ff_chunk=FF_CHUNK),
        grid=(n_prompt + n_sample,),
        in_specs=[p_spec(w) for w in widths] + [s_spec(w) for w in widths]
                 + [const((D_MODEL, D_MODEL)), layer(1, D_MODEL), layer(1, D_MODEL),
                    const((D_MODEL, D_FF)), const((D_FF, D_MODEL)), layer(1, D_MODEL), layer(1, D_MODEL)],
        out_specs=(p_spec(D_MODEL), s_spec(D_MODEL)),
        out_shape=(jax.ShapeDtypeStruct((bp, seq, D_MODEL), F32),
                   jax.ShapeDtypeStruct((n_sample, tl, D_MODEL), F32)),
        compiler_params=_params(("arbitrary",)),
        name="post",
    )(*prompt, *sample, w_out, g1, b1, w_up, w_down, g2, b2)
    return yp, ys.reshape(bs, dec_seq, D_MODEL)


def kernel(x_prompt, x_sample, cache_k, cache_v, state_ssm_re, state_ssm_im, w_in, lambda_q1, lambda_k1,
           lambda_q2, lambda_k2, subln_g, ssm_a_re, ssm_a_im, ssm_log_dt, ssm_b_re, ssm_b_im, ssm_c_re,
           ssm_c_im, ssm_d, w_glu, b_glu, w_out, ln1_g, ln1_b, w_up, w_down, ln2_g, ln2_b):
    assert w_in.shape[0] == DEPTH
    l = 0
    lam_init = _lambda_init(l)
    bp, seq, _ = x_prompt.shape
    bs, dec_seq, _ = x_sample.shape
    past = cache_k.shape[2]
    assert bp % SUBLANES == 0 and bs % SUBLANES == 0 and bs % SAMPLE_SEQS_PER_STEP == 0
    assert seq % ROW_TILE == 0 and seq % ATTN_BLOCK == 0 and SUBLANES * dec_seq == ROW_TILE

    lb_re, lb_im, bre, bim, cre, cim, d, g128, lam = _prep(
        l, ssm_a_re, ssm_a_im, ssm_log_dt, ssm_b_re, ssm_b_im, ssm_c_re, ssm_c_im, ssm_d, subln_g,
        lambda_q1, lambda_k1, lambda_q2, lambda_k2, lam_init)
    w_in_bf, w_glu_bf, w_out_bf, w_up_bf, w_down_bf = (w[l].astype(BF16) for w in (w_in, w_glu, w_out, w_up, w_down))
    row = lambda a: a.reshape(a.shape[0], 1, a.shape[1])
    b_glu3 = row(b_glu)
    ln = [row(a) for a in (ln1_g, ln1_b, ln2_g, ln2_b)]

    def mixers(after, x, positions, cache, h0_re, h0_im, nb, tl, tt):
        bsz, n, _ = x.shape
        q, k, v, u_t = _inproj(after, x, w_in_bf, _rope_tables(positions), nb, tl)
        if cache is None:
            attn = _attn_prompt(lam, q, k, v, g128, ATTN_BLOCK, lam_init)
        else:
            attn = _attn_sample(lam, q, cache[0], cache[1], k, v, g128, lam_init)
        ssm, h_re, h_im = _ssm(l, u_t, h0_re, h0_im, lb_re, lb_im, bre, bim, cre, cim, d, w_glu_bf, b_glu3, tt)
        shape_kv = (1, bsz, n, N_HEADS, HEAD_DIM)
        return attn, ssm, k.reshape(shape_kv), v.reshape(shape_kv), h_re[None], h_im[None]

    cache = tuple(jnp.transpose(c[l], (0, 2, 3, 1)).reshape(bs, ATTN_WIDTH, past) for c in (cache_k, cache_v))
    attn_s, ssm_s, ks, vs, rs, is_ = mixers(lam, x_sample, past + np.arange(dec_seq), cache, state_ssm_re[l],
                                            state_ssm_im[l], nb=SUBLANES, tl=dec_seq, tt=dec_seq)
    zeros = jnp.zeros((bp, N_SSM_GROUPS, SSM_STATE), F32)
    after = ssm_s[0, 0:1, 0:1].astype(F32)
    attn_p, ssm_p, kp, vp, rp, ip = mixers(after, x_prompt, np.arange(seq), None, zeros, zeros,
                                           nb=1, tl=ROW_TILE, tt=SSM_TIME_TILE)
    yp, ys = _post(l, (x_prompt, attn_p, ssm_p), (x_sample, attn_s, ssm_s), w_out_bf, ln[0], ln[1],
                   w_up_bf, w_down_bf, ln[2], ln[3], tl=ROW_TILE)
    return (yp, ys, kp, vp, rp, ip, ks, vs, rs, is_)
```

```python
import functools
import math

import numpy as np
import jax
import jax.numpy as jnp
from jax import lax
from jax.experimental import pallas as pl
from jax.experimental.pallas import tpu as pltpu

D_MODEL = 1024
DEPTH = 1
CHUNK = 64
ATTN_WIDTH = 512
SSM_WIDTH = 512
N_HEADS = 8
HEAD_DIM = 64
HALF_DIM = 32
ROT_DIM = 8
ROPE_THETA = 500000.0
SSM_GROUP = 16
N_SSM_GROUPS = 32
SSM_STATE = 64
D_FF = 4 * D_MODEL
LN_EPS = 1e-5
SUBLN_EPS = 1e-5
DEEPNORM_ALPHA = (2 * DEPTH) ** 0.25
QK_SCALE = HALF_DIM ** -0.5
LOG2_E = math.log2(math.e)

SUBLANES = 8
LANES = 128
SSM_LANES = N_SSM_GROUPS * SSM_STATE
GROUPS_PER_SLAB = LANES // SSM_GROUP
N_SLABS = SSM_WIDTH // LANES
SLAB_STATES = GROUPS_PER_SLAB * SSM_STATE
HEADS_PER_SLAB = LANES // HEAD_DIM
MAPS_PER_SLAB = LANES // HALF_DIM
VMEM_LIMIT_BYTES = 56 * 1024 * 1024

ROW_TILE = 512
ATTN_BLOCK = 512
SSM_TIME_TILE = 64
SAMPLE_SEQS_PER_STEP = 2
FF_CHUNK = 1024

F32 = jnp.float32
BF16 = jnp.bfloat16


def _lambda_init(layer_idx):
    return 0.8 - 0.6 * math.exp(-0.3 * layer_idx)


def _params(semantics):
    return pltpu.CompilerParams(dimension_semantics=semantics, vmem_limit_bytes=VMEM_LIMIT_BYTES)


def _lane_concat_rows(x):
    return jnp.concatenate([x[r:r + 1, :] for r in range(x.shape[0])], axis=1)


def _block_diag_slabs(blocks):
    slabs = []
    for s in range(N_SLABS):
        rows = []
        for gi in range(GROUPS_PER_SLAB):
            pieces = []
            if gi > 0:
                pieces.append(jnp.zeros((SSM_GROUP, gi * SSM_STATE), F32))
            pieces.append(blocks[s * GROUPS_PER_SLAB + gi])
            if gi < GROUPS_PER_SLAB - 1:
                pieces.append(jnp.zeros((SSM_GROUP, (GROUPS_PER_SLAB - 1 - gi) * SSM_STATE), F32))
            rows.append(jnp.concatenate(pieces, axis=1))
        slabs.append(jnp.concatenate(rows, axis=0))
    return slabs


def _prep_kernel(are_ref, aim_ref, logdt_ref, bre_ref, bim_ref, cre_ref, cim_ref, d_ref, g_ref,
                 lq1_ref, lk1_ref, lq2_ref, lk2_ref,
                 lbre_ref, lbim_ref, bbre_ref, bbim_ref, ccre_ref, ccim_ref, dd_ref, gg_ref, lam_ref,
                 *, lam_init):
    g = N_SSM_GROUPS
    eye = lax.broadcasted_iota(jnp.int32, (g, g), 0) == lax.broadcasted_iota(jnp.int32, (g, g), 1)
    logdt = jnp.sum(jnp.where(eye, jnp.broadcast_to(logdt_ref[...], (g, g)), 0.0), axis=-1, keepdims=True)
    dt = jnp.exp(logdt)
    ar = are_ref[...]
    ai = aim_ref[...]
    mag = jnp.exp(ar * dt)
    lb_re = mag * jnp.cos(ai * dt)
    lb_im = mag * jnp.sin(ai * dt)
    nr = lb_re - 1.0
    ni = lb_im
    den = ar * ar + ai * ai
    f_re = (nr * ar + ni * ai) / den
    f_im = (ni * ar - nr * ai) / den
    lbre_ref[...] = _lane_concat_rows(lb_re)
    lbim_ref[...] = _lane_concat_rows(lb_im)
    br = bre_ref[...]
    bi = bim_ref[...]
    bb_re = f_re[:, None, :] * br - f_im[:, None, :] * bi
    bb_im = f_re[:, None, :] * bi + f_im[:, None, :] * br
    for out_ref, blocks in ((bbre_ref, bb_re), (bbim_ref, bb_im), (ccre_ref, cre_ref[...]), (ccim_ref, cim_ref[...])):
        for s, slab in enumerate(_block_diag_slabs(blocks)):
            out_ref[s] = slab.astype(BF16)
    dd_ref[...] = _lane_concat_rows(d_ref[...])
    gg_ref[...] = jnp.concatenate([g_ref[...]] * HEADS_PER_SLAB, axis=1)
    s1 = jnp.sum(lq1_ref[...] * lk1_ref[...], axis=-1, keepdims=True)
    s2 = jnp.sum(lq2_ref[...] * lk2_ref[...], axis=-1, keepdims=True)
    lam_ref[...] = jnp.exp(s1) - jnp.exp(s2) + lam_init


def _prep(l, a_re, a_im, log_dt, b_re, b_im, c_re, c_im, d, subln_g, lq1, lk1, lq2, lk2, lam_init):
    g, p, c = N_SSM_GROUPS, SSM_STATE, SSM_GROUP
    layer = lambda *shape: pl.BlockSpec((None,) + shape, lambda: (l,) + (0,) * len(shape))
    row = lambda a: a.reshape(a.shape[0], 1, a.shape[1])
    slab = jax.ShapeDtypeStruct((N_SLABS, LANES, SLAB_STATES), BF16)
    out_shape = (jax.ShapeDtypeStruct((1, SSM_LANES), F32), jax.ShapeDtypeStruct((1, SSM_LANES), F32),
                 slab, slab, slab, slab,
                 jax.ShapeDtypeStruct((1, SSM_WIDTH), F32), jax.ShapeDtypeStruct((1, LANES), F32),
                 jax.ShapeDtypeStruct((1, 1), F32))
    return pl.pallas_call(
        functools.partial(_prep_kernel, lam_init=lam_init),
        in_specs=[layer(g, p), layer(g, p), layer(1, g), layer(g, c, p), layer(g, c, p), layer(g, c, p),
                  layer(g, c, p), layer(g, c), layer(1, HEAD_DIM)] + [layer(1, HALF_DIM)] * 4,
        out_shape=out_shape, name="prep",
    )(a_re, a_im, row(log_dt), jnp.swapaxes(b_re, 2, 3), jnp.swapaxes(b_im, 2, 3), c_re, c_im, d,
      row(subln_g), row(lq1), row(lk1), row(lq2), row(lk2))


def _rope_tables(positions):
    inv = ROPE_THETA ** (-np.arange(0, ROT_DIM, 2, dtype=np.float64) / ROT_DIM)
    ang = np.asarray(positions, np.float64)[:, None] * inv[None, :]
    r = np.arange(LANES) % HALF_DIM
    half = ROT_DIM // 2
    idx = r % half
    cos = np.where(r[None, :] < ROT_DIM, np.cos(ang)[:, idx], 1.0)
    sin = np.sin(ang)[:, idx]
    s_up = np.where(r[None, :] < half, -sin, 0.0)
    s_dn = np.where((r[None, :] >= half) & (r[None, :] < ROT_DIM), sin, 0.0)
    return (jnp.asarray(cos, F32), jnp.asarray(s_up, F32), jnp.asarray(s_dn, F32))


def _inproj_kernel(after_ref, x_ref, w_ref, cos_ref, sup_ref, sdn_ref, q_ref, k_ref, v_ref, u_ref, *, nb, tl):
    del after_ref
    half = ROT_DIM // 2
    x = x_ref[...].reshape(nb * tl, D_MODEL).astype(BF16)
    cos = jnp.concatenate([cos_ref[...]] * nb, axis=0)
    sup = jnp.concatenate([sup_ref[...]] * nb, axis=0)
    sdn = jnp.concatenate([sdn_ref[...]] * nb, axis=0)

    def rope(t):
        slabs = []
        for j in range(ATTN_WIDTH // LANES):
            s = t[:, j * LANES:(j + 1) * LANES]
            slabs.append(s * cos + pltpu.roll(s, LANES - half, 1) * sup + pltpu.roll(s, half, 1) * sdn)
        return jnp.concatenate(slabs, axis=1)

    q = jnp.dot(x, w_ref[:, 0:ATTN_WIDTH], preferred_element_type=F32)
    q_ref[...] = (rope(q) * (QK_SCALE * LOG2_E)).astype(BF16).reshape(nb, tl, ATTN_WIDTH)
    k = jnp.dot(x, w_ref[:, ATTN_WIDTH:2 * ATTN_WIDTH], preferred_element_type=F32)
    k_ref[...] = rope(k).reshape(nb, tl, ATTN_WIDTH)
    v = jnp.dot(x, w_ref[:, 2 * ATTN_WIDTH:3 * ATTN_WIDTH], preferred_element_type=F32)
    v_ref[...] = v.reshape(nb, tl, ATTN_WIDTH)
    u = jnp.dot(x, w_ref[:, 3 * ATTN_WIDTH:], preferred_element_type=F32)
    first_slot = (pl.program_id(1) * nb) % SUBLANES
    for b in range(nb):
        for s in range(N_SLABS):
            u_ref[s, pl.ds(first_slot + b, tl, stride=SUBLANES), :] = u[b * tl:(b + 1) * tl,
                                                                        s * LANES:(s + 1) * LANES]


def _inproj(after, x, w_bf, tables, nb, tl):
    bsz, seq, _ = x.shape
    nblk = bsz // SUBLANES
    per_blk = SUBLANES // nb
    grid = (seq // tl, bsz // nb)
    tab_spec = pl.BlockSpec((tl, LANES), lambda i, b: (i, 0))
    act_spec = lambda width: pl.BlockSpec((nb, tl, width), lambda i, b: (b, i, 0))
    out_shape = (jax.ShapeDtypeStruct((bsz, seq, ATTN_WIDTH), BF16),
                 jax.ShapeDtypeStruct((bsz, seq, ATTN_WIDTH), F32),
                 jax.ShapeDtypeStruct((bsz, seq, ATTN_WIDTH), F32),
                 jax.ShapeDtypeStruct((nblk, N_SLABS, seq * SUBLANES, LANES), F32))
    return pl.pallas_call(
        functools.partial(_inproj_kernel, nb=nb, tl=tl),
        grid=grid,
        in_specs=[pl.BlockSpec(memory_space=pltpu.SMEM), act_spec(D_MODEL),
                  pl.BlockSpec((D_MODEL, 4 * ATTN_WIDTH), lambda i, b: (0, 0)),
                  tab_spec, tab_spec, tab_spec],
        out_specs=(act_spec(ATTN_WIDTH), act_spec(ATTN_WIDTH), act_spec(ATTN_WIDTH),
                   pl.BlockSpec((None, N_SLABS, tl * SUBLANES, LANES), lambda i, b: (b // per_blk, 0, i, 0))),
        out_shape=out_shape,
        compiler_params=_params(("parallel", "arbitrary")),
        name="inproj",
    )(after, x, w_bf, *tables)


def _expand_maps(q):
    lane = lax.broadcasted_iota(jnp.int32, q.shape, 1)
    zero = jnp.zeros_like(q)
    return jnp.concatenate(
        [jnp.where((lane >= m * HALF_DIM) & (lane < (m + 1) * HALF_DIM), q, zero) for m in range(MAPS_PER_SLAB)],
        axis=0)


def _diff_combine(o, lam, g, n, lam_init):
    lane = lax.broadcasted_iota(jnp.int32, (n, LANES), 1)
    first = lane < HEAD_DIM
    d = jnp.where(first, o[0:n] - lam * o[n:2 * n], o[2 * n:3 * n] - lam * o[3 * n:4 * n])
    sq = d * d
    ss_a = jnp.sum(jnp.where(first, sq, 0.0), axis=-1, keepdims=True)
    ss_b = jnp.sum(jnp.where(first, 0.0, sq), axis=-1, keepdims=True)
    ms = jnp.where(first, ss_a, ss_b) * (1.0 / HEAD_DIM)
    return d * lax.rsqrt(ms + SUBLN_EPS) * g * (1.0 - lam_init)


def _attn_prompt_kernel(lam_ref, q_ref, k_ref, v_ref, g_ref, o_ref, kb, vb, qx_a, qx_b, s_a, s_b, m_ref, acc_ref,
                        *, blk, lam_init):
    seq = q_ref.shape[0]
    hb = blk // 2
    half = MAPS_PER_SLAB * hb
    rows = 2 * half
    dims = (((1,), (1,)), ((), ()))

    kb[...] = k_ref[...].astype(BF16)
    vb[:, 0:LANES] = v_ref[...].astype(BF16)
    vb[:, LANES:2 * LANES] = jnp.ones(v_ref.shape, BF16)

    steps = [(i, j) for i in range(seq // blk) for j in range(i + 1)]
    qx_bufs = (qx_a, qx_b)
    s_bufs = (s_a, s_b)

    def expand_queries(i):
        qx = qx_bufs[i % 2]
        qx[0:half, :] = _expand_maps(q_ref[i * blk:i * blk + hb, :])
        qx[half:rows, :] = _expand_maps(q_ref[i * blk + hb:(i + 1) * blk, :])

    def scores(n):
        i, j = steps[n]
        s_bufs[n % 2][...] = lax.dot_general(qx_bufs[i % 2][...], kb[j * blk:(j + 1) * blk, :], dims,
                                             preferred_element_type=F32)

    def update(j, src, h, width, row_offset):
        rs = slice(h * half, (h + 1) * half)
        s = src[rs, 0:width]
        if row_offset is not None:
            r = lax.broadcasted_iota(jnp.int32, (half, width), 0)
            c = lax.broadcasted_iota(jnp.int32, (half, width), 1)
            s = jnp.where((row_offset + r % hb) // CHUNK >= c // CHUNK, s, -jnp.inf)
        m_prev = m_ref[rs, :]
        m_new = jnp.maximum(m_prev, jnp.max(s, axis=-1, keepdims=True))
        alpha = jnp.exp2(m_prev - m_new)
        p = jnp.exp2(s - jnp.concatenate([m_new] * (width // LANES), axis=1)).astype(BF16)
        pv = jnp.dot(p, vb[j * blk:j * blk + width, :], preferred_element_type=F32)
        acc_ref[rs, :] = jnp.concatenate([alpha, alpha], axis=1) * acc_ref[rs, :] + pv
        m_ref[rs, :] = m_new

    expand_queries(0)
    scores(0)
    for n, (i, j) in enumerate(steps):
        if n + 1 < len(steps):
            if steps[n + 1][1] == 0:
                expand_queries(steps[n + 1][0])
            scores(n + 1)
        if j == 0:
            m_ref[...] = jnp.full((rows, LANES), -jnp.inf, F32)
            acc_ref[...] = jnp.zeros((rows, 2 * LANES), F32)
        src = s_bufs[n % 2]
        if j < i:
            update(j, src, 0, blk, None)
            update(j, src, 1, blk, None)
        else:
            update(j, src, 0, hb, 0)
            update(j, src, 1, blk, hb)
            for h in range(2):
                acc = acc_ref[h * half:(h + 1) * half, :]
                o = acc[:, 0:LANES] / acc[:, LANES:2 * LANES]
                o_ref[i * blk + h * hb:i * blk + (h + 1) * hb, :] = _diff_combine(
                    o, lam_ref[0, 0], g_ref[...], hb, lam_init).astype(BF16)


def _attn_prompt(lam, q, k, v, g128, blk, lam_init):
    bsz, seq, _ = q.shape
    rows = MAPS_PER_SLAB * blk
    slab_spec = pl.BlockSpec((None, seq, LANES), lambda b, j: (b, 0, j))
    return pl.pallas_call(
        functools.partial(_attn_prompt_kernel, blk=blk, lam_init=lam_init),
        grid=(bsz, ATTN_WIDTH // LANES),
        in_specs=[pl.BlockSpec(memory_space=pltpu.SMEM), slab_spec, slab_spec, slab_spec,
                  pl.BlockSpec((1, LANES), lambda b, j: (0, 0))],
        out_specs=slab_spec,
        out_shape=jax.ShapeDtypeStruct((bsz, seq, ATTN_WIDTH), BF16),
        scratch_shapes=[pltpu.VMEM((seq, LANES), BF16), pltpu.VMEM((seq, 2 * LANES), BF16),
                        pltpu.VMEM((rows, LANES), BF16), pltpu.VMEM((rows, LANES), BF16),
                        pltpu.VMEM((rows, blk), F32), pltpu.VMEM((rows, blk), F32),
                        pltpu.VMEM((rows, LANES), F32), pltpu.VMEM((rows, 2 * LANES), F32)],
        compiler_params=_params(("parallel", "parallel")),
        name="attn_prompt",
    )(lam, q, k, v, g128)


def _attn_sample_kernel(lam_ref, q_ref, kc_ref, vc_ref, kn_ref, vn_ref, g_ref, o_ref, *, n, nseq, lam_init):
    dims = (((1,), (1,)), ((), ()))
    chains = [(b, slice(j * LANES, (j + 1) * LANES)) for b in range(nseq) for j in range(ATTN_WIDTH // LANES)]
    scores = []
    for b, cols in chains:
        qx = _expand_maps(q_ref[b, :, cols])
        scores.append((jnp.dot(qx, kc_ref[b, cols, :].astype(BF16), preferred_element_type=F32),
                       lax.dot_general(qx, kn_ref[b, :, cols].astype(BF16), dims, preferred_element_type=F32)))
    for (b, cols), (s_c, s_n) in zip(chains, scores):
        m = jnp.maximum(jnp.max(s_c, axis=-1, keepdims=True), jnp.max(s_n, axis=-1, keepdims=True))
        p_c = jnp.exp2(s_c - m)
        p_n = jnp.exp2(s_n - m)
        l = jnp.sum(p_c, axis=-1, keepdims=True) + jnp.sum(p_n, axis=-1, keepdims=True)
        acc = (lax.dot_general(p_c.astype(BF16), vc_ref[b, cols, :].astype(BF16), dims, preferred_element_type=F32)
               + jnp.dot(p_n.astype(BF16), vn_ref[b, :, cols].astype(BF16), preferred_element_type=F32))
        o_ref[b, :, cols] = _diff_combine(acc / l, lam_ref[0, 0], g_ref[...], n, lam_init).astype(BF16)


def _attn_sample(lam, q, cache_kt, cache_vt, k_new, v_new, g128, lam_init, nseq=SAMPLE_SEQS_PER_STEP):
    bsz, n, _ = q.shape
    past = cache_kt.shape[2]
    assert past % CHUNK == 0 and n <= CHUNK
    cache_spec = pl.BlockSpec((nseq, ATTN_WIDTH, past), lambda b: (b, 0, 0))
    new_spec = pl.BlockSpec((nseq, n, ATTN_WIDTH), lambda b: (b, 0, 0))
    return pl.pallas_call(
        functools.partial(_attn_sample_kernel, n=n, nseq=nseq, lam_init=lam_init),
        grid=(bsz // nseq,),
        in_specs=[pl.BlockSpec(memory_space=pltpu.SMEM), new_spec, cache_spec, cache_spec, new_spec, new_spec,
                  pl.BlockSpec((1, LANES), lambda b: (0, 0))],
        out_specs=new_spec,
        out_shape=jax.ShapeDtypeStruct((bsz, n, ATTN_WIDTH), BF16),
        compiler_params=_params(("parallel",)),
        name="attn_sample",
    )(lam, q, cache_kt, cache_vt, k_new, v_new, g128)


def _ssm_kernel(u_ref, h0re_ref, h0im_ref, are_ref, aim_ref, bre_ref, bim_ref, cre_ref, cim_ref, d_ref,
                wglu_ref, bglu_ref, out_ref, hre_out, him_out, bure, buim, hsre, hsim, hre, him, res,
                *, tt):
    i = pl.program_id(1)
    dims = (((1,), (1,)), ((), ()))

    @pl.when(i == 0)
    def _():
        for b in range(SUBLANES):
            hre[b:b + 1, :] = _lane_concat_rows(h0re_ref[b])
            him[b:b + 1, :] = _lane_concat_rows(h0im_ref[b])

    ys = []
    for s in range(N_SLABS):
        us = u_ref[s].astype(BF16)
        cols = slice(s * SLAB_STATES, (s + 1) * SLAB_STATES)
        bure[:, cols] = jnp.dot(us, bre_ref[s], preferred_element_type=F32)
        buim[:, cols] = jnp.dot(us, bim_ref[s], preferred_element_type=F32)
        ar = jnp.broadcast_to(are_ref[:, cols], (SUBLANES, SLAB_STATES))
        ai = jnp.broadcast_to(aim_ref[:, cols], (SUBLANES, SLAB_STATES))
        hr = hre[:, cols]
        hi = him[:, cols]
        for t in range(tt):
            r = slice(t * SUBLANES, (t + 1) * SUBLANES)
            hr, hi = ar * hr - ai * hi + bure[r, cols], ar * hi + ai * hr + buim[r, cols]
            hsre[r, cols] = hr
            hsim[r, cols] = hi
        hre[:, cols] = hr
        him[:, cols] = hi
        ys.append(lax.dot_general(hsre[:, cols].astype(BF16), cre_ref[s], dims, preferred_element_type=F32)
                  - lax.dot_general(hsim[:, cols].astype(BF16), cim_ref[s], dims, preferred_element_type=F32)
                  + d_ref[:, s * LANES:(s + 1) * LANES] * u_ref[s])
    y = jnp.concatenate(ys, axis=1)
    z = jnp.dot(jax.nn.gelu(y).astype(BF16), wglu_ref[...], preferred_element_type=F32) + bglu_ref[...]
    gated = z[:, :SSM_WIDTH] * jax.nn.sigmoid(z[:, SSM_WIDTH:])
    for s in range(N_SLABS):
        res[s] = gated[:, s * LANES:(s + 1) * LANES]
    for b in range(SUBLANES):
        for s in range(N_SLABS):
            out_ref[b, :, s * LANES:(s + 1) * LANES] = res[s, pl.ds(b, tt, stride=SUBLANES), :].astype(BF16)

    @pl.when(i == pl.num_programs(1) - 1)
    def _():
        for b in range(SUBLANES):
            for g in range(N_SSM_GROUPS):
                lanes = slice(g * SSM_STATE, (g + 1) * SSM_STATE)
                hre_out[b, g:g + 1, :] = hre[b:b + 1, lanes]
                him_out[b, g:g + 1, :] = him[b:b + 1, lanes]


def _ssm(l, u_t, h0_re, h0_im, lb_re, lb_im, bre, bim, cre, cim, d, w_glu, b_glu, tt):
    nblk, _, rows_total, _ = u_t.shape
    seq = rows_total // SUBLANES
    rows = SUBLANES * tt
    grid = (nblk, seq // tt)
    const = lambda shape: pl.BlockSpec(shape, lambda b, i: (0,) * len(shape))
    layer = lambda *shape: pl.BlockSpec((None,) + shape, lambda b, i: (l,) + (0,) * len(shape))
    slab_spec = const((N_SLABS, LANES, SLAB_STATES))
    state_spec = pl.BlockSpec((SUBLANES, N_SSM_GROUPS, SSM_STATE), lambda b, i: (b, 0, 0))
    state_shape = jax.ShapeDtypeStruct((nblk * SUBLANES, N_SSM_GROUPS, SSM_STATE), F32)
    return pl.pallas_call(
        functools.partial(_ssm_kernel, tt=tt),
        grid=grid,
        in_specs=[pl.BlockSpec((None, N_SLABS, rows, LANES), lambda b, i: (b, 0, i, 0)),
                  state_spec, state_spec,
                  const((1, SSM_LANES)), const((1, SSM_LANES)),
                  slab_spec, slab_spec, slab_spec, slab_spec,
                  const((1, SSM_WIDTH)), const((SSM_WIDTH, 2 * SSM_WIDTH)), layer(1, 2 * SSM_WIDTH)],
        out_specs=(pl.BlockSpec((SUBLANES, tt, SSM_WIDTH), lambda b, i: (b, i, 0)), state_spec, state_spec),
        out_shape=(jax.ShapeDtypeStruct((nblk * SUBLANES, seq, SSM_WIDTH), BF16), state_shape, state_shape),
        scratch_shapes=[pltpu.VMEM((rows, SSM_LANES), F32) for _ in range(4)]
                       + [pltpu.VMEM((SUBLANES, SSM_LANES), F32) for _ in range(2)]
                       + [pltpu.VMEM((N_SLABS, rows, LANES), F32)],
        compiler_params=_params(("parallel", "arbitrary")),
        name="ssm",
    )(u_t, h0_re, h0_im, lb_re, lb_im, bre, bim, cre, cim, d, w_glu, b_glu)


def _layer_norm(x, g, b):
    mu = jnp.mean(x, axis=-1, keepdims=True)
    xc = x - mu
    var = jnp.mean(xc * xc, axis=-1, keepdims=True)
    return xc * lax.rsqrt(var + LN_EPS) * g + b


def _post_kernel(xp_ref, ap_ref, sp_ref, xs_ref, as_ref, ss_ref, wout_ref, g1_ref, b1_ref, wup_ref, wdown_ref,
                 g2_ref, b2_ref, op_ref, os_ref, *, tl, n_prompt, ff_chunk):
    is_prompt = pl.program_id(0) < n_prompt
    tile = functools.partial(_post_tile, wout_ref, g1_ref, b1_ref, wup_ref, wdown_ref, g2_ref, b2_ref,
                             tl=tl, ff_chunk=ff_chunk)

    @pl.when(is_prompt)
    def _():
        tile(xp_ref, ap_ref, sp_ref, op_ref)

    @pl.when(jnp.logical_not(is_prompt))
    def _():
        tile(xs_ref, as_ref, ss_ref, os_ref)


def _post_tile(wout_ref, g1_ref, b1_ref, wup_ref, wdown_ref, g2_ref, b2_ref, x_ref, a_ref, s_ref, o_ref,
               *, tl, ff_chunk):
    chains = (0, 1)
    n = tl // len(chains)
    views = [slice(c * n, (c + 1) * n) for c in chains]
    xs = [x_ref[v, :] for v in views]
    mix = [jnp.dot(jnp.concatenate([a_ref[v, :], s_ref[v, :]], axis=1), wout_ref[...],
                   preferred_element_type=F32) for v in views]
    x1 = [None, None]
    x1b = [None, None]
    h = [None, None]
    ff = [jnp.zeros((n, D_MODEL), F32) for _ in chains]
    n_chunks = D_FF // ff_chunk

    def up(c, k):
        return jnp.dot(x1b[c], wup_ref[:, k * ff_chunk:(k + 1) * ff_chunk], preferred_element_type=F32)

    def down(c, k):
        act = jnp.square(jnp.maximum(h[c], 0.0)).astype(BF16)
        return ff[c] + jnp.dot(act, wdown_ref[k * ff_chunk:(k + 1) * ff_chunk, :], preferred_element_type=F32)

    x1[0] = _layer_norm(DEEPNORM_ALPHA * xs[0] + mix[0], g1_ref[...], b1_ref[...])
    x1b[0] = x1[0].astype(BF16)
    h[0] = up(0, 0)
    x1[1] = _layer_norm(DEEPNORM_ALPHA * xs[1] + mix[1], g1_ref[...], b1_ref[...])
    x1b[1] = x1[1].astype(BF16)
    for k in range(n_chunks):
        h[1] = up(1, k)
        ff[0] = down(0, k)
        if k + 1 < n_chunks:
            h[0] = up(0, k + 1)
        ff[1] = down(1, k)
    for c in chains:
        o_ref[views[c], :] = _layer_norm(DEEPNORM_ALPHA * x1[c] + ff[c], g2_ref[...], b2_ref[...])


def _post(l, prompt, sample, w_out, g1, b1, w_up, w_down, g2, b2, tl):
    bp, seq, _ = prompt[0].shape
    bs, dec_seq, _ = sample[0].shape
    per_seq = seq // tl
    n_prompt = bp * per_seq
    n_sample = bs * dec_seq // tl
    sample = [a.reshape(n_sample, tl, a.shape[-1]) for a in sample]

    def p_spec(width):
        def index(s):
            t = jnp.minimum(s, n_prompt - 1)
            return (t // per_seq, t % per_seq, 0)
        return pl.BlockSpec((None, tl, width), index)

    s_spec = lambda width: pl.BlockSpec((None, tl, width), lambda s: (jnp.maximum(s - n_prompt, 0), 0, 0))
    const = lambda shape: pl.BlockSpec(shape, lambda s: (0,) * len(shape), pipeline_mode=pl.Buffered(1))
    layer = lambda *shape: pl.BlockSpec((None,) + shape, lambda s: (l,) + (0,) * len(shape))
    widths = (D_MODEL, ATTN_WIDTH, SSM_WIDTH)
    yp, ys = pl.pallas_call(
        functools.partial(_post_kernel, tl=tl, n_prompt=n_prompt, ff_chunk=FF_CHUNK),
        grid=(n_prompt + n_sample,),
        in_specs=[p_spec(w) for w in widths] + [s_spec(w) for w in widths]
                 + [const((D_MODEL, D_MODEL)), layer(1, D_MODEL), layer(1, D_MODEL),
                    const((D_MODEL, D_FF)), const((D_FF, D_MODEL)), layer(1, D_MODEL), layer(1, D_MODEL)],
        out_specs=(p_spec(D_MODEL), s_spec(D_MODEL)),
        out_shape=(jax.ShapeDtypeStruct((bp, seq, D_MODEL), F32),
                   jax.ShapeDtypeStruct((n_sample, tl, D_MODEL), F32)),
        compiler_params=_params(("arbitrary",)),
        name="post",
    )(*prompt, *sample, w_out, g1, b1, w_up, w_down, g2, b2)
    return yp, ys.reshape(bs, dec_seq, D_MODEL)


def kernel(x_prompt, x_sample, cache_k, cache_v, state_ssm_re, state_ssm_im, w_in, lambda_q1, lambda_k1,
           lambda_q2, lambda_k2, subln_g, ssm_a_re, ssm_a_im, ssm_log_dt, ssm_b_re, ssm_b_im, ssm_c_re,
           ssm_c_im, ssm_d, w_glu, b_glu, w_out, ln1_g, ln1_b, w_up, w_down, ln2_g, ln2_b):
    assert w_in.shape[0] == DEPTH
    l = 0
    lam_init = _lambda_init(l)
    bp, seq, _ = x_prompt.shape
    bs, dec_seq, _ = x_sample.shape
    past = cache_k.shape[2]
    assert bp % SUBLANES == 0 and bs % SUBLANES == 0 and bs % SAMPLE_SEQS_PER_STEP == 0
    assert seq % ROW_TILE == 0 and seq % ATTN_BLOCK == 0 and SUBLANES * dec_seq == ROW_TILE

    lb_re, lb_im, bre, bim, cre, cim, d, g128, lam = _prep(
        l, ssm_a_re, ssm_a_im, ssm_log_dt, ssm_b_re, ssm_b_im, ssm_c_re, ssm_c_im, ssm_d, subln_g,
        lambda_q1, lambda_k1, lambda_q2, lambda_k2, lam_init)
    w_in_bf, w_glu_bf, w_out_bf, w_up_bf, w_down_bf = (w[l].astype(BF16) for w in (w_in, w_glu, w_out, w_up, w_down))
    row = lambda a: a.reshape(a.shape[0], 1, a.shape[1])
    b_glu3 = row(b_glu)
    ln = [row(a) for a in (ln1_g, ln1_b, ln2_g, ln2_b)]

    def mixers(after, x, positions, cache, h0_re, h0_im, nb, tl, tt):
        bsz, n, _ = x.shape
        q, k, v, u_t = _inproj(after, x, w_in_bf, _rope_tables(positions), nb, tl)
        if cache is None:
            attn = _attn_prompt(lam, q, k, v, g128, ATTN_BLOCK, lam_init)
        else:
            attn = _attn_sample(lam, q, cache[0], cache[1], k, v, g128, lam_init)
        ssm, h_re, h_im = _ssm(l, u_t, h0_re, h0_im, lb_re, lb_im, bre, bim, cre, cim, d, w_glu_bf, b_glu3, tt)
        shape_kv = (1, bsz, n, N_HEADS, HEAD_DIM)
        return attn, ssm, k.reshape(shape_kv), v.reshape(shape_kv), h_re[None], h_im[None]

    cache = tuple(jnp.transpose(c[l], (0, 2, 3, 1)).reshape(bs, ATTN_WIDTH, past) for c in (cache_k, cache_v))
    attn_s, ssm_s, ks, vs, rs, is_ = mixers(lam, x_sample, past + np.arange(dec_seq), cache, state_ssm_re[l],
                                            state_ssm_im[l], nb=SUBLANES, tl=dec_seq, tt=dec_seq)
    zeros = jnp.zeros((bp, N_SSM_GROUPS, SSM_STATE), F32)
    after = ssm_s[0, 0:1, 0:1].astype(F32)
    attn_p, ssm_p, kp, vp, rp, ip = mixers(after, x_prompt, np.arange(seq), None, zeros, zeros,
                                           nb=SUBLANES, tl=ROW_TILE // SUBLANES, tt=SSM_TIME_TILE)
    yp, ys = _post(l, (x_prompt, attn_p, ssm_p), (x_sample, attn_s, ssm_s), w_out_bf, ln[0], ln[1],
                   w_up_bf, w_down_bf, ln[2], ln[3], tl=ROW_TILE)
    return (yp, ys, kp, vp, rp, ip, ks, vs, rs, is_)
```

```python
import functools
import math

import numpy as np
import jax
import jax.numpy as jnp
from jax import lax
from jax.experimental import pallas as pl
from jax.experimental.pallas import tpu as pltpu

D_MODEL = 1024
DEPTH = 1
CHUNK = 64
ATTN_WIDTH = 512
SSM_WIDTH = 512
N_HEADS = 8
HEAD_DIM = 64
HALF_DIM = 32
ROT_DIM = 8
ROPE_THETA = 500000.0
SSM_GROUP = 16
N_SSM_GROUPS = 32
SSM_STATE = 64
D_FF = 4 * D_MODEL
LN_EPS = 1e-5
SUBLN_EPS = 1e-5
DEEPNORM_ALPHA = (2 * DEPTH) ** 0.25
QK_SCALE = HALF_DIM ** -0.5
LOG2_E = math.log2(math.e)

SUBLANES = 8
LANES = 128
SSM_LANES = N_SSM_GROUPS * SSM_STATE
GROUPS_PER_SLAB = LANES // SSM_GROUP
N_SLABS = SSM_WIDTH // LANES
SLAB_STATES = GROUPS_PER_SLAB * SSM_STATE
HEADS_PER_SLAB = LANES // HEAD_DIM
MAPS_PER_SLAB = LANES // HALF_DIM
VMEM_LIMIT_BYTES = 56 * 1024 * 1024

ROW_TILE = 512
ATTN_BLOCK = 512
SSM_TIME_TILE = 64
SAMPLE_SEQS_PER_STEP = 2
FF_CHUNK = 1024

F32 = jnp.float32
BF16 = jnp.bfloat16


def _lambda_init(layer_idx):
    return 0.8 - 0.6 * math.exp(-0.3 * layer_idx)


def _params(semantics):
    return pltpu.CompilerParams(dimension_semantics=semantics, vmem_limit_bytes=VMEM_LIMIT_BYTES)


def _lane_concat_rows(x):
    return jnp.concatenate([x[r:r + 1, :] for r in range(x.shape[0])], axis=1)


def _block_diag_slabs(blocks):
    slabs = []
    for s in range(N_SLABS):
        rows = []
        for gi in range(GROUPS_PER_SLAB):
            pieces = []
            if gi > 0:
                pieces.append(jnp.zeros((SSM_GROUP, gi * SSM_STATE), F32))
            pieces.append(blocks[s * GROUPS_PER_SLAB + gi])
            if gi < GROUPS_PER_SLAB - 1:
                pieces.append(jnp.zeros((SSM_GROUP, (GROUPS_PER_SLAB - 1 - gi) * SSM_STATE), F32))
            rows.append(jnp.concatenate(pieces, axis=1))
        slabs.append(jnp.concatenate(rows, axis=0))
    return slabs


def _prep_kernel(are_ref, aim_ref, logdt_ref, bre_ref, bim_ref, cre_ref, cim_ref, d_ref, g_ref,
                 lq1_ref, lk1_ref, lq2_ref, lk2_ref,
                 lbre_ref, lbim_ref, bbre_ref, bbim_ref, ccre_ref, ccim_ref, dd_ref, gg_ref, lam_ref,
                 *, lam_init):
    g = N_SSM_GROUPS
    eye = lax.broadcasted_iota(jnp.int32, (g, g), 0) == lax.broadcasted_iota(jnp.int32, (g, g), 1)
    logdt = jnp.sum(jnp.where(eye, jnp.broadcast_to(logdt_ref[...], (g, g)), 0.0), axis=-1, keepdims=True)
    dt = jnp.exp(logdt)
    ar = are_ref[...]
    ai = aim_ref[...]
    mag = jnp.exp(ar * dt)
    lb_re = mag * jnp.cos(ai * dt)
    lb_im = mag * jnp.sin(ai * dt)
    nr = lb_re - 1.0
    ni = lb_im
    den = ar * ar + ai * ai
    f_re = (nr * ar + ni * ai) / den
    f_im = (ni * ar - nr * ai) / den
    lbre_ref[...] = _lane_concat_rows(lb_re)
    lbim_ref[...] = _lane_concat_rows(lb_im)
    br = bre_ref[...]
    bi = bim_ref[...]
    bb_re = f_re[:, None, :] * br - f_im[:, None, :] * bi
    bb_im = f_re[:, None, :] * bi + f_im[:, None, :] * br
    for out_ref, blocks in ((bbre_ref, bb_re), (bbim_ref, bb_im), (ccre_ref, cre_ref[...]), (ccim_ref, cim_ref[...])):
        for s, slab in enumerate(_block_diag_slabs(blocks)):
            out_ref[s] = slab.astype(BF16)
    dd_ref[...] = _lane_concat_rows(d_ref[...])
    gg_ref[...] = jnp.concatenate([g_ref[...]] * HEADS_PER_SLAB, axis=1)
    s1 = jnp.sum(lq1_ref[...] * lk1_ref[...], axis=-1, keepdims=True)
    s2 = jnp.sum(lq2_ref[...] * lk2_ref[...], axis=-1, keepdims=True)
    lam_ref[...] = jnp.exp(s1) - jnp.exp(s2) + lam_init


def _prep(l, a_re, a_im, log_dt, b_re, b_im, c_re, c_im, d, subln_g, lq1, lk1, lq2, lk2, lam_init):
    g, p, c = N_SSM_GROUPS, SSM_STATE, SSM_GROUP
    layer = lambda *shape: pl.BlockSpec((None,) + shape, lambda: (l,) + (0,) * len(shape))
    row = lambda a: a.reshape(a.shape[0], 1, a.shape[1])
    slab = jax.ShapeDtypeStruct((N_SLABS, LANES, SLAB_STATES), BF16)
    out_shape = (jax.ShapeDtypeStruct((1, SSM_LANES), F32), jax.ShapeDtypeStruct((1, SSM_LANES), F32),
                 slab, slab, slab, slab,
                 jax.ShapeDtypeStruct((1, SSM_WIDTH), F32), jax.ShapeDtypeStruct((1, LANES), F32),
                 jax.ShapeDtypeStruct((1, 1), F32))
    return pl.pallas_call(
        functools.partial(_prep_kernel, lam_init=lam_init),
        in_specs=[layer(g, p), layer(g, p), layer(1, g), layer(g, c, p), layer(g, c, p), layer(g, c, p),
                  layer(g, c, p), layer(g, c), layer(1, HEAD_DIM)] + [layer(1, HALF_DIM)] * 4,
        out_shape=out_shape, name="prep",
    )(a_re, a_im, row(log_dt), jnp.swapaxes(b_re, 2, 3), jnp.swapaxes(b_im, 2, 3), c_re, c_im, d,
      row(subln_g), row(lq1), row(lk1), row(lq2), row(lk2))


def _rope_tables(positions):
    inv = ROPE_THETA ** (-np.arange(0, ROT_DIM, 2, dtype=np.float64) / ROT_DIM)
    ang = np.asarray(positions, np.float64)[:, None] * inv[None, :]
    r = np.arange(LANES) % HALF_DIM
    half = ROT_DIM // 2
    idx = r % half
    cos = np.where(r[None, :] < ROT_DIM, np.cos(ang)[:, idx], 1.0)
    sin = np.sin(ang)[:, idx]
    s_up = np.where(r[None, :] < half, -sin, 0.0)
    s_dn = np.where((r[None, :] >= half) & (r[None, :] < ROT_DIM), sin, 0.0)
    return (jnp.asarray(cos, F32), jnp.asarray(s_up, F32), jnp.asarray(s_dn, F32))


def _inproj_kernel(after_ref, x_ref, w_ref, cos_ref, sup_ref, sdn_ref, q_ref, k_ref, v_ref, u_ref, *, nb, tl):
    del after_ref
    half = ROT_DIM // 2
    x = x_ref[...].reshape(nb * tl, D_MODEL).astype(BF16)
    cos = jnp.concatenate([cos_ref[...]] * nb, axis=0)
    sup = jnp.concatenate([sup_ref[...]] * nb, axis=0)
    sdn = jnp.concatenate([sdn_ref[...]] * nb, axis=0)

    def rope(t):
        slabs = []
        for j in range(ATTN_WIDTH // LANES):
            s = t[:, j * LANES:(j + 1) * LANES]
            slabs.append(s * cos + pltpu.roll(s, LANES - half, 1) * sup + pltpu.roll(s, half, 1) * sdn)
        return jnp.concatenate(slabs, axis=1)

    q = jnp.dot(x, w_ref[:, 0:ATTN_WIDTH], preferred_element_type=F32)
    q_ref[...] = (rope(q) * (QK_SCALE * LOG2_E)).astype(BF16).reshape(nb, tl, ATTN_WIDTH)
    k = jnp.dot(x, w_ref[:, ATTN_WIDTH:2 * ATTN_WIDTH], preferred_element_type=F32)
    k_ref[...] = rope(k).reshape(nb, tl, ATTN_WIDTH)
    v = jnp.dot(x, w_ref[:, 2 * ATTN_WIDTH:3 * ATTN_WIDTH], preferred_element_type=F32)
    v_ref[...] = v.reshape(nb, tl, ATTN_WIDTH)
    u = jnp.dot(x, w_ref[:, 3 * ATTN_WIDTH:], preferred_element_type=F32)
    first_slot = (pl.program_id(1) * nb) % SUBLANES
    for b in range(nb):
        for s in range(N_SLABS):
            u_ref[s, pl.ds(first_slot + b, tl, stride=SUBLANES), :] = u[b * tl:(b + 1) * tl,
                                                                        s * LANES:(s + 1) * LANES]


def _inproj(after, x, w_bf, tables, nb, tl):
    bsz, seq, _ = x.shape
    nblk = bsz // SUBLANES
    per_blk = SUBLANES // nb
    grid = (seq // tl, bsz // nb)
    tab_spec = pl.BlockSpec((tl, LANES), lambda i, b: (i, 0))
    act_spec = lambda width: pl.BlockSpec((nb, tl, width), lambda i, b: (b, i, 0))
    out_shape = (jax.ShapeDtypeStruct((bsz, seq, ATTN_WIDTH), BF16),
                 jax.ShapeDtypeStruct((bsz, seq, ATTN_WIDTH), F32),
                 jax.ShapeDtypeStruct((bsz, seq, ATTN_WIDTH), F32),
                 jax.ShapeDtypeStruct((nblk, N_SLABS, seq * SUBLANES, LANES), F32))
    return pl.pallas_call(
        functools.partial(_inproj_kernel, nb=nb, tl=tl),
        grid=grid,
        in_specs=[pl.BlockSpec(memory_space=pltpu.SMEM), act_spec(D_MODEL),
                  pl.BlockSpec((D_MODEL, 4 * ATTN_WIDTH), lambda i, b: (0, 0)),
                  tab_spec, tab_spec, tab_spec],
        out_specs=(act_spec(ATTN_WIDTH), act_spec(ATTN_WIDTH), act_spec(ATTN_WIDTH),
                   pl.BlockSpec((None, N_SLABS, tl * SUBLANES, LANES), lambda i, b: (b // per_blk, 0, i, 0))),
        out_shape=out_shape,
        compiler_params=_params(("parallel", "arbitrary")),
        name="inproj",
    )(after, x, w_bf, *tables)


def _expand_maps(q):
    lane = lax.broadcasted_iota(jnp.int32, q.shape, 1)
    zero = jnp.zeros_like(q)
    return jnp.concatenate(
        [jnp.where((lane >= m * HALF_DIM) & (lane < (m + 1) * HALF_DIM), q, zero) for m in range(MAPS_PER_SLAB)],
        axis=0)


def _diff_combine(o, lam, g, n, lam_init):
    lane = lax.broadcasted_iota(jnp.int32, (n, LANES), 1)
    first = lane < HEAD_DIM
    d = jnp.where(first, o[0:n] - lam * o[n:2 * n], o[2 * n:3 * n] - lam * o[3 * n:4 * n])
    sq = d * d
    ss_a = jnp.sum(jnp.where(first, sq, 0.0), axis=-1, keepdims=True)
    ss_b = jnp.sum(jnp.where(first, 0.0, sq), axis=-1, keepdims=True)
    ms = jnp.where(first, ss_a, ss_b) * (1.0 / HEAD_DIM)
    return d * lax.rsqrt(ms + SUBLN_EPS) * g * (1.0 - lam_init)


def _attn_prompt_kernel(lam_ref, q_ref, k_ref, v_ref, g_ref, o_ref, kb, vb, qx_a, qx_b, s_a, s_b, m_ref, acc_ref,
                        *, blk, lam_init):
    seq = q_ref.shape[0]
    hb = blk // 2
    half = MAPS_PER_SLAB * hb
    rows = 2 * half
    dims = (((1,), (1,)), ((), ()))

    kb[...] = k_ref[...].astype(BF16)
    vb[:, 0:LANES] = v_ref[...].astype(BF16)
    vb[:, LANES:2 * LANES] = jnp.ones(v_ref.shape, BF16)

    steps = [(i, j) for i in range(seq // blk) for j in range(i + 1)]
    qx_bufs = (qx_a, qx_b)
    s_bufs = (s_a, s_b)

    def expand_queries(i):
        qx = qx_bufs[i % 2]
        qx[0:half, :] = _expand_maps(q_ref[i * blk:i * blk + hb, :])
        qx[half:rows, :] = _expand_maps(q_ref[i * blk + hb:(i + 1) * blk, :])

    def scores(n):
        i, j = steps[n]
        s_bufs[n % 2][...] = lax.dot_general(qx_bufs[i % 2][...], kb[j * blk:(j + 1) * blk, :], dims,
                                             preferred_element_type=F32)

    def update(j, src, h, width, row_offset):
        rs = slice(h * half, (h + 1) * half)
        s = src[rs, 0:width]
        if row_offset is not None:
            r = lax.broadcasted_iota(jnp.int32, (half, width), 0)
            c = lax.broadcasted_iota(jnp.int32, (half, width), 1)
            s = jnp.where((row_offset + r % hb) // CHUNK >= c // CHUNK, s, -jnp.inf)
        m_prev = m_ref[rs, :]
        m_new = jnp.maximum(m_prev, jnp.max(s, axis=-1, keepdims=True))
        alpha = jnp.exp2(m_prev - m_new)
        p = jnp.exp2(s - jnp.concatenate([m_new] * (width // LANES), axis=1)).astype(BF16)
        pv = jnp.dot(p, vb[j * blk:j * blk + width, :], preferred_element_type=F32)
        acc_ref[rs, :] = jnp.concatenate([alpha, alpha], axis=1) * acc_ref[rs, :] + pv
        m_ref[rs, :] = m_new

    expand_queries(0)
    scores(0)
    for n, (i, j) in enumerate(steps):
        if n + 1 < len(steps):
            if steps[n + 1][1] == 0:
                expand_queries(steps[n + 1][0])
            scores(n + 1)
        if j == 0:
            m_ref[...] = jnp.full((rows, LANES), -jnp.inf, F32)
            acc_ref[...] = jnp.zeros((rows, 2 * LANES), F32)
        src = s_bufs[n % 2]
        if j < i:
            update(j, src, 0, blk, None)
            update(j, src, 1, blk, None)
        else:
            update(j, src, 0, hb, 0)
            update(j, src, 1, blk, hb)
            for h in range(2):
                acc = acc_ref[h * half:(h + 1) * half, :]
                o = acc[:, 0:LANES] / acc[:, LANES:2 * LANES]
                o_ref[i * blk + h * hb:i * blk + (h + 1) * hb, :] = _diff_combine(
                    o, lam_ref[0, 0], g_ref[...], hb, lam_init).astype(BF16)


def _attn_prompt(lam, q, k, v, g128, blk, lam_init):
    bsz, seq, _ = q.shape
    rows = MAPS_PER_SLAB * blk
    slab_spec = pl.BlockSpec((None, seq, LANES), lambda b, j: (b, 0, j))
    return pl.pallas_call(
        functools.partial(_attn_prompt_kernel, blk=blk, lam_init=lam_init),
        grid=(bsz, ATTN_WIDTH // LANES),
        in_specs=[pl.BlockSpec(memory_space=pltpu.SMEM), slab_spec, slab_spec, slab_spec,
                  pl.BlockSpec((1, LANES), lambda b, j: (0, 0))],
        out_specs=slab_spec,
        out_shape=jax.ShapeDtypeStruct((bsz, seq, ATTN_WIDTH), BF16),
        scratch_shapes=[pltpu.VMEM((seq, LANES), BF16), pltpu.VMEM((seq, 2 * LANES), BF16),
                        pltpu.VMEM((rows, LANES), BF16), pltpu.VMEM((rows, LANES), BF16),
                        pltpu.VMEM((rows, blk), F32), pltpu.VMEM((rows, blk), F32),
                        pltpu.VMEM((rows, LANES), F32), pltpu.VMEM((rows, 2 * LANES), F32)],
        compiler_params=_params(("parallel", "parallel")),
        name="attn_prompt",
    )(lam, q, k, v, g128)


def _attn_sample_kernel(lam_ref, q_ref, kc_ref, vc_ref, kn_ref, vn_ref, g_ref, o_ref, *, n, nseq, lam_init):
    dims = (((1,), (1,)), ((), ()))
    chains = [(b, slice(j * LANES, (j + 1) * LANES)) for b in range(nseq) for j in range(ATTN_WIDTH // LANES)]
    scores = []
    for b, cols in chains:
        qx = _expand_maps(q_ref[b, :, cols])
        scores.append((jnp.dot(qx, kc_ref[b, cols, :].astype(BF16), preferred_element_type=F32),
                       lax.dot_general(qx, kn_ref[b, :, cols].astype(BF16), dims, preferred_element_type=F32)))
    for (b, cols), (s_c, s_n) in zip(chains, scores):
        m = jnp.maximum(jnp.max(s_c, axis=-1, keepdims=True), jnp.max(s_n, axis=-1, keepdims=True))
        p_c = jnp.exp2(s_c - m)
        p_n = jnp.exp2(s_n - m)
        l = jnp.sum(p_c, axis=-1, keepdims=True) + jnp.sum(p_n, axis=-1, keepdims=True)
        acc = (lax.dot_general(p_c.astype(BF16), vc_ref[b, cols, :].astype(BF16), dims, preferred_element_type=F32)
               + jnp.dot(p_n.astype(BF16), vn_ref[b, :, cols].astype(BF16), preferred_element_type=F32))
        o_ref[b, :, cols] = _diff_combine(acc / l, lam_ref[0, 0], g_ref[...], n, lam_init).astype(BF16)


def _attn_sample(lam, q, cache_kt, cache_vt, k_new, v_new, g128, lam_init, nseq=SAMPLE_SEQS_PER_STEP):
    bsz, n, _ = q.shape
    past = cache_kt.shape[2]
    assert past % CHUNK == 0 and n <= CHUNK
    cache_spec = pl.BlockSpec((nseq, ATTN_WIDTH, past), lambda b: (b, 0, 0))
    new_spec = pl.BlockSpec((nseq, n, ATTN_WIDTH), lambda b: (b, 0, 0))
    return pl.pallas_call(
        functools.partial(_attn_sample_kernel, n=n, nseq=nseq, lam_init=lam_init),
        grid=(bsz // nseq,),
        in_specs=[pl.BlockSpec(memory_space=pltpu.SMEM), new_spec, cache_spec, cache_spec, new_spec, new_spec,
                  pl.BlockSpec((1, LANES), lambda b: (0, 0))],
        out_specs=new_spec,
        out_shape=jax.ShapeDtypeStruct((bsz, n, ATTN_WIDTH), BF16),
        compiler_params=_params(("parallel",)),
        name="attn_sample",
    )(lam, q, cache_kt, cache_vt, k_new, v_new, g128)


def _ssm_kernel(u_ref, h0re_ref, h0im_ref, are_ref, aim_ref, bre_ref, bim_ref, cre_ref, cim_ref, d_ref,
                wglu_ref, bglu_ref, out_ref, hre_out, him_out, bure, buim, hsre, hsim, hre, him, res,
                *, tt):
    i = pl.program_id(1)
    dims = (((1,), (1,)), ((), ()))

    @pl.when(i == 0)
    def _():
        for b in range(SUBLANES):
            hre[b:b + 1, :] = _lane_concat_rows(h0re_ref[b])
            him[b:b + 1, :] = _lane_concat_rows(h0im_ref[b])

    ys = []
    for s in range(N_SLABS):
        us = u_ref[s].astype(BF16)
        cols = slice(s * SLAB_STATES, (s + 1) * SLAB_STATES)
        bure[:, cols] = jnp.dot(us, bre_ref[s], preferred_element_type=F32)
        buim[:, cols] = jnp.dot(us, bim_ref[s], preferred_element_type=F32)
        ar = jnp.broadcast_to(are_ref[:, cols], (SUBLANES, SLAB_STATES))
        ai = jnp.broadcast_to(aim_ref[:, cols], (SUBLANES, SLAB_STATES))
        hr = hre[:, cols]
        hi = him[:, cols]
        for t in range(tt):
            r = slice(t * SUBLANES, (t + 1) * SUBLANES)
            hr, hi = ar * hr - ai * hi + bure[r, cols], ar * hi + ai * hr + buim[r, cols]
            hsre[r, cols] = hr
            hsim[r, cols] = hi
        hre[:, cols] = hr
        him[:, cols] = hi
        ys.append(lax.dot_general(hsre[:, cols].astype(BF16), cre_ref[s], dims, preferred_element_type=F32)
                  - lax.dot_general(hsim[:, cols].astype(BF16), cim_ref[s], dims, preferred_element_type=F32)
                  + d_ref[:, s * LANES:(s + 1) * LANES] * u_ref[s])
    y = jnp.concatenate(ys, axis=1)
    z = jnp.dot(jax.nn.gelu(y).astype(BF16), wglu_ref[...], preferred_element_type=F32) + bglu_ref[...]
    gated = z[:, :SSM_WIDTH] * jax.nn.sigmoid(z[:, SSM_WIDTH:])
    for s in range(N_SLABS):
        res[s] = gated[:, s * LANES:(s + 1) * LANES]
    for b in range(SUBLANES):
        for s in range(N_SLABS):
            out_ref[b, :, s * LANES:(s + 1) * LANES] = res[s, pl.ds(b, tt, stride=SUBLANES), :].astype(BF16)

    @pl.when(i == pl.num_programs(1) - 1)
    def _():
        for b in range(SUBLANES):
            for g in range(N_SSM_GROUPS):
                lanes = slice(g * SSM_STATE, (g + 1) * SSM_STATE)
                hre_out[b, g:g + 1, :] = hre[b:b + 1, lanes]
                him_out[b, g:g + 1, :] = him[b:b + 1, lanes]


def _ssm(l, u_t, h0_re, h0_im, lb_re, lb_im, bre, bim, cre, cim, d, w_glu, b_glu, tt):
    nblk, _, rows_total, _ = u_t.shape
    seq = rows_total // SUBLANES
    rows = SUBLANES * tt
    grid = (nblk, seq // tt)
    const = lambda shape: pl.BlockSpec(shape, lambda b, i: (0,) * len(shape))
    layer = lambda *shape: pl.BlockSpec((None,) + shape, lambda b, i: (l,) + (0,) * len(shape))
    slab_spec = const((N_SLABS, LANES, SLAB_STATES))
    state_spec = pl.BlockSpec((SUBLANES, N_SSM_GROUPS, SSM_STATE), lambda b, i: (b, 0, 0))
    state_shape = jax.ShapeDtypeStruct((nblk * SUBLANES, N_SSM_GROUPS, SSM_STATE), F32)
    return pl.pallas_call(
        functools.partial(_ssm_kernel, tt=tt),
        grid=grid,
        in_specs=[pl.BlockSpec((None, N_SLABS, rows, LANES), lambda b, i: (b, 0, i, 0)),
                  state_spec, state_spec,
                  const((1, SSM_LANES)), const((1, SSM_LANES)),
                  slab_spec, slab_spec, slab_spec, slab_spec,
                  const((1, SSM_WIDTH)), const((SSM_WIDTH, 2 * SSM_WIDTH)), layer(1, 2 * SSM_WIDTH)],
        out_specs=(pl.BlockSpec((SUBLANES, tt, SSM_WIDTH), lambda b, i: (b, i, 0)), state_spec, state_spec),
        out_shape=(jax.ShapeDtypeStruct((nblk * SUBLANES, seq, SSM_WIDTH), BF16), state_shape, state_shape),
        scratch_shapes=[pltpu.VMEM((rows, SSM_LANES), F32) for _ in range(4)]
                       + [pltpu.VMEM((SUBLANES, SSM_LANES), F32) for _ in range(2)]
                       + [pltpu.VMEM((N_SLABS, rows, LANES), F32)],
        compiler_params=_params(("parallel", "arbitrary")),
        name="ssm",
    )(u_t, h0_re, h0_im, lb_re, lb_im, bre, bim, cre, cim, d, w_glu, b_glu)


def _layer_norm(x, g, b):
    mu = jnp.mean(x, axis=-1, keepdims=True)
    xc = x - mu
    var = jnp.mean(xc * xc, axis=-1, keepdims=True)
    return xc * lax.rsqrt(var + LN_EPS) * g + b


def _post_kernel(xp_ref, ap_ref, sp_ref, xs_ref, as_ref, ss_ref, wout_ref, g1_ref, b1_ref, wup_ref, wdown_ref,
                 g2_ref, b2_ref, op_ref, os_ref, *, tl, n_prompt, ff_chunk):
    is_prompt = pl.program_id(0) < n_prompt
    tile = functools.partial(_post_tile, wout_ref, g1_ref, b1_ref, wup_ref, wdown_ref, g2_ref, b2_ref,
                             tl=tl, ff_chunk=ff_chunk)

    @pl.when(is_prompt)
    def _():
        tile(xp_ref, ap_ref, sp_ref, op_ref)

    @pl.when(jnp.logical_not(is_prompt))
    def _():
        tile(xs_ref, as_ref, ss_ref, os_ref)


def _post_tile(wout_ref, g1_ref, b1_ref, wup_ref, wdown_ref, g2_ref, b2_ref, x_ref, a_ref, s_ref, o_ref,
               *, tl, ff_chunk):
    chains = (0, 1)
    n = tl // len(chains)
    views = [slice(c * n, (c + 1) * n) for c in chains]
    xs = [x_ref[v, :] for v in views]
    mix = [jnp.dot(jnp.concatenate([a_ref[v, :], s_ref[v, :]], axis=1), wout_ref[...],
                   preferred_element_type=F32) for v in views]
    x1 = [None, None]
    x1b = [None, None]
    h = [None, None]
    ff = [jnp.zeros((n, D_MODEL), F32) for _ in chains]
    n_chunks = D_FF // ff_chunk

    def up(c, k):
        return jnp.dot(x1b[c], wup_ref[:, k * ff_chunk:(k + 1) * ff_chunk], preferred_element_type=F32)

    def down(c, k):
        act = jnp.square(jnp.maximum(h[c], 0.0)).astype(BF16)
        return ff[c] + jnp.dot(act, wdown_ref[k * ff_chunk:(k + 1) * ff_chunk, :], preferred_element_type=F32)

    x1[0] = _layer_norm(DEEPNORM_ALPHA * xs[0] + mix[0], g1_ref[...], b1_ref[...])
    x1b[0] = x1[0].astype(BF16)
    h[0] = up(0, 0)
    x1[1] = _layer_norm(DEEPNORM_ALPHA * xs[1] + mix[1], g1_ref[...], b1_ref[...])
    x1b[1] = x1[1].astype(BF16)
    for k in range(n_chunks):
        h[1] = up(1, k)
        ff[0] = down(0, k)
        if k + 1 < n_chunks:
            h[0] = up(0, k + 1)
        ff[1] = down(1, k)
    for c in chains:
        o_ref[views[c], :] = _layer_norm(DEEPNORM_ALPHA * x1[c] + ff[c], g2_ref[...], b2_ref[...])


def _post(l, prompt, sample, w_out, g1, b1, w_up, w_down, g2, b2, tl):
    bp, seq, _ = prompt[0].shape
    bs, dec_seq, _ = sample[0].shape
    per_seq = seq // tl
    n_prompt = bp * per_seq
    n_sample = bs * dec_seq // tl
    sample = [a.reshape(n_sample, tl, a.shape[-1]) for a in sample]

    def p_spec(width):
        def index(s):
            t = jnp.minimum(s, n_prompt - 1)
            return (t // per_seq, t % per_seq, 0)
        return pl.BlockSpec((None, tl, width), index)

    s_spec = lambda width: pl.BlockSpec((None, tl, width), lambda s: (jnp.maximum(s - n_prompt, 0), 0, 0))
    const = lambda shape: pl.BlockSpec(shape, lambda s: (0,) * len(shape), pipeline_mode=pl.Buffered(1))
    layer = lambda *shape: pl.BlockSpec((None,) + shape, lambda s: (l,) + (0,) * len(shape))
    widths = (D_MODEL, ATTN_WIDTH, SSM_WIDTH)
    yp, ys = pl.pallas_call(
        functools.partial(_post_kernel, tl=tl, n_prompt=n_prompt, ff_chunk=FF_CHUNK),
        grid=(n_prompt + n_sample,),
        in_specs=[p_spec(w) for w in widths] + [s_spec(w) for w in widths]
                 + [const((D_MODEL, D_MODEL)), layer(1, D_MODEL), layer(1, D_MODEL),
                    const((D_MODEL, D_FF)), const((D_FF, D_MODEL)), layer(1, D_MODEL), layer(1, D_MODEL)],
        out_specs=(p_spec(D_MODEL), s_spec(D_MODEL)),
        out_shape=(jax.ShapeDtypeStruct((bp, seq, D_MODEL), F32),
                   jax.ShapeDtypeStruct((n_sample, tl, D_MODEL), F32)),
        compiler_params=_params(("arbitrary",)),
        name="post",
    )(*prompt, *sample, w_out, g1, b1, w_up, w_down, g2, b2)
    return yp, ys.reshape(bs, dec_seq, D_MODEL)


def kernel(x_prompt, x_sample, cache_k, cache_v, state_ssm_re, state_ssm_im, w_in, lambda_q1, lambda_k1,
           lambda_q2, lambda_k2, subln_g, ssm_a_re, ssm_a_im, ssm_log_dt, ssm_b_re, ssm_b_im, ssm_c_re,
           ssm_c_im, ssm_d, w_glu, b_glu, w_out, ln1_g, ln1_b, w_up, w_down, ln2_g, ln2_b):
    assert w_in.shape[0] == DEPTH
    l = 0
    lam_init = _lambda_init(l)
    bp, seq, _ = x_prompt.shape
    bs, dec_seq, _ = x_sample.shape
    past = cache_k.shape[2]
    assert bp % SUBLANES == 0 and bs % SUBLANES == 0 and bs % SAMPLE_SEQS_PER_STEP == 0
    assert seq % ROW_TILE == 0 and seq % ATTN_BLOCK == 0 and SUBLANES * dec_seq == ROW_TILE

    lb_re, lb_im, bre, bim, cre, cim, d, g128, lam = _prep(
        l, ssm_a_re, ssm_a_im, ssm_log_dt, ssm_b_re, ssm_b_im, ssm_c_re, ssm_c_im, ssm_d, subln_g,
        lambda_q1, lambda_k1, lambda_q2, lambda_k2, lam_init)
    w_in_bf, w_glu_bf, w_out_bf, w_up_bf, w_down_bf = (w[l].astype(BF16) for w in (w_in, w_glu, w_out, w_up, w_down))
    row = lambda a: a.reshape(a.shape[0], 1, a.shape[1])
    b_glu3 = row(b_glu)
    ln = [row(a) for a in (ln1_g, ln1_b, ln2_g, ln2_b)]

    def mixers(after, x, positions, cache, h0_re, h0_im, nb, tl, tt):
        bsz, n, _ = x.shape
        q, k, v, u_t = _inproj(after, x, w_in_bf, _rope_tables(positions), nb, tl)
        if cache is None:
            attn = _attn_prompt(lam, q, k, v, g128, ATTN_BLOCK, lam_init)
        else:
            attn = _attn_sample(lam, q, cache[0], cache[1], k, v, g128, lam_init)
        ssm, h_re, h_im = _ssm(l, u_t, h0_re, h0_im, lb_re, lb_im, bre, bim, cre, cim, d, w_glu_bf, b_glu3, tt)
        shape_kv = (1, bsz, n, N_HEADS, HEAD_DIM)
        return attn, ssm, k.reshape(shape_kv), v.reshape(shape_kv), h_re[None], h_im[None]

    cache = tuple(jnp.transpose(c[l], (0, 2, 3, 1)).reshape(bs, ATTN_WIDTH, past) for c in (cache_k, cache_v))
    attn_s, ssm_s, ks, vs, rs, is_ = mixers(lam, x_sample, past + np.arange(dec_seq), cache, state_ssm_re[l],
                                            state_ssm_im[l], nb=SUBLANES, tl=dec_seq, tt=dec_seq)
    zeros = jnp.zeros((bp, N_SSM_GROUPS, SSM_STATE), F32)
    after = ssm_s[0, 0:1, 0:1].astype(F32)
    attn_p, ssm_p, kp, vp, rp, ip = mixers(after, x_prompt, np.arange(seq), None, zeros, zeros,
                                           nb=SUBLANES, tl=2 * ROW_TILE // SUBLANES, tt=SSM_TIME_TILE)
    yp, ys = _post(l, (x_prompt, attn_p, ssm_p), (x_sample, attn_s, ssm_s), w_out_bf, ln[0], ln[1],
                   w_up_bf, w_down_bf, ln[2], ln[3], tl=ROW_TILE)
    return (yp, ys, kp, vp, rp, ip, ks, vs, rs, is_)
```

```python
import functools
import math

import numpy as np
import jax
import jax.numpy as jnp
from jax import lax
from jax.experimental import pallas as pl
from jax.experimental.pallas import tpu as pltpu

D_MODEL = 1024
DEPTH = 1
CHUNK = 64
ATTN_WIDTH = 512
SSM_WIDTH = 512
N_HEADS = 8
HEAD_DIM = 64
HALF_DIM = 32
ROT_DIM = 8
ROPE_THETA = 500000.0
SSM_GROUP = 16
N_SSM_GROUPS = 32
SSM_STATE = 64
D_FF = 4 * D_MODEL
LN_EPS = 1e-5
SUBLN_EPS = 1e-5
DEEPNORM_ALPHA = (2 * DEPTH) ** 0.25
QK_SCALE = HALF_DIM ** -0.5
LOG2_E = math.log2(math.e)

SUBLANES = 8
LANES = 128
SSM_LANES = N_SSM_GROUPS * SSM_STATE
GROUPS_PER_SLAB = LANES // SSM_GROUP
N_SLABS = SSM_WIDTH // LANES
SLAB_STATES = GROUPS_PER_SLAB * SSM_STATE
HEADS_PER_SLAB = LANES // HEAD_DIM
MAPS_PER_SLAB = LANES // HALF_DIM
VMEM_LIMIT_BYTES = 56 * 1024 * 1024

ROW_TILE = 512
ATTN_BLOCK = 512
SSM_TIME_TILE = 64
SAMPLE_SEQS_PER_STEP = 2
FF_CHUNK = 1024
X_RING = 3

F32 = jnp.float32
BF16 = jnp.bfloat16


def _lambda_init(layer_idx):
    return 0.8 - 0.6 * math.exp(-0.3 * layer_idx)


def _params(semantics):
    return pltpu.CompilerParams(dimension_semantics=semantics, vmem_limit_bytes=VMEM_LIMIT_BYTES)


def _lane_concat_rows(x):
    return jnp.concatenate([x[r:r + 1, :] for r in range(x.shape[0])], axis=1)


def _block_diag_slabs(blocks):
    slabs = []
    for s in range(N_SLABS):
        rows = []
        for gi in range(GROUPS_PER_SLAB):
            pieces = []
            if gi > 0:
                pieces.append(jnp.zeros((SSM_GROUP, gi * SSM_STATE), F32))
            pieces.append(blocks[s * GROUPS_PER_SLAB + gi])
            if gi < GROUPS_PER_SLAB - 1:
                pieces.append(jnp.zeros((SSM_GROUP, (GROUPS_PER_SLAB - 1 - gi) * SSM_STATE), F32))
            rows.append(jnp.concatenate(pieces, axis=1))
        slabs.append(jnp.concatenate(rows, axis=0))
    return slabs


def _prep_kernel(are_ref, aim_ref, logdt_ref, bre_ref, bim_ref, cre_ref, cim_ref, d_ref, g_ref,
                 lq1_ref, lk1_ref, lq2_ref, lk2_ref,
                 lbre_ref, lbim_ref, bbre_ref, bbim_ref, ccre_ref, ccim_ref, dd_ref, gg_ref, lam_ref,
                 *, lam_init):
    g = N_SSM_GROUPS
    eye = lax.broadcasted_iota(jnp.int32, (g, g), 0) == lax.broadcasted_iota(jnp.int32, (g, g), 1)
    logdt = jnp.sum(jnp.where(eye, jnp.broadcast_to(logdt_ref[...], (g, g)), 0.0), axis=-1, keepdims=True)
    dt = jnp.exp(logdt)
    ar = are_ref[...]
    ai = aim_ref[...]
    mag = jnp.exp(ar * dt)
    lb_re = mag * jnp.cos(ai * dt)
    lb_im = mag * jnp.sin(ai * dt)
    nr = lb_re - 1.0
    ni = lb_im
    den = ar * ar + ai * ai
    f_re = (nr * ar + ni * ai) / den
    f_im = (ni * ar - nr * ai) / den
    lbre_ref[...] = _lane_concat_rows(lb_re)
    lbim_ref[...] = _lane_concat_rows(lb_im)
    br = bre_ref[...]
    bi = bim_ref[...]
    bb_re = f_re[:, None, :] * br - f_im[:, None, :] * bi
    bb_im = f_re[:, None, :] * bi + f_im[:, None, :] * br
    for out_ref, blocks in ((bbre_ref, bb_re), (bbim_ref, bb_im), (ccre_ref, cre_ref[...]), (ccim_ref, cim_ref[...])):
        for s, slab in enumerate(_block_diag_slabs(blocks)):
            out_ref[s] = slab.astype(BF16)
    dd_ref[...] = _lane_concat_rows(d_ref[...])
    gg_ref[...] = jnp.concatenate([g_ref[...]] * HEADS_PER_SLAB, axis=1)
    s1 = jnp.sum(lq1_ref[...] * lk1_ref[...], axis=-1, keepdims=True)
    s2 = jnp.sum(lq2_ref[...] * lk2_ref[...], axis=-1, keepdims=True)
    lam_ref[...] = jnp.exp(s1) - jnp.exp(s2) + lam_init


def _prep(l, a_re, a_im, log_dt, b_re, b_im, c_re, c_im, d, subln_g, lq1, lk1, lq2, lk2, lam_init):
    g, p, c = N_SSM_GROUPS, SSM_STATE, SSM_GROUP
    layer = lambda *shape: pl.BlockSpec((None,) + shape, lambda: (l,) + (0,) * len(shape))
    row = lambda a: a.reshape(a.shape[0], 1, a.shape[1])
    slab = jax.ShapeDtypeStruct((N_SLABS, LANES, SLAB_STATES), BF16)
    out_shape = (jax.ShapeDtypeStruct((1, SSM_LANES), F32), jax.ShapeDtypeStruct((1, SSM_LANES), F32),
                 slab, slab, slab, slab,
                 jax.ShapeDtypeStruct((1, SSM_WIDTH), F32), jax.ShapeDtypeStruct((1, LANES), F32),
                 jax.ShapeDtypeStruct((1, 1), F32))
    return pl.pallas_call(
        functools.partial(_prep_kernel, lam_init=lam_init),
        in_specs=[layer(g, p), layer(g, p), layer(1, g), layer(g, c, p), layer(g, c, p), layer(g, c, p),
                  layer(g, c, p), layer(g, c), layer(1, HEAD_DIM)] + [layer(1, HALF_DIM)] * 4,
        out_shape=out_shape, name="prep",
    )(a_re, a_im, row(log_dt), jnp.swapaxes(b_re, 2, 3), jnp.swapaxes(b_im, 2, 3), c_re, c_im, d,
      row(subln_g), row(lq1), row(lk1), row(lq2), row(lk2))


def _rope_tables(positions):
    inv = ROPE_THETA ** (-np.arange(0, ROT_DIM, 2, dtype=np.float64) / ROT_DIM)
    ang = np.asarray(positions, np.float64)[:, None] * inv[None, :]
    r = np.arange(LANES) % HALF_DIM
    half = ROT_DIM // 2
    idx = r % half
    cos = np.where(r[None, :] < ROT_DIM, np.cos(ang)[:, idx], 1.0)
    sin = np.sin(ang)[:, idx]
    s_up = np.where(r[None, :] < half, -sin, 0.0)
    s_dn = np.where((r[None, :] >= half) & (r[None, :] < ROT_DIM), sin, 0.0)
    return (jnp.asarray(cos, F32), jnp.asarray(s_up, F32), jnp.asarray(s_dn, F32))


def _inproj_kernel(after_ref, x_hbm, w_ref, cos_ref, sup_ref, sdn_ref, q_ref, k_ref, v_ref, u_ref, xbuf, sems,
                   *, nb, tl):
    del after_ref
    half = ROT_DIM // 2
    n_b = pl.num_programs(1)
    total = pl.num_programs(0) * n_b
    step = pl.program_id(0) * n_b + pl.program_id(1)

    def x_copy(s):
        slot = lax.rem(s, X_RING)
        src = x_hbm.at[pl.ds(lax.rem(s, n_b) * nb, nb), pl.ds(lax.div(s, n_b) * tl, tl), :]
        return pltpu.make_async_copy(src, xbuf.at[slot], sems.at[slot])

    @pl.when(step == 0)
    def _():
        for s in range(X_RING - 1):
            @pl.when(s < total)
            def _():
                x_copy(jnp.int32(s)).start()

    @pl.when(step + (X_RING - 1) < total)
    def _():
        x_copy(step + (X_RING - 1)).start()

    x_copy(step).wait()
    x = xbuf[lax.rem(step, X_RING)].reshape(nb * tl, D_MODEL).astype(BF16)
    cos = jnp.concatenate([cos_ref[...]] * nb, axis=0)
    sup = jnp.concatenate([sup_ref[...]] * nb, axis=0)
    sdn = jnp.concatenate([sdn_ref[...]] * nb, axis=0)

    def rope(t):
        slabs = []
        for j in range(ATTN_WIDTH // LANES):
            s = t[:, j * LANES:(j + 1) * LANES]
            slabs.append(s * cos + pltpu.roll(s, LANES - half, 1) * sup + pltpu.roll(s, half, 1) * sdn)
        return jnp.concatenate(slabs, axis=1)

    q = jnp.dot(x, w_ref[:, 0:ATTN_WIDTH], preferred_element_type=F32)
    q_ref[...] = (rope(q) * (QK_SCALE * LOG2_E)).astype(BF16).reshape(nb, tl, ATTN_WIDTH)
    k = jnp.dot(x, w_ref[:, ATTN_WIDTH:2 * ATTN_WIDTH], preferred_element_type=F32)
    k_ref[...] = rope(k).reshape(nb, tl, ATTN_WIDTH)
    v = jnp.dot(x, w_ref[:, 2 * ATTN_WIDTH:3 * ATTN_WIDTH], preferred_element_type=F32)
    v_ref[...] = v.reshape(nb, tl, ATTN_WIDTH)
    u = jnp.dot(x, w_ref[:, 3 * ATTN_WIDTH:], preferred_element_type=F32)
    first_slot = (pl.program_id(1) * nb) % SUBLANES
    for b in range(nb):
        for s in range(N_SLABS):
            u_ref[s, pl.ds(first_slot + b, tl, stride=SUBLANES), :] = u[b * tl:(b + 1) * tl,
                                                                        s * LANES:(s + 1) * LANES]


def _inproj(after, x, w_bf, tables, nb, tl):
    bsz, seq, _ = x.shape
    nblk = bsz // SUBLANES
    per_blk = SUBLANES // nb
    grid = (seq // tl, bsz // nb)
    tab_spec = pl.BlockSpec((tl, LANES), lambda i, b: (i, 0))
    act_spec = lambda width: pl.BlockSpec((nb, tl, width), lambda i, b: (b, i, 0))
    out_shape = (jax.ShapeDtypeStruct((bsz, seq, ATTN_WIDTH), BF16),
                 jax.ShapeDtypeStruct((bsz, seq, ATTN_WIDTH), F32),
                 jax.ShapeDtypeStruct((bsz, seq, ATTN_WIDTH), F32),
                 jax.ShapeDtypeStruct((nblk, N_SLABS, seq * SUBLANES, LANES), F32))
    return pl.pallas_call(
        functools.partial(_inproj_kernel, nb=nb, tl=tl),
        grid=grid,
        in_specs=[pl.BlockSpec(memory_space=pltpu.SMEM), pl.BlockSpec(memory_space=pl.ANY),
                  pl.BlockSpec((D_MODEL, 4 * ATTN_WIDTH), lambda i, b: (0, 0)),
                  tab_spec, tab_spec, tab_spec],
        out_specs=(act_spec(ATTN_WIDTH), act_spec(ATTN_WIDTH), act_spec(ATTN_WIDTH),
                   pl.BlockSpec((None, N_SLABS, tl * SUBLANES, LANES), lambda i, b: (b // per_blk, 0, i, 0))),
        out_shape=out_shape,
        scratch_shapes=[pltpu.VMEM((X_RING, nb, tl, D_MODEL), F32), pltpu.SemaphoreType.DMA((X_RING,))],
        compiler_params=_params(("arbitrary", "arbitrary")),
        name="inproj",
    )(after, x, w_bf, *tables)


def _expand_maps(q):
    lane = lax.broadcasted_iota(jnp.int32, q.shape, 1)
    zero = jnp.zeros_like(q)
    return jnp.concatenate(
        [jnp.where((lane >= m * HALF_DIM) & (lane < (m + 1) * HALF_DIM), q, zero) for m in range(MAPS_PER_SLAB)],
        axis=0)


def _diff_combine(o, lam, g, n, lam_init):
    lane = lax.broadcasted_iota(jnp.int32, (n, LANES), 1)
    first = lane < HEAD_DIM
    d = jnp.where(first, o[0:n] - lam * o[n:2 * n], o[2 * n:3 * n] - lam * o[3 * n:4 * n])
    sq = d * d
    ss_a = jnp.sum(jnp.where(first, sq, 0.0), axis=-1, keepdims=True)
    ss_b = jnp.sum(jnp.where(first, 0.0, sq), axis=-1, keepdims=True)
    ms = jnp.where(first, ss_a, ss_b) * (1.0 / HEAD_DIM)
    return d * lax.rsqrt(ms + SUBLN_EPS) * g * (1.0 - lam_init)


def _attn_prompt_kernel(lam_ref, q_ref, k_ref, v_ref, g_ref, o_ref, kb, vb, qx_a, qx_b, s_a, s_b, m_ref, acc_ref,
                        *, blk, lam_init):
    seq = q_ref.shape[0]
    hb = blk // 2
    half = MAPS_PER_SLAB * hb
    rows = 2 * half
    dims = (((1,), (1,)), ((), ()))

    kb[...] = k_ref[...].astype(BF16)
    vb[:, 0:LANES] = v_ref[...].astype(BF16)
    vb[:, LANES:2 * LANES] = jnp.ones(v_ref.shape, BF16)

    steps = [(i, j) for i in range(seq // blk) for j in range(i + 1)]
    qx_bufs = (qx_a, qx_b)
    s_bufs = (s_a, s_b)

    def expand_queries(i):
        qx = qx_bufs[i % 2]
        qx[0:half, :] = _expand_maps(q_ref[i * blk:i * blk + hb, :])
        qx[half:rows, :] = _expand_maps(q_ref[i * blk + hb:(i + 1) * blk, :])

    def scores(n):
        i, j = steps[n]
        s_bufs[n % 2][...] = lax.dot_general(qx_bufs[i % 2][...], kb[j * blk:(j + 1) * blk, :], dims,
                                             preferred_element_type=F32)

    def update(j, src, h, width, row_offset):
        rs = slice(h * half, (h + 1) * half)
        s = src[rs, 0:width]
        if row_offset is not None:
            r = lax.broadcasted_iota(jnp.int32, (half, width), 0)
            c = lax.broadcasted_iota(jnp.int32, (half, width), 1)
            s = jnp.where((row_offset + r % hb) // CHUNK >= c // CHUNK, s, -jnp.inf)
        m_prev = m_ref[rs, :]
        m_new = jnp.maximum(m_prev, jnp.max(s, axis=-1, keepdims=True))
        alpha = jnp.exp2(m_prev - m_new)
        p = jnp.exp2(s - jnp.concatenate([m_new] * (width // LANES), axis=1)).astype(BF16)
        pv = jnp.dot(p, vb[j * blk:j * blk + width, :], preferred_element_type=F32)
        acc_ref[rs, :] = jnp.concatenate([alpha, alpha], axis=1) * acc_ref[rs, :] + pv
        m_ref[rs, :] = m_new

    expand_queries(0)
    scores(0)
    for n, (i, j) in enumerate(steps):
        if n + 1 < len(steps):
            if steps[n + 1][1] == 0:
                expand_queries(steps[n + 1][0])
            scores(n + 1)
        if j == 0:
            m_ref[...] = jnp.full((rows, LANES), -jnp.inf, F32)
            acc_ref[...] = jnp.zeros((rows, 2 * LANES), F32)
        src = s_bufs[n % 2]
        if j < i:
            update(j, src, 0, blk, None)
            update(j, src, 1, blk, None)
        else:
            update(j, src, 0, hb, 0)
            update(j, src, 1, blk, hb)
            for h in range(2):
                acc = acc_ref[h * half:(h + 1) * half, :]
                o = acc[:, 0:LANES] / acc[:, LANES:2 * LANES]
                o_ref[i * blk + h * hb:i * blk + (h + 1) * hb, :] = _diff_combine(
                    o, lam_ref[0, 0], g_ref[...], hb, lam_init).astype(BF16)


def _attn_prompt(lam, q, k, v, g128, blk, lam_init):
    bsz, seq, _ = q.shape
    rows = MAPS_PER_SLAB * blk
    slab_spec = pl.BlockSpec((None, seq, LANES), lambda b, j: (b, 0, j))
    return pl.pallas_call(
        functools.partial(_attn_prompt_kernel, blk=blk, lam_init=lam_init),
        grid=(bsz, ATTN_WIDTH // LANES),
        in_specs=[pl.BlockSpec(memory_space=pltpu.SMEM), slab_spec, slab_spec, slab_spec,
                  pl.BlockSpec((1, LANES), lambda b, j: (0, 0))],
        out_specs=slab_spec,
        out_shape=jax.ShapeDtypeStruct((bsz, seq, ATTN_WIDTH), BF16),
        scratch_shapes=[pltpu.VMEM((seq, LANES), BF16), pltpu.VMEM((seq, 2 * LANES), BF16),
                        pltpu.VMEM((rows, LANES), BF16), pltpu.VMEM((rows, LANES), BF16),
                        pltpu.VMEM((rows, blk), F32), pltpu.VMEM((rows, blk), F32),
                        pltpu.VMEM((rows, LANES), F32), pltpu.VMEM((rows, 2 * LANES), F32)],
        compiler_params=_params(("parallel", "parallel")),
        name="attn_prompt",
    )(lam, q, k, v, g128)


def _attn_sample_kernel(lam_ref, q_ref, kc_ref, vc_ref, kn_ref, vn_ref, g_ref, o_ref, *, n, nseq, lam_init):
    dims = (((1,), (1,)), ((), ()))
    chains = [(b, slice(j * LANES, (j + 1) * LANES)) for b in range(nseq) for j in range(ATTN_WIDTH // LANES)]
    scores = []
    for b, cols in chains:
        qx = _expand_maps(q_ref[b, :, cols])
        scores.append((jnp.dot(qx, kc_ref[b, cols, :].astype(BF16), preferred_element_type=F32),
                       lax.dot_general(qx, kn_ref[b, :, cols].astype(BF16), dims, preferred_element_type=F32)))
    for (b, cols), (s_c, s_n) in zip(chains, scores):
        m = jnp.maximum(jnp.max(s_c, axis=-1, keepdims=True), jnp.max(s_n, axis=-1, keepdims=True))
        p_c = jnp.exp2(s_c - m)
        p_n = jnp.exp2(s_n - m)
        l = jnp.sum(p_c, axis=-1, keepdims=True) + jnp.sum(p_n, axis=-1, keepdims=True)
        acc = (lax.dot_general(p_c.astype(BF16), vc_ref[b, cols, :].astype(BF16), dims, preferred_element_type=F32)
               + jnp.dot(p_n.astype(BF16), vn_ref[b, :, cols].astype(BF16), preferred_element_type=F32))
        o_ref[b, :, cols] = _diff_combine(acc / l, lam_ref[0, 0], g_ref[...], n, lam_init).astype(BF16)


def _attn_sample(lam, q, cache_kt, cache_vt, k_new, v_new, g128, lam_init, nseq=SAMPLE_SEQS_PER_STEP):
    bsz, n, _ = q.shape
    past = cache_kt.shape[2]
    assert past % CHUNK == 0 and n <= CHUNK
    cache_spec = pl.BlockSpec((nseq, ATTN_WIDTH, past), lambda b: (b, 0, 0))
    new_spec = pl.BlockSpec((nseq, n, ATTN_WIDTH), lambda b: (b, 0, 0))
    return pl.pallas_call(
        functools.partial(_attn_sample_kernel, n=n, nseq=nseq, lam_init=lam_init),
        grid=(bsz // nseq,),
        in_specs=[pl.BlockSpec(memory_space=pltpu.SMEM), new_spec, cache_spec, cache_spec, new_spec, new_spec,
                  pl.BlockSpec((1, LANES), lambda b: (0, 0))],
        out_specs=new_spec,
        out_shape=jax.ShapeDtypeStruct((bsz, n, ATTN_WIDTH), BF16),
        compiler_params=_params(("parallel",)),
        name="attn_sample",
    )(lam, q, cache_kt, cache_vt, k_new, v_new, g128)


def _ssm_kernel(u_ref, h0re_ref, h0im_ref, are_ref, aim_ref, bre_ref, bim_ref, cre_ref, cim_ref, d_ref,
                wglu_ref, bglu_ref, out_ref, hre_out, him_out, bure, buim, hsre, hsim, hre, him, res,
                *, tt):
    i = pl.program_id(1)
    dims = (((1,), (1,)), ((), ()))

    @pl.when(i == 0)
    def _():
        for b in range(SUBLANES):
            hre[b:b + 1, :] = _lane_concat_rows(h0re_ref[b])
            him[b:b + 1, :] = _lane_concat_rows(h0im_ref[b])

    ys = []
    for s in range(N_SLABS):
        us = u_ref[s].astype(BF16)
        cols = slice(s * SLAB_STATES, (s + 1) * SLAB_STATES)
        bure[:, cols] = jnp.dot(us, bre_ref[s], preferred_element_type=F32)
        buim[:, cols] = jnp.dot(us, bim_ref[s], preferred_element_type=F32)
        ar = jnp.broadcast_to(are_ref[:, cols], (SUBLANES, SLAB_STATES))
        ai = jnp.broadcast_to(aim_ref[:, cols], (SUBLANES, SLAB_STATES))
        hr = hre[:, cols]
        hi = him[:, cols]
        for t in range(tt):
            r = slice(t * SUBLANES, (t + 1) * SUBLANES)
            hr, hi = ar * hr - ai * hi + bure[r, cols], ar * hi + ai * hr + buim[r, cols]
            hsre[r, cols] = hr
            hsim[r, cols] = hi
        hre[:, cols] = hr
        him[:, cols] = hi
        ys.append(lax.dot_general(hsre[:, cols].astype(BF16), cre_ref[s], dims, preferred_element_type=F32)
                  - lax.dot_general(hsim[:, cols].astype(BF16), cim_ref[s], dims, preferred_element_type=F32)
                  + d_ref[:, s * LANES:(s + 1) * LANES] * u_ref[s])
    y = jnp.concatenate(ys, axis=1)
    z = jnp.dot(jax.nn.gelu(y).astype(BF16), wglu_ref[...], preferred_element_type=F32) + bglu_ref[...]
    gated = z[:, :SSM_WIDTH] * jax.nn.sigmoid(z[:, SSM_WIDTH:])
    for s in range(N_SLABS):
        res[s] = gated[:, s * LANES:(s + 1) * LANES]
    for b in range(SUBLANES):
        for s in range(N_SLABS):
            out_ref[b, :, s * LANES:(s + 1) * LANES] = res[s, pl.ds(b, tt, stride=SUBLANES), :].astype(BF16)

    @pl.when(i == pl.num_programs(1) - 1)
    def _():
        for b in range(SUBLANES):
            for g in range(N_SSM_GROUPS):
                lanes = slice(g * SSM_STATE, (g + 1) * SSM_STATE)
                hre_out[b, g:g + 1, :] = hre[b:b + 1, lanes]
                him_out[b, g:g + 1, :] = him[b:b + 1, lanes]


def _ssm(l, u_t, h0_re, h0_im, lb_re, lb_im, bre, bim, cre, cim, d, w_glu, b_glu, tt):
    nblk, _, rows_total, _ = u_t.shape
    seq = rows_total // SUBLANES
    rows = SUBLANES * tt
    grid = (nblk, seq // tt)
    const = lambda shape: pl.BlockSpec(shape, lambda b, i: (0,) * len(shape))
    layer = lambda *shape: pl.BlockSpec((None,) + shape, lambda b, i: (l,) + (0,) * len(shape))
    slab_spec = const((N_SLABS, LANES, SLAB_STATES))
    state_spec = pl.BlockSpec((SUBLANES, N_SSM_GROUPS, SSM_STATE), lambda b, i: (b, 0, 0))
    state_shape = jax.ShapeDtypeStruct((nblk * SUBLANES, N_SSM_GROUPS, SSM_STATE), F32)
    return pl.pallas_call(
        functools.partial(_ssm_kernel, tt=tt),
        grid=grid,
        in_specs=[pl.BlockSpec((None, N_SLABS, rows, LANES), lambda b, i: (b, 0, i, 0)),
                  state_spec, state_spec,
                  const((1, SSM_LANES)), const((1, SSM_LANES)),
                  slab_spec, slab_spec, slab_spec, slab_spec,
                  const((1, SSM_WIDTH)), const((SSM_WIDTH, 2 * SSM_WIDTH)), layer(1, 2 * SSM_WIDTH)],
        out_specs=(pl.BlockSpec((SUBLANES, tt, SSM_WIDTH), lambda b, i: (b, i, 0)), state_spec, state_spec),
        out_shape=(jax.ShapeDtypeStruct((nblk * SUBLANES, seq, SSM_WIDTH), BF16), state_shape, state_shape),
        scratch_shapes=[pltpu.VMEM((rows, SSM_LANES), F32) for _ in range(4)]
                       + [pltpu.VMEM((SUBLANES, SSM_LANES), F32) for _ in range(2)]
                       + [pltpu.VMEM((N_SLABS, rows, LANES), F32)],
        compiler_params=_params(("parallel", "arbitrary")),
        name="ssm",
    )(u_t, h0_re, h0_im, lb_re, lb_im, bre, bim, cre, cim, d, w_glu, b_glu)


def _layer_norm(x, g, b):
    mu = jnp.mean(x, axis=-1, keepdims=True)
    xc = x - mu
    var = jnp.mean(xc * xc, axis=-1, keepdims=True)
    return xc * lax.rsqrt(var + LN_EPS) * g + b


def _post_kernel(xp_ref, ap_ref, sp_ref, xs_ref, as_ref, ss_ref, wout_ref, g1_ref, b1_ref, wup_ref, wdown_ref,
                 g2_ref, b2_ref, op_ref, os_ref, *, tl, n_prompt, ff_chunk):
    is_prompt = pl.program_id(0) < n_prompt
    tile = functools.partial(_post_tile, wout_ref, g1_ref, b1_ref, wup_ref, wdown_ref, g2_ref, b2_ref,
                             tl=tl, ff_chunk=ff_chunk)

    @pl.when(is_prompt)
    def _():
        tile(xp_ref, ap_ref, sp_ref, op_ref)

    @pl.when(jnp.logical_not(is_prompt))
    def _():
        tile(xs_ref, as_ref, ss_ref, os_ref)


def _post_tile(wout_ref, g1_ref, b1_ref, wup_ref, wdown_ref, g2_ref, b2_ref, x_ref, a_ref, s_ref, o_ref,
               *, tl, ff_chunk):
    chains = (0, 1)
    n = tl // len(chains)
    views = [slice(c * n, (c + 1) * n) for c in chains]
    xs = [x_ref[v, :] for v in views]
    mix = [jnp.dot(jnp.concatenate([a_ref[v, :], s_ref[v, :]], axis=1), wout_ref[...],
                   preferred_element_type=F32) for v in views]
    x1 = [None, None]
    x1b = [None, None]
    h = [None, None]
    ff = [jnp.zeros((n, D_MODEL), F32) for _ in chains]
    n_chunks = D_FF // ff_chunk

    def up(c, k):
        return jnp.dot(x1b[c], wup_ref[:, k * ff_chunk:(k + 1) * ff_chunk], preferred_element_type=F32)

    def down(c, k):
        act = jnp.square(jnp.maximum(h[c], 0.0)).astype(BF16)
        return ff[c] + jnp.dot(act, wdown_ref[k * ff_chunk:(k + 1) * ff_chunk, :], preferred_element_type=F32)

    x1[0] = _layer_norm(DEEPNORM_ALPHA * xs[0] + mix[0], g1_ref[...], b1_ref[...])
    x1b[0] = x1[0].astype(BF16)
    h[0] = up(0, 0)
    x1[1] = _layer_norm(DEEPNORM_ALPHA * xs[1] + mix[1], g1_ref[...], b1_ref[...])
    x1b[1] = x1[1].astype(BF16)
    for k in range(n_chunks):
        h[1] = up(1, k)
        ff[0] = down(0, k)
        if k + 1 < n_chunks:
            h[0] = up(0, k + 1)
        ff[1] = down(1, k)
    for c in chains:
        o_ref[views[c], :] = _layer_norm(DEEPNORM_ALPHA * x1[c] + ff[c], g2_ref[...], b2_ref[...])


def _post(l, prompt, sample, w_out, g1, b1, w_up, w_down, g2, b2, tl):
    bp, seq, _ = prompt[0].shape
    bs, dec_seq, _ = sample[0].shape
    per_seq = seq // tl
    n_prompt = bp * per_seq
    n_sample = bs * dec_seq // tl
    sample = [a.reshape(n_sample, tl, a.shape[-1]) for a in sample]

    def p_spec(width):
        def index(s):
            t = jnp.minimum(s, n_prompt - 1)
            return (t // per_seq, t % per_seq, 0)
        return pl.BlockSpec((None, tl, width), index)

    s_spec = lambda width: pl.BlockSpec((None, tl, width), lambda s: (jnp.maximum(s - n_prompt, 0), 0, 0))
    const = lambda shape: pl.BlockSpec(shape, lambda s: (0,) * len(shape), pipeline_mode=pl.Buffered(1))
    layer = lambda *shape: pl.BlockSpec((None,) + shape, lambda s: (l,) + (0,) * len(shape))
    widths = (D_MODEL, ATTN_WIDTH, SSM_WIDTH)
    yp, ys = pl.pallas_call(
        functools.partial(_post_kernel, tl=tl, n_prompt=n_prompt, ff_chunk=FF_CHUNK),
        grid=(n_prompt + n_sample,),
        in_specs=[p_spec(w) for w in widths] + [s_spec(w) for w in widths]
                 + [const((D_MODEL, D_MODEL)), layer(1, D_MODEL), layer(1, D_MODEL),
                    const((D_MODEL, D_FF)), const((D_FF, D_MODEL)), layer(1, D_MODEL), layer(1, D_MODEL)],
        out_specs=(p_spec(D_MODEL), s_spec(D_MODEL)),
        out_shape=(jax.ShapeDtypeStruct((bp, seq, D_MODEL), F32),
                   jax.ShapeDtypeStruct((n_sample, tl, D_MODEL), F32)),
        compiler_params=_params(("arbitrary",)),
        name="post",
    )(*prompt, *sample, w_out, g1, b1, w_up, w_down, g2, b2)
    return yp, ys.reshape(bs, dec_seq, D_MODEL)


def kernel(x_prompt, x_sample, cache_k, cache_v, state_ssm_re, state_ssm_im, w_in, lambda_q1, lambda_k1,
           lambda_q2, lambda_k2, subln_g, ssm_a_re, ssm_a_im, ssm_log_dt, ssm_b_re, ssm_b_im, ssm_c_re,
           ssm_c_im, ssm_d, w_glu, b_glu, w_out, ln1_g, ln1_b, w_up, w_down, ln2_g, ln2_b):
    assert w_in.shape[0] == DEPTH
    l = 0
    lam_init = _lambda_init(l)
    bp, seq, _ = x_prompt.shape
    bs, dec_seq, _ = x_sample.shape
    past = cache_k.shape[2]
    assert bp % SUBLANES == 0 and bs % SUBLANES == 0 and bs % SAMPLE_SEQS_PER_STEP == 0
    assert seq % ROW_TILE == 0 and seq % ATTN_BLOCK == 0 and SUBLANES * dec_seq == ROW_TILE

    lb_re, lb_im, bre, bim, cre, cim, d, g128, lam = _prep(
        l, ssm_a_re, ssm_a_im, ssm_log_dt, ssm_b_re, ssm_b_im, ssm_c_re, ssm_c_im, ssm_d, subln_g,
        lambda_q1, lambda_k1, lambda_q2, lambda_k2, lam_init)
    w_in_bf, w_glu_bf, w_out_bf, w_up_bf, w_down_bf = (w[l].astype(BF16) for w in (w_in, w_glu, w_out, w_up, w_down))
    row = lambda a: a.reshape(a.shape[0], 1, a.shape[1])
    b_glu3 = row(b_glu)
    ln = [row(a) for a in (ln1_g, ln1_b, ln2_g, ln2_b)]

    def mixers(after, x, positions, cache, h0_re, h0_im, nb, tl, tt):
        bsz, n, _ = x.shape
        q, k, v, u_t = _inproj(after, x, w_in_bf, _rope_tables(positions), nb, tl)
        if cache is None:
            attn = _attn_prompt(lam, q, k, v, g128, ATTN_BLOCK, lam_init)
        else:
            attn = _attn_sample(lam, q, cache[0], cache[1], k, v, g128, lam_init)
        ssm, h_re, h_im = _ssm(l, u_t, h0_re, h0_im, lb_re, lb_im, bre, bim, cre, cim, d, w_glu_bf, b_glu3, tt)
        shape_kv = (1, bsz, n, N_HEADS, HEAD_DIM)
        return attn, ssm, k.reshape(shape_kv), v.reshape(shape_kv), h_re[None], h_im[None]

    cache = tuple(jnp.transpose(c[l], (0, 2, 3, 1)).reshape(bs, ATTN_WIDTH, past) for c in (cache_k, cache_v))
    attn_s, ssm_s, ks, vs, rs, is_ = mixers(lam, x_sample, past + np.arange(dec_seq), cache, state_ssm_re[l],
                                            state_ssm_im[l], nb=SUBLANES, tl=dec_seq, tt=dec_seq)
    zeros = jnp.zeros((bp, N_SSM_GROUPS, SSM_STATE), F32)
    after = ssm_s[0, 0:1, 0:1].astype(F32)
    attn_p, ssm_p, kp, vp, rp, ip = mixers(after, x_prompt, np.arange(seq), None, zeros, zeros,
                                           nb=SUBLANES, tl=2 * ROW_TILE // SUBLANES, tt=SSM_TIME_TILE)
    yp, ys = _post(l, (x_prompt, attn_p, ssm_p), (x_sample, attn_s, ssm_s), w_out_bf, ln[0], ln[1],
                   w_up_bf, w_down_bf, ln[2], ln[3], tl=ROW_TILE)
    return (yp, ys, kp, vp, rp, ip, ks, vs, rs, is_)
```
